```python
import math
import jax, jax.numpy as jnp
from jax import lax
import numpy as np

D_MODEL = 2048
BATCH = 1
SEQ = 8192
DEPTH = 1
DEC_BATCH = 128
DEC_SEQ = 8
PAST_LEN = 2048
PAGE_SIZE = 128

N_HEADS = 8
HEAD_DIM = 128
ATT_WIDTH = N_HEADS * HEAD_DIM
MOBA_BLOCK = 256
MOBA_TOPK = 3
MOBA_Q_CHUNK = 64
GM_GROUPS = 8
GM_CHUNK = 128
GM_WIDTH = 1024
GM_GROUP_DIM = GM_WIDTH // GM_GROUPS
NUM_BUCKETS = 32
MAX_DISTANCE = 128
D_FF = -(-8 * D_MODEL // (3 * 256)) * 256
N_IN = 3 * ATT_WIDTH + 2 * GM_WIDTH + 2 * D_MODEL
EPS = 1e-6

kernel_name = "moba_gmlp_gated_hybrid_step"


def rms_norm(x, g):
    xf = x.astype(jnp.float32)
    y = xf * lax.rsqrt(jnp.mean(xf * xf, axis=-1, keepdims=True) + EPS)
    return (y * g.astype(jnp.float32)).astype(x.dtype)


def layer_norm(x, g, b):
    xf = x.astype(jnp.float32)
    mu = jnp.mean(xf, axis=-1, keepdims=True)
    xc = xf - mu
    y = xc * lax.rsqrt(jnp.mean(xc * xc, axis=-1, keepdims=True) + EPS)
    return (y * g.astype(jnp.float32) + b.astype(jnp.float32)).astype(x.dtype)


def t5_bucket(rel):
    n = jnp.maximum(rel, 0)
    max_exact = NUM_BUCKETS // 2
    nf = jnp.maximum(n, 1).astype(jnp.float32)
    large = max_exact + (jnp.log(nf / max_exact) / math.log(MAX_DISTANCE / max_exact)
                         * (NUM_BUCKETS - max_exact)).astype(jnp.int32)
    large = jnp.minimum(large, NUM_BUCKETS - 1)
    return jnp.where(n < max_exact, n, large)


def to_blocks(t):
    L = t.shape[0]
    nb = -(-L // MOBA_BLOCK)
    t = jnp.pad(t, ((0, nb * MOBA_BLOCK - L), (0, 0), (0, 0)))
    return t.reshape(nb, MOBA_BLOCK, N_HEADS, HEAD_DIM).transpose(2, 0, 1, 3)


def moba_query_block(q, qpos, j, kb, vb, kbar, bias_t):
    H, Q, dh = q.shape
    nb = kb.shape[1]
    k_eff = min(MOBA_TOPK, nb)
    scale = HEAD_DIM ** -0.5
    qf = q.astype(jnp.float32)
    blk_scores = jnp.einsum('hqd,hnd->hqn', qf, kbar)
    blk_scores = jnp.where((jnp.arange(nb) < j)[None, None, :], blk_scores, -jnp.inf)
    _, top_idx = lax.top_k(blk_scores, k_eff)
    valid = jnp.arange(k_eff) < j
    h_idx = jnp.arange(H)[:, None, None]
    k_sel = kb[h_idx, top_idx]
    v_sel = vb[h_idx, top_idx]
    offs = jnp.arange(MOBA_BLOCK)
    rel_sel = qpos[None, :, None, None] - (top_idx[..., None] * MOBA_BLOCK + offs)
    bias_sel = bias_t[h_idx[..., None], t5_bucket(rel_sel)].astype(jnp.float32)
    logit_sel = jnp.einsum('hqd,hqkpd->hqkp', qf, k_sel.astype(jnp.float32)) * scale + bias_sel
    logit_sel = jnp.where(valid[None, None, :, None], logit_sel, -jnp.inf)
    logit_sel = logit_sel.reshape(H, Q, k_eff * MOBA_BLOCK)
    k_own = lax.dynamic_index_in_dim(kb, j, axis=1, keepdims=False)
    v_own = lax.dynamic_index_in_dim(vb, j, axis=1, keepdims=False)
    rel_own = qpos[:, None] - (j * MOBA_BLOCK + offs)[None, :]
    bias_own = bias_t[:, t5_bucket(rel_own)].astype(jnp.float32)
    logit_own = jnp.einsum('hqd,hpd->hqp', qf, k_own.astype(jnp.float32)) * scale + bias_own
    logit_own = jnp.where((rel_own >= 0)[None], logit_own, -jnp.inf)
    probs = jax.nn.softmax(jnp.concatenate([logit_sel, logit_own], axis=-1), axis=-1)
    p_sel = probs[..., :k_eff * MOBA_BLOCK].reshape(H, Q, k_eff, MOBA_BLOCK)
    p_own = probs[..., k_eff * MOBA_BLOCK:]
    out = (jnp.einsum('hqkp,hqkpd->hqd', p_sel, v_sel.astype(jnp.float32))
           + jnp.einsum('hqp,hpd->hqd', p_own, v_own.astype(jnp.float32)))
    return out.astype(q.dtype)


def moba_prompt_seq(q, k, v, bias_t):
    kb, vb = to_blocks(k), to_blocks(v)
    kbar = jnp.mean(kb.astype(jnp.float32), axis=2)
    S = q.shape[0]
    n_chunks = S // MOBA_Q_CHUNK
    qc = q.reshape(n_chunks, MOBA_Q_CHUNK, N_HEADS, HEAD_DIM).transpose(0, 2, 1, 3)

    def body(args):
        q_blk, c = args
        start = c * MOBA_Q_CHUNK
        qpos = start + jnp.arange(MOBA_Q_CHUNK, dtype=jnp.int32)
        return moba_query_block(q_blk, qpos, start // MOBA_BLOCK, kb, vb, kbar, bias_t)

    out = lax.map(body, (qc, jnp.arange(n_chunks, dtype=jnp.int32)))
    return out.transpose(0, 2, 1, 3).reshape(S, ATT_WIDTH)


def moba_sample_seq(q, k_new, v_new, pages, cache_k, cache_v, bias_t):
    past_len = pages.shape[0] * cache_k.shape[1]
    k_past = cache_k[pages].reshape(past_len, N_HEADS, HEAD_DIM)
    v_past = cache_v[pages].reshape(past_len, N_HEADS, HEAD_DIM)
    kb = to_blocks(jnp.concatenate([k_past, k_new], axis=0))
    vb = to_blocks(jnp.concatenate([v_past, v_new], axis=0))
    kbar = jnp.mean(kb.astype(jnp.float32), axis=2)
    T = q.shape[0]
    qpos = past_len + jnp.arange(T, dtype=jnp.int32)
    out = moba_query_block(q.transpose(1, 0, 2), qpos, past_len // MOBA_BLOCK, kb, vb, kbar, bias_t)
    return out.transpose(1, 0, 2).reshape(T, ATT_WIDTH)


def gm_weights(gm_ws):
    tril = jnp.tril(jnp.ones((GM_CHUNK, GM_CHUNK), dtype=bool))
    return jnp.where(tril[None], gm_ws, jnp.zeros_like(gm_ws))


def gm_spatial_prompt(vn, gm_ws, gm_bs):
    B, S, _ = vn.shape
    nc = S // GM_CHUNK
    vg = vn.reshape(B, nc, GM_CHUNK, GM_GROUPS, GM_GROUP_DIM)
    s = jnp.einsum('gts,bcsgd->bctgd', gm_weights(gm_ws), vg) + gm_bs.T[None, None, :, :, None]
    return s.reshape(B, S, GM_WIDTH)


def gm_spatial_sample(vn, gm_ws, gm_bs):
    B, T, _ = vn.shape
    vg = vn.reshape(B, T, GM_GROUPS, GM_GROUP_DIM)
    w = gm_weights(gm_ws)[:, :T, :T]
    s = jnp.einsum('gts,bsgd->btgd', w, vg) + gm_bs[:, :T].T[None, :, :, None]
    return s.reshape(B, T, GM_WIDTH)


def decoder_layer(x, c, lw, bias_t, cache_k, cache_v, page_table):
    (w_ada, b_ada, norm1_g, w_in, q_norm_g, k_norm_g, gm_ln_g, gm_ln_b, gm_ws, gm_bs,
     w_pa, w_pb, w_o, norm2_g, w_gu, w_down) = lw
    B, T, _ = x.shape
    mod = (jax.nn.silu(c) @ w_ada + b_ada).reshape(B, 6, D_MODEL)
    sh1, sc1, gt1, sh2, sc2, gt2 = [mod[:, i][:, None, :] for i in range(6)]
    h = rms_norm(x, norm1_g) * (1 + sc1) + sh1
    z = h @ w_in
    cuts = [ATT_WIDTH, 2 * ATT_WIDTH, 3 * ATT_WIDTH, 3 * ATT_WIDTH + GM_WIDTH,
            3 * ATT_WIDTH + 2 * GM_WIDTH, 3 * ATT_WIDTH + 2 * GM_WIDTH + D_MODEL]
    q, k, v, u_gm, v_gm, ga, gb = jnp.split(z, cuts, axis=-1)
    q = rms_norm(q.reshape(B, T, N_HEADS, HEAD_DIM), q_norm_g)
    k = rms_norm(k.reshape(B, T, N_HEADS, HEAD_DIM), k_norm_g)
    v = v.reshape(B, T, N_HEADS, HEAD_DIM)
    u_gm = jax.nn.gelu(u_gm)
    vn = layer_norm(jax.nn.gelu(v_gm), gm_ln_g, gm_ln_b)
    if cache_k is None:
        att = lax.map(lambda a: moba_prompt_seq(a[0], a[1], a[2], bias_t), (q, k, v))
        s = gm_spatial_prompt(vn, gm_ws, gm_bs)
    else:
        att = lax.map(lambda a: moba_sample_seq(a[0], a[1], a[2], a[3], cache_k, cache_v, bias_t),
                      (q, k, v, page_table))
        s = gm_spatial_sample(vn, gm_ws, gm_bs)
    gm = u_gm * s
    merged = jax.nn.sigmoid(ga) * (gm @ w_pa) + jax.nn.sigmoid(gb) * (att @ w_pb)
    x = x + gt1 * (merged @ w_o)
    h2 = rms_norm(x, norm2_g) * (1 + sc2) + sh2
    g_ff, u_ff = jnp.split(h2 @ w_gu, 2, axis=-1)
    x = x + gt2 * ((jax.nn.silu(g_ff) * u_ff) @ w_down)
    return x, k, v, vn


def setup_inputs(seed: int = 0) -> dict:
    key = jax.random.key(seed)
    ks = jax.random.split(key, 32)
    n_pages = PAST_LEN // PAGE_SIZE
    n_used = DEC_BATCH * n_pages
    n_phys = n_used + max(1, n_used // 4)
    nrm = lambda k, shape, s: jax.random.normal(k, shape, jnp.float32) * s
    page_table = jax.random.permutation(ks[0], n_phys)[:n_used].reshape(DEC_BATCH, n_pages).astype(jnp.int32)
    return {
        "x_prompt": nrm(ks[1], (BATCH, SEQ, D_MODEL), 1.0),
        "x_sample": nrm(ks[2], (DEC_BATCH, DEC_SEQ, D_MODEL), 1.0),
        "c_prompt": nrm(ks[3], (BATCH, D_MODEL), 1.0),
        "c_sample": nrm(ks[4], (DEC_BATCH, D_MODEL), 1.0),
        "cache_k": nrm(ks[5], (DEPTH, n_phys, PAGE_SIZE, N_HEADS, HEAD_DIM), 1.0),
        "cache_v": nrm(ks[6], (DEPTH, n_phys, PAGE_SIZE, N_HEADS, HEAD_DIM), 1.0),
        "page_table": page_table,
        "w_ada": nrm(ks[7], (DEPTH, D_MODEL, 6 * D_MODEL), 0.5 * D_MODEL ** -0.5),
        "b_ada": nrm(ks[8], (DEPTH, 6 * D_MODEL), 0.1),
        "norm1_g": 1.0 + nrm(ks[9], (DEPTH, D_MODEL), 0.1),
        "w_in": nrm(ks[10], (DEPTH, D_MODEL, N_IN), D_MODEL ** -0.5),
        "q_norm_g": 1.0 + nrm(ks[11], (DEPTH, HEAD_DIM), 0.1),
        "k_norm_g": 1.0 + nrm(ks[12], (DEPTH, HEAD_DIM), 0.1),
        "gm_ln_g": 1.0 + nrm(ks[13], (DEPTH, GM_WIDTH), 0.1),
        "gm_ln_b": nrm(ks[14], (DEPTH, GM_WIDTH), 0.1),
        "gm_ws": nrm(ks[15], (DEPTH, GM_GROUPS, GM_CHUNK, GM_CHUNK), 0.5 * GM_CHUNK ** -0.5),
        "gm_bs": 1.0 + nrm(ks[16], (DEPTH, GM_GROUPS, GM_CHUNK), 0.1),
        "w_pa": nrm(ks[17], (DEPTH, GM_WIDTH, D_MODEL), GM_WIDTH ** -0.5),
        "w_pb": nrm(ks[18], (DEPTH, ATT_WIDTH, D_MODEL), ATT_WIDTH ** -0.5),
        "w_o": nrm(ks[19], (DEPTH, D_MODEL, D_MODEL), D_MODEL ** -0.5),
        "norm2_g": 1.0 + nrm(ks[20], (DEPTH, D_MODEL), 0.1),
        "w_gu": nrm(ks[21], (DEPTH, D_MODEL, 2 * D_FF), D_MODEL ** -0.5),
        "w_down": nrm(ks[22], (DEPTH, D_FF, D_MODEL), D_FF ** -0.5),
        "rel_bias": nrm(ks[23], (NUM_BUCKETS, N_HEADS), 0.5),
    }


def reference(x_prompt, x_sample, c_prompt, c_sample, cache_k, cache_v, page_table,
              w_ada, b_ada, norm1_g, w_in, q_norm_g, k_norm_g, gm_ln_g, gm_ln_b, gm_ws, gm_bs,
              w_pa, w_pb, w_o, norm2_g, w_gu, w_down, rel_bias):
    bias_t = rel_bias.T
    y_p, y_s = x_prompt, x_sample
    kp, vp, ksm, vsm, gms = [], [], [], [], []
    for l in range(DEPTH):
        lw = (w_ada[l], b_ada[l], norm1_g[l], w_in[l], q_norm_g[l], k_norm_g[l], gm_ln_g[l], gm_ln_b[l],
              gm_ws[l], gm_bs[l], w_pa[l], w_pb[l], w_o[l], norm2_g[l], w_gu[l], w_down[l])
        y_p, k1, v1, _ = decoder_layer(y_p, c_prompt, lw, bias_t, None, None, None)
        y_s, k2, v2, g2 = decoder_layer(y_s, c_sample, lw, bias_t, cache_k[l], cache_v[l], page_table)
        kp.append(k1); vp.append(v1); ksm.append(k2); vsm.append(v2); gms.append(g2)
    k_prompt = jnp.stack(kp)
    v_prompt = jnp.stack(vp)
    k_sample = jnp.stack(ksm)
    v_sample = jnp.stack(vsm)
    gm_v_sample = jnp.stack(gms)
    return (y_p, y_s, k_prompt, v_prompt, k_sample, v_sample, gm_v_sample)
```

```python
import functools
import math

import jax
import jax.numpy as jnp
from jax import lax
from jax.experimental import pallas as pl
from jax.experimental.pallas import tpu as pltpu

F32 = jnp.float32
BF16 = jnp.bfloat16

D_MODEL = 2048
N_HEADS = 8
HEAD_DIM = 128
ATT_WIDTH = N_HEADS * HEAD_DIM
GM_WIDTH = 1024
GM_GROUPS = 8
GM_CHUNK = 128
MOBA_BLOCK = 256
MOBA_TOPK = 3
NUM_BUCKETS = 32
MAX_DISTANCE = 128
PAGE_SIZE = 128
D_FF = 5632
EPS = 1e-6
SCALE = HEAD_DIM ** -0.5

V7X_LANES = 128
V7X_SUBLANES = 8
V7X_VMEM_BYTES = 64 * 1024 * 1024
VMEM_LIMIT = V7X_VMEM_BYTES - 8 * 1024 * 1024

NEG = -1e30
SEG = 1024

NT_DIMS = (((1,), (1,)), ((), ()))


def _cparams(n_axes, vmem=VMEM_LIMIT):
    return pltpu.CompilerParams(dimension_semantics=("arbitrary",) * n_axes, vmem_limit_bytes=vmem)


def _sigmoid(x):
    return 1.0 / (1.0 + jnp.exp(-x))


def _gelu_tanh(x):
    c = math.sqrt(2.0 / math.pi)
    return x * (0.5 * (1.0 + jnp.tanh(c * (x + 0.044715 * (x * x * x)))))


def _t5_bucket(rel):
    n = jnp.maximum(rel, 0)
    max_exact = NUM_BUCKETS // 2
    nf = jnp.maximum(n, 1).astype(F32)
    large = max_exact + (jnp.log(nf / max_exact) / math.log(MAX_DISTANCE / max_exact)
                         * (NUM_BUCKETS - max_exact)).astype(jnp.int32)
    large = jnp.minimum(large, NUM_BUCKETS - 1)
    return jnp.where(n < max_exact, n, large)


def _mod_kernel(c_ref, w_ref, b_ref, o_ref):
    c = c_ref[...]
    s = (c * _sigmoid(c)).astype(BF16)
    o_ref[...] = jnp.dot(s, w_ref[...].astype(BF16), preferred_element_type=F32) + b_ref[...]


def _modulation(c_all, w_ada, b_ada):
    rows = c_all.shape[0]
    n_out = w_ada.shape[1]
    tn = 1024
    return pl.pallas_call(
        _mod_kernel,
        out_shape=jax.ShapeDtypeStruct((rows, n_out), F32),
        grid=(n_out // tn,),
        in_specs=[pl.BlockSpec((rows, D_MODEL), lambda j: (0, 0)),
                  pl.BlockSpec((D_MODEL, tn), lambda j: (0, j)),
                  pl.BlockSpec((1, tn), lambda j: (0, j))],
        out_specs=pl.BlockSpec((rows, tn), lambda j: (0, j)),
        compiler_params=_cparams(1),
        name="mod",
    )(c_all, w_ada, b_ada.reshape(1, n_out))


_SEG_K, _SEG_Q, _SEG_V, _SEG_U, _SEG_VG, _SEG_GA0, _SEG_GA1, _SEG_GB0, _SEG_GB1 = range(9)


def _inproj_kernel(x_ref, sh_ref, sc_ref, g1_ref, w_ref, qg_ref, kg_ref, lng_ref, lnb_ref,
                   *rest, tm, prompt):
    if prompt:
        q32, k16, v16, k_out, v_out, u16, vn16, ga16, gb16, kbar, h_scr = rest
        k32 = v32 = None
    else:
        q32, k32, v32, k_out, v_out, u16, vn16, vn32, ga16, gb16, h_scr = rest
        k16 = v16 = None
    j = pl.program_id(1)

    @pl.when(j == 0)
    def _():
        x = x_ref[...]
        ms = jnp.mean(x * x, axis=-1, keepdims=True)
        y = x * lax.rsqrt(ms + EPS) * g1_ref[...]
        h = y * (1.0 + sc_ref[...]) + sh_ref[...]
        h_scr[...] = h.reshape(tm, D_MODEL).astype(BF16)

    if prompt:
        @pl.when(j == 0)
        def _():
            kbar[...] = jnp.zeros_like(kbar)

    z = jnp.dot(h_scr[...], w_ref[...], preferred_element_type=F32)

    def store_heads(y_of_head, dst32, dst16, dst_out, with_means):
        for h in range(N_HEADS):
            sl = slice(h * HEAD_DIM, (h + 1) * HEAD_DIM)
            yh = y_of_head(z[:, sl])
            if dst32 is not None:
                dst32[:, sl] = yh
            if dst16 is not None:
                dst16[:, sl] = yh.astype(BF16)
            if dst_out is not None:
                dst_out[pl.ds(h, tm, stride=N_HEADS), :] = yh
            if with_means:
                for r in range(tm // MOBA_BLOCK):
                    blk = yh[r * MOBA_BLOCK:(r + 1) * MOBA_BLOCK]
                    kbar[r:r + 1, sl] = jnp.mean(blk, axis=0, keepdims=True)

    def head_norm(g_ref):
        def f(zh):
            ms = jnp.mean(zh * zh, axis=-1, keepdims=True)
            return zh * lax.rsqrt(ms + EPS) * g_ref[...]
        return f

    @pl.when(j == _SEG_K)
    def _():
        store_heads(head_norm(kg_ref), k32, k16, k_out, prompt)

    @pl.when(j == _SEG_Q)
    def _():
        store_heads(head_norm(qg_ref), q32, None, None, False)

    @pl.when(j == _SEG_V)
    def _():
        store_heads(lambda zh: zh, v32, v16, v_out, False)

    @pl.when(j == _SEG_U)
    def _():
        u16[...] = _gelu_tanh(z).astype(BF16)

    @pl.when(j == _SEG_VG)
    def _():
        a = _gelu_tanh(z)
        mu = jnp.mean(a, axis=-1, keepdims=True)
        ac = a - mu
        y = ac * lax.rsqrt(jnp.mean(ac * ac, axis=-1, keepdims=True) + EPS)
        y = y * lng_ref[...] + lnb_ref[...]
        vn16[...] = y.astype(BF16)
        if not prompt:
            vn32[...] = y

    @pl.when((j == _SEG_GA0) | (j == _SEG_GA1))
    def _():
        ga16[...] = _sigmoid(z).astype(BF16)

    @pl.when((j == _SEG_GB0) | (j == _SEG_GB1))
    def _():
        gb16[...] = _sigmoid(z).astype(BF16)


def _mod_spec(g_rows, chunk, prompt, n_grid):
    if n_grid == 2:
        if prompt:
            return pl.BlockSpec((1, 1, D_MODEL), lambda i, j: (_PROMPT_MOD_ROW, 0, chunk))
        return pl.BlockSpec((g_rows, 1, D_MODEL), lambda i, j: (i, 0, chunk))
    if prompt:
        return pl.BlockSpec((1, 1, D_MODEL), lambda i: (_PROMPT_MOD_ROW, 0, chunk))
    return pl.BlockSpec((g_rows, 1, D_MODEL), lambda i: (i, 0, chunk))


_PROMPT_MOD_ROW = 128


def _w_in_col(j):
    return jnp.where(j < 2, 1 - j, j)


def _inproj(x3, mod3, g1, w_in16, qg, kg, lng, lnb, *, prompt, tm):
    n_rows = x3.shape[0] * V7X_SUBLANES
    g_rows = tm // V7X_SUBLANES
    grid = (n_rows // tm, 9)
    row_blk = lambda i, j: (i, 0)
    f32_seg = jax.ShapeDtypeStruct((n_rows, SEG), F32)
    b16_seg = jax.ShapeDtypeStruct((n_rows, SEG), BF16)
    b16_gate = jax.ShapeDtypeStruct((n_rows, D_MODEL), BF16)
    seg_spec = pl.BlockSpec((tm, SEG), row_blk)
    kv_out = jax.ShapeDtypeStruct((n_rows * N_HEADS, HEAD_DIM), F32)
    kv_spec = pl.BlockSpec((tm * N_HEADS, HEAD_DIM), row_blk)
    ga_spec = pl.BlockSpec((tm, SEG), lambda i, j: (i, jnp.clip(j - _SEG_GA0, 0, 1)))
    gb_spec = pl.BlockSpec((tm, SEG), lambda i, j: (i, jnp.clip(j - _SEG_GB0, 0, 1)))
    if prompt:
        assert tm % MOBA_BLOCK == 0 and tm // MOBA_BLOCK <= V7X_SUBLANES
        out_shape = [f32_seg, b16_seg, b16_seg, kv_out, kv_out, b16_seg, b16_seg, b16_gate, b16_gate,
                     jax.ShapeDtypeStruct((grid[0] * V7X_SUBLANES, SEG), F32)]
        out_specs = ([seg_spec] * 3 + [kv_spec] * 2 + [seg_spec] * 2
                     + [ga_spec, gb_spec, pl.BlockSpec((V7X_SUBLANES, SEG), row_blk)])
    else:
        out_shape = [f32_seg, f32_seg, f32_seg, kv_out, kv_out, b16_seg, b16_seg, f32_seg, b16_gate, b16_gate]
        out_specs = [seg_spec] * 3 + [kv_spec] * 2 + [seg_spec] * 3 + [ga_spec, gb_spec]
    vec = lambda n: pl.BlockSpec((1, n), lambda i, j: (0, 0))
    return pl.pallas_call(
        functools.partial(_inproj_kernel, tm=tm, prompt=prompt),
        out_shape=out_shape,
        grid=grid,
        in_specs=[pl.BlockSpec((g_rows, V7X_SUBLANES, D_MODEL), lambda i, j: (i, 0, 0)),
                  _mod_spec(g_rows, 0, prompt, 2),
                  _mod_spec(g_rows, 1, prompt, 2),
                  pl.BlockSpec((1, 1, D_MODEL), lambda i, j: (0, 0, 0)),
                  pl.BlockSpec((D_MODEL, SEG), lambda i, j: (0, _w_in_col(j))),
                  vec(HEAD_DIM), vec(HEAD_DIM), vec(GM_WIDTH), vec(GM_WIDTH)],
        out_specs=out_specs,
        scratch_shapes=[pltpu.VMEM((tm, D_MODEL), BF16)],
        compiler_params=_cparams(2),
        name="inproj_prompt" if prompt else "inproj_sample",
    )(x3, mod3, mod3, g1.reshape(1, 1, D_MODEL), w_in16, qg.reshape(1, HEAD_DIM), kg.reshape(1, HEAD_DIM),
      lng.reshape(1, GM_WIDTH), lnb.reshape(1, GM_WIDTH))


def _select_topk(scores, n_valid):
    lane = lax.broadcasted_iota(jnp.int32, scores.shape, 1)
    lane_f = lane.astype(F32)
    cand = lane < n_valid
    sel = jnp.zeros(scores.shape, jnp.bool_)
    for _ in range(MOBA_TOPK):
        s_m = jnp.where(cand, scores, -jnp.inf)
        mx = jnp.max(s_m, axis=-1, keepdims=True)
        first = jnp.min(jnp.where(cand & (s_m == mx), lane_f, 1e9), axis=-1, keepdims=True)
        pick = lane_f == first
        sel = sel | pick
        cand = cand & jnp.logical_not(pick)
    return sel


def _flash_update(s, v, m_scr, l_scr, acc_scr):
    m_old = m_scr[...]
    m_new = jnp.maximum(m_old, jnp.max(s, axis=-1, keepdims=True))
    alpha = jnp.exp(m_old - m_new)
    p = jnp.exp(s - m_new)
    l_scr[...] = alpha * l_scr[...] + jnp.sum(p, axis=-1, keepdims=True)
    acc_scr[...] = alpha * acc_scr[...] + jnp.dot(p.astype(BF16), v, preferred_element_type=F32)
    m_scr[...] = m_new


def _attn_prompt_kernel(rb_ref, q_ref, k_ref, v_ref, kbar_ref, e_ref, bko_ref, bkp_ref, o_ref,
                        town, tprev, m_scr, l_scr, acc_scr):
    h = pl.program_id(0)
    j = pl.program_id(1)
    tq = MOBA_BLOCK

    @pl.when(j == 0)
    def _():
        b_far = rb_ref[h, NUM_BUCKETS - 1]
        bo = bko_ref[...]
        bp = bkp_ref[...]
        to = jnp.zeros((tq, tq), F32)
        tp = jnp.zeros((tq, tq), F32)
        for b in range(NUM_BUCKETS - 1):
            val = rb_ref[h, b] - b_far
            to = jnp.where(bo == b, val, to)
            tp = jnp.where(bp == b, val, tp)
        r = lax.broadcasted_iota(jnp.int32, (tq, tq), 0)
        c = lax.broadcasted_iota(jnp.int32, (tq, tq), 1)
        town[...] = jnp.where(c <= r, to, NEG)
        tprev[...] = tp

    q = q_ref[...]
    scores = lax.dot_general(q, kbar_ref[...], NT_DIMS, precision=lax.Precision.HIGHEST,
                             preferred_element_type=F32)
    sel = _select_topk(scores, j)
    q_s = (q * SCALE).astype(BF16)
    q_aug = jnp.where(sel, 0.0, NEG).astype(BF16)
    q_cat = jnp.concatenate([q_s, q_aug], axis=1)

    own = pl.ds(pl.multiple_of(j * tq, tq), tq)
    s = lax.dot_general(q_s, k_ref[own, :], NT_DIMS, preferred_element_type=F32) + town[...]
    m0 = jnp.max(s, axis=-1, keepdims=True)
    p = jnp.exp(s - m0)
    m_scr[...] = m0
    l_scr[...] = jnp.sum(p, axis=-1, keepdims=True)
    acc_scr[...] = jnp.dot(p.astype(BF16), v_ref[own, :], preferred_element_type=F32)

    def block_logits(n):
        rows = pl.ds(pl.multiple_of(n * tq, tq), tq)
        k_cat = jnp.concatenate([k_ref[rows, :], e_ref[rows, :]], axis=1)
        return lax.dot_general(q_cat, k_cat, NT_DIMS, preferred_element_type=F32), rows

    @pl.when(j >= 1)
    def _():
        s1, rows = block_logits(j - 1)
        _flash_update(s1 + tprev[...], v_ref[rows, :], m_scr, l_scr, acc_scr)

    def far_body(n, carry):
        s2, rows = block_logits(n)
        _flash_update(s2, v_ref[rows, :], m_scr, l_scr, acc_scr)
        return carry

    lax.fori_loop(0, jnp.maximum(j - 1, 0), far_body, 0)

    o_ref[...] = (acc_scr[...] / l_scr[...]).astype(o_ref.dtype)


def _attn_prompt(q32, k16, v16, kbar, rel_bias):
    seq = q32.shape[0]
    tq = MOBA_BLOCK
    n_blocks = seq // tq
    pos = jnp.arange(seq, dtype=jnp.int32)
    onehot = (pos[:, None] // tq == jnp.arange(HEAD_DIM, dtype=jnp.int32)[None, :]).astype(BF16)
    r = jnp.arange(tq, dtype=jnp.int32)
    bucket_own = _t5_bucket(r[:, None] - r[None, :])
    bucket_prev = _t5_bucket(tq + r[:, None] - r[None, :])
    head_blk = lambda h, j: (0, h)
    const = lambda h, j: (0, 0)
    return pl.pallas_call(
        _attn_prompt_kernel,
        out_shape=jax.ShapeDtypeStruct((seq, ATT_WIDTH), BF16),
        grid=(N_HEADS, n_blocks),
        in_specs=[pl.BlockSpec(memory_space=pltpu.SMEM),
                  pl.BlockSpec((tq, HEAD_DIM), lambda h, j: (j, h)),
                  pl.BlockSpec((seq, HEAD_DIM), head_blk),
                  pl.BlockSpec((seq, HEAD_DIM), head_blk),
                  pl.BlockSpec((kbar.shape[0], HEAD_DIM), head_blk),
                  pl.BlockSpec((seq, HEAD_DIM), const),
                  pl.BlockSpec((tq, tq), const),
                  pl.BlockSpec((tq, tq), const)],
        out_specs=pl.BlockSpec((tq, HEAD_DIM), lambda h, j: (j, h)),
        scratch_shapes=[pltpu.VMEM((tq, tq), F32), pltpu.VMEM((tq, tq), F32),
                        pltpu.VMEM((tq, 1), F32), pltpu.VMEM((tq, 1), F32), pltpu.VMEM((tq, HEAD_DIM), F32)],
        compiler_params=_cparams(2),
        name="attn_prompt",
    )(rel_bias.T, q32, k16, v16, kbar, onehot, bucket_own, bucket_prev)


def _attn_sample_kernel(pt_ref, q_ref, kn_ref, vn_ref, rbrows_ref, bkp_ref, *rest, n_pages, dec_seq):
    k_pages = rest[:n_pages]
    v_pages = rest[n_pages:2 * n_pages]
    o_ref, k16, v16, kbar_scr, bprev_scr, bown_scr = rest[2 * n_pages:]
    b = pl.program_id(0)
    past = n_pages * PAGE_SIZE
    n_past_blocks = past // MOBA_BLOCK
    rows = N_HEADS * dec_seq
    pages_per_block = MOBA_BLOCK // PAGE_SIZE

    @pl.when(b == 0)
    def _():
        kbar_scr[...] = jnp.zeros_like(kbar_scr)
        rb = rbrows_ref[...]
        bp = bkp_ref[...]
        tp = jnp.zeros((rows, MOBA_BLOCK), F32)
        for bb in range(NUM_BUCKETS - 1):
            tp = jnp.where(bp == bb, rb[:, bb:bb + 1], tp)
        bprev_scr[...] = tp
        t_row = lax.broadcasted_iota(jnp.int32, (rows, V7X_LANES), 0) & (dec_seq - 1)
        c_col = lax.broadcasted_iota(jnp.int32, (rows, V7X_LANES), 1)
        rel = t_row - c_col
        to = jnp.zeros((rows, V7X_LANES), F32)
        for bb in range(dec_seq):
            to = jnp.where(rel == bb, rb[:, bb:bb + 1], to)
        bown_scr[...] = jnp.where(rel >= 0, to, NEG)

    for n in range(n_past_blocks):
        for h in range(N_HEADS):
            sl = slice(h * HEAD_DIM, (h + 1) * HEAD_DIM)
            ksum = None
            for pp in range(pages_per_block):
                p = n * pages_per_block + pp
                page_rows = slice(p * PAGE_SIZE, (p + 1) * PAGE_SIZE)
                kh = k_pages[p][pl.ds(h, PAGE_SIZE, stride=N_HEADS), :]
                k16[page_rows, sl] = kh.astype(BF16)
                v16[page_rows, sl] = v_pages[p][pl.ds(h, PAGE_SIZE, stride=N_HEADS), :].astype(BF16)
                cs = jnp.sum(kh, axis=0, keepdims=True)
                ksum = cs if ksum is None else ksum + cs
            kbar_scr[n:n + 1, sl] = ksum * (1.0 / MOBA_BLOCK)
    pad = jnp.zeros((V7X_LANES - dec_seq, ATT_WIDTH), F32)
    k16[past:past + V7X_LANES, :] = jnp.concatenate([kn_ref[...], pad], axis=0).astype(BF16)
    v16[past:past + V7X_LANES, :] = jnp.concatenate([vn_ref[...], pad], axis=0).astype(BF16)

    q = q_ref[...]
    q_rep = jnp.concatenate([q] * N_HEADS, axis=0)
    row_head = lax.broadcasted_iota(jnp.int32, (rows, ATT_WIDTH), 0) >> int(math.log2(dec_seq))
    col_head = lax.broadcasted_iota(jnp.int32, (rows, ATT_WIDTH), 1) >> int(math.log2(HEAD_DIM))
    on_diag = row_head == col_head
    q_bd = jnp.where(on_diag, q_rep, 0.0)

    scores = lax.dot_general(q_bd, kbar_scr[...], NT_DIMS, precision=lax.Precision.HIGHEST,
                             preferred_element_type=F32)
    sel = _select_topk(scores, n_past_blocks)
    mask_add = jnp.where(sel, 0.0, NEG)

    q_s = (q_bd * SCALE).astype(BF16)
    s = lax.dot_general(q_s, k16[...], NT_DIMS, preferred_element_type=F32)
    pieces = []
    for n in range(n_past_blocks):
        piece = s[:, n * MOBA_BLOCK:(n + 1) * MOBA_BLOCK] + mask_add[:, n:n + 1]
        if n == n_past_blocks - 1:
            piece = piece + bprev_scr[...]
        pieces.append(piece)
    pieces.append(s[:, past:] + bown_scr[...])
    s = jnp.concatenate(pieces, axis=1)
    m = jnp.max(s, axis=-1, keepdims=True)
    p = jnp.exp(s - m)
    l = jnp.sum(p, axis=-1, keepdims=True)
    out = jnp.dot(p.astype(BF16), v16[...], preferred_element_type=F32) / l
    for h in range(N_HEADS):
        sl = slice(h * HEAD_DIM, (h + 1) * HEAD_DIM)
        o_ref[:, sl] = out[h * dec_seq:(h + 1) * dec_seq, sl]


def _attn_sample(q3, k3, v3, cache_k, cache_v, page_table, rel_bias):
    n_seq, dec_seq, _ = q3.shape
    n_pages = page_table.shape[1]
    past = n_pages * PAGE_SIZE
    rows = N_HEADS * dec_seq
    page_rows = PAGE_SIZE * N_HEADS
    ck = cache_k.reshape(-1, HEAD_DIM)
    cv = cache_v.reshape(-1, HEAD_DIM)
    rb_rel = rel_bias.T - rel_bias[NUM_BUCKETS - 1][:, None]
    rbrows = jnp.repeat(rb_rel, dec_seq, axis=0)
    t = jnp.arange(rows, dtype=jnp.int32) % dec_seq
    c = jnp.arange(MOBA_BLOCK, dtype=jnp.int32)
    bucket_prev = _t5_bucket(MOBA_BLOCK + t[:, None] - c[None, :])
    tok = pl.BlockSpec((None, dec_seq, ATT_WIDTH), lambda b, pt: (b, 0, 0))
    const2 = lambda shape: pl.BlockSpec(shape, lambda b, pt: (0, 0))

    def page_spec(p):
        return pl.BlockSpec((page_rows, HEAD_DIM), lambda b, pt: (pt[b, p], 0))

    grid_spec = pltpu.PrefetchScalarGridSpec(
        num_scalar_prefetch=1,
        grid=(n_seq,),
        in_specs=[tok, tok, tok, const2((rows, NUM_BUCKETS)), const2((rows, MOBA_BLOCK))]
                 + [page_spec(p) for p in range(n_pages)] * 2,
        out_specs=tok,
        scratch_shapes=[pltpu.VMEM((past + V7X_LANES, ATT_WIDTH), BF16),
                        pltpu.VMEM((past + V7X_LANES, ATT_WIDTH), BF16),
                        pltpu.VMEM((V7X_LANES, ATT_WIDTH), F32),
                        pltpu.VMEM((rows, MOBA_BLOCK), F32),
                        pltpu.VMEM((rows, V7X_LANES), F32)],
    )
    return pl.pallas_call(
        functools.partial(_attn_sample_kernel, n_pages=n_pages, dec_seq=dec_seq),
        out_shape=jax.ShapeDtypeStruct((n_seq, dec_seq, ATT_WIDTH), F32),
        grid_spec=grid_spec,
        compiler_params=_cparams(1),
        name="attn_sample",
    )(page_table, q3, k3, v3, rbrows, bucket_prev, *([ck] * n_pages), *([cv] * n_pages))


def _merge_kernel(x_ref, gt_ref, u_ref, vn_ref, ga_ref, gb_ref, att_ref, wsp_ref, bsp_ref,
                  wpa_ref, wpb_ref, wo_ref, o_ref, gm_scr, *, tm):
    for c in range(tm // GM_CHUNK):
        rows = slice(c * GM_CHUNK, (c + 1) * GM_CHUNK)
        for g in range(GM_GROUPS):
            cols = slice(g * GM_CHUNK, (g + 1) * GM_CHUNK)
            s = jnp.dot(wsp_ref[g], vn_ref[rows, cols], preferred_element_type=F32) + bsp_ref[:, cols]
            gm_scr[rows, cols] = (u_ref[rows, cols].astype(F32) * s).astype(BF16)
    a = jnp.dot(gm_scr[...], wpa_ref[...], preferred_element_type=F32)
    b = jnp.dot(att_ref[...].astype(BF16), wpb_ref[...], preferred_element_type=F32)
    merged = ga_ref[...].astype(F32) * a + gb_ref[...].astype(F32) * b
    c_out = jnp.dot(merged.astype(BF16), wo_ref[...], preferred_element_type=F32)
    o_ref[...] = x_ref[...] + gt_ref[...] * c_out.reshape(o_ref.shape)


def _merge(x3, mod3, u16, vn16, ga16, gb16, att, w_sp16, b_sp, w_pa16, w_pb16, w_o16, *, prompt, tm):
    n_rows = x3.shape[0] * V7X_SUBLANES
    g_rows = tm // V7X_SUBLANES
    row3 = pl.BlockSpec((g_rows, V7X_SUBLANES, D_MODEL), lambda i: (i, 0, 0))
    seg = pl.BlockSpec((tm, SEG), lambda i: (i, 0))
    gate = pl.BlockSpec((tm, D_MODEL), lambda i: (i, 0))
    resident = lambda shape: pl.BlockSpec(shape, lambda i: (0,) * len(shape), pipeline_mode=pl.Buffered(1))
    return pl.pallas_call(
        functools.partial(_merge_kernel, tm=tm),
        out_shape=jax.ShapeDtypeStruct(x3.shape, F32),
        grid=(n_rows // tm,),
        in_specs=[row3, _mod_spec(g_rows, 2, prompt, 1), seg, seg, gate, gate, seg,
                  resident((GM_GROUPS, GM_CHUNK, GM_CHUNK)), resident((GM_CHUNK, GM_WIDTH)),
                  resident((GM_WIDTH, D_MODEL)), resident((ATT_WIDTH, D_MODEL)), resident((D_MODEL, D_MODEL))],
        out_specs=row3,
        scratch_shapes=[pltpu.VMEM((tm, GM_WIDTH), BF16)],
        compiler_params=_cparams(1),
        name="merge_prompt" if prompt else "merge_sample",
    )(x3, mod3, u16, vn16, ga16, gb16, att, w_sp16, b_sp, w_pa16, w_pb16, w_o16)


def _ffn_kernel(x_ref, sh_ref, sc_ref, gt_ref, g2_ref, wg_ref, wu_ref, wd_ref, o_ref, h_scr, acc_scr, *, tm):
    j = pl.program_id(1)

    @pl.when(j == 0)
    def _():
        x = x_ref[...]
        ms = jnp.mean(x * x, axis=-1, keepdims=True)
        y = x * lax.rsqrt(ms + EPS) * g2_ref[...]
        h = y * (1.0 + sc_ref[...]) + sh_ref[...]
        h_scr[...] = h.reshape(tm, D_MODEL).astype(BF16)
        acc_scr[...] = jnp.zeros_like(acc_scr)

    h = h_scr[...]
    g = jnp.dot(h, wg_ref[...], preferred_element_type=F32)
    u = jnp.dot(h, wu_ref[...], preferred_element_type=F32)
    a = (g * _sigmoid(g) * u).astype(BF16)
    acc_scr[...] += jnp.dot(a, wd_ref[...], preferred_element_type=F32)

    @pl.when(j == pl.num_programs(1) - 1)
    def _():
        o_ref[...] = x_ref[...] + gt_ref[...] * acc_scr[...].reshape(o_ref.shape)


def _ffn(x3, mod3, g2, w_gu16, w_down16, *, prompt, tm, tf):
    n_rows = x3.shape[0] * V7X_SUBLANES
    g_rows = tm // V7X_SUBLANES
    n_chunks = D_FF // tf
    row3 = pl.BlockSpec((g_rows, V7X_SUBLANES, D_MODEL), lambda i, j: (i, 0, 0))
    return pl.pallas_call(
        functools.partial(_ffn_kernel, tm=tm),
        out_shape=jax.ShapeDtypeStruct(x3.shape, F32),
        grid=(n_rows // tm, n_chunks),
        in_specs=[row3, _mod_spec(g_rows, 3, prompt, 2), _mod_spec(g_rows, 4, prompt, 2),
                  _mod_spec(g_rows, 5, prompt, 2),
                  pl.BlockSpec((1, 1, D_MODEL), lambda i, j: (0, 0, 0)),
                  pl.BlockSpec((D_MODEL, tf), lambda i, j: (0, j)),
                  pl.BlockSpec((D_MODEL, tf), lambda i, j: (0, n_chunks + j)),
                  pl.BlockSpec((tf, D_MODEL), lambda i, j: (j, 0))],
        out_specs=row3,
        scratch_shapes=[pltpu.VMEM((tm, D_MODEL), BF16), pltpu.VMEM((tm, D_MODEL), F32)],
        compiler_params=_cparams(2),
        name="ffn_prompt" if prompt else "ffn_sample",
    )(x3, mod3, mod3, mod3, g2.reshape(1, 1, D_MODEL), w_gu16, w_gu16, w_down16)


def _gm_spatial_weights(gm_ws, gm_bs, dec_seq):
    tril = jnp.tril(jnp.ones((GM_CHUNK, GM_CHUNK), dtype=bool))
    w = jnp.where(tril[None], gm_ws, jnp.zeros_like(gm_ws))
    w_prompt = w.astype(BF16)
    b_prompt = jnp.repeat(gm_bs.T, GM_WIDTH // GM_GROUPS, axis=1)
    reps = GM_CHUNK // dec_seq
    eye = jnp.eye(reps, dtype=F32)
    w_small = w[:, :dec_seq, :dec_seq]
    w_sample = jnp.einsum('ab,gts->gatbs', eye, w_small).reshape(GM_GROUPS, GM_CHUNK, GM_CHUNK).astype(BF16)
    b_sample = jnp.tile(jnp.repeat(gm_bs[:, :dec_seq].T, GM_WIDTH // GM_GROUPS, axis=1), (reps, 1))
    return w_prompt, b_prompt, w_sample, b_sample


def kernel(x_prompt, x_sample, c_prompt, c_sample, cache_k, cache_v, page_table, w_ada, b_ada, norm1_g, w_in,
           q_norm_g, k_norm_g, gm_ln_g, gm_ln_b, gm_ws, gm_bs, w_pa, w_pb, w_o, norm2_g, w_gu, w_down,
           rel_bias):
    depth = w_in.shape[0]
    assert depth == 1, "single-layer step"
    n_seq, dec_seq, _ = x_sample.shape
    batch, seq, _ = x_prompt.shape
    assert batch == 1 and n_seq == _PROMPT_MOD_ROW and dec_seq == V7X_SUBLANES

    pad_rows = V7X_SUBLANES - 1
    c_all = jnp.concatenate([c_sample, c_prompt, jnp.zeros((pad_rows, D_MODEL), F32)], axis=0)
    mod = _modulation(c_all, w_ada[0], b_ada[0])
    mod3 = mod.reshape(mod.shape[0], 1, 6 * D_MODEL)

    w_in16 = w_in[0].astype(BF16)
    w_pa16 = w_pa[0].astype(BF16)
    w_pb16 = w_pb[0].astype(BF16)
    w_o16 = w_o[0].astype(BF16)
    w_gu16 = w_gu[0].astype(BF16)
    w_down16 = w_down[0].astype(BF16)
    wsp_p, bsp_p, wsp_s, bsp_s = _gm_spatial_weights(gm_ws[0], gm_bs[0], dec_seq)

    xp3 = x_prompt.reshape(seq // V7X_SUBLANES, V7X_SUBLANES, D_MODEL)
    xs3 = x_sample

    proj = functools.partial(_inproj, g1=norm1_g[0], w_in16=w_in16, qg=q_norm_g[0], kg=k_norm_g[0],
                             lng=gm_ln_g[0], lnb=gm_ln_b[0])
    tm_prompt = 512
    q32, k16, v16, k_out, v_out, u16, vn16, ga16, gb16, kbar_tiles = proj(xp3, mod3, prompt=True, tm=tm_prompt)
    per_tile = tm_prompt // MOBA_BLOCK
    kbar = kbar_tiles.reshape(-1, V7X_SUBLANES, ATT_WIDTH)[:, :per_tile].reshape(-1, ATT_WIDTH)
    kbar = jnp.pad(kbar, ((0, V7X_LANES - kbar.shape[0]), (0, 0)))
    sq32, sk32, sv32, sk_out, sv_out, su16, svn16, svn32, sga16, sgb16 = proj(xs3, mod3, prompt=False, tm=256)

    att_p = _attn_prompt(q32, k16, v16, kbar, rel_bias)
    tok3 = lambda a: a.reshape(n_seq, dec_seq, ATT_WIDTH)
    att_s = _attn_sample(tok3(sq32), tok3(sk32), tok3(sv32), cache_k[0], cache_v[0], page_table, rel_bias)
    att_s = att_s.reshape(n_seq * dec_seq, ATT_WIDTH)

    mrg = functools.partial(_merge, w_pa16=w_pa16, w_pb16=w_pb16, w_o16=w_o16, tm=256)
    x1p = mrg(xp3, mod3, u16, vn16, ga16, gb16, att_p, wsp_p, bsp_p, prompt=True)
    x1s = mrg(xs3, mod3, su16, svn16, sga16, sgb16, att_s, wsp_s, bsp_s, prompt=False)

    ffn = functools.partial(_ffn, g2=norm2_g[0], w_gu16=w_gu16, w_down16=w_down16, tm=512, tf=512)
    y_p = ffn(x1p, mod3, prompt=True).reshape(batch, seq, D_MODEL)
    y_s = ffn(x1s, mod3, prompt=False)

    kv_p = lambda a: a.reshape(1, batch, seq, N_HEADS, HEAD_DIM)
    kv_s = lambda a: a.reshape(1, n_seq, dec_seq, N_HEADS, HEAD_DIM)
    return (y_p, y_s, kv_p(k_out), kv_p(v_out), kv_s(sk_out), kv_s(sv_out),
            svn32.reshape(1, n_seq, dec_seq, GM_WIDTH))
```

```python
import functools
import math

import jax
import jax.numpy as jnp
from jax import lax
from jax.experimental import pallas as pl
from jax.experimental.pallas import tpu as pltpu

F32 = jnp.float32
BF16 = jnp.bfloat16

D_MODEL = 2048
N_HEADS = 8
HEAD_DIM = 128
ATT_WIDTH = N_HEADS * HEAD_DIM
GM_WIDTH = 1024
GM_GROUPS = 8
GM_CHUNK = 128
MOBA_BLOCK = 256
MOBA_TOPK = 3
NUM_BUCKETS = 32
MAX_DISTANCE = 128
PAGE_SIZE = 128
D_FF = 5632
EPS = 1e-6
SCALE = HEAD_DIM ** -0.5
LOG2E = math.log2(math.e)

V7X_LANES = 128
V7X_SUBLANES = 8
V7X_VMEM_BYTES = 64 * 1024 * 1024
VMEM_LIMIT = V7X_VMEM_BYTES - 8 * 1024 * 1024

NEG = -1e30
SEG = 1024

NT_DIMS = (((1,), (1,)), ((), ()))


def _cparams(n_axes, vmem=VMEM_LIMIT):
    return pltpu.CompilerParams(dimension_semantics=("arbitrary",) * n_axes, vmem_limit_bytes=vmem)


def _sigmoid(x):
    return 1.0 / (1.0 + jnp.exp(-x))


def _gelu_tanh(x):
    c = math.sqrt(2.0 / math.pi)
    return x * (0.5 * (1.0 + jnp.tanh(c * (x + 0.044715 * (x * x * x)))))


def _t5_bucket(rel):
    n = jnp.maximum(rel, 0)
    max_exact = NUM_BUCKETS // 2
    nf = jnp.maximum(n, 1).astype(F32)
    large = max_exact + (jnp.log(nf / max_exact) / math.log(MAX_DISTANCE / max_exact)
                         * (NUM_BUCKETS - max_exact)).astype(jnp.int32)
    large = jnp.minimum(large, NUM_BUCKETS - 1)
    return jnp.where(n < max_exact, n, large)


def _mod_kernel(c_ref, w_ref, b_ref, o_ref):
    c = c_ref[...]
    s = (c * _sigmoid(c)).astype(BF16)
    o_ref[...] = jnp.dot(s, w_ref[...].astype(BF16), preferred_element_type=F32) + b_ref[...]


def _modulation(c_all, w_ada, b_ada):
    rows = c_all.shape[0]
    n_out = w_ada.shape[1]
    tn = 1024
    return pl.pallas_call(
        _mod_kernel,
        out_shape=jax.ShapeDtypeStruct((rows, n_out), F32),
        grid=(n_out // tn,),
        in_specs=[pl.BlockSpec((rows, D_MODEL), lambda j: (0, 0)),
                  pl.BlockSpec((D_MODEL, tn), lambda j: (0, j)),
                  pl.BlockSpec((1, tn), lambda j: (0, j))],
        out_specs=pl.BlockSpec((rows, tn), lambda j: (0, j)),
        compiler_params=_cparams(1),
        name="mod",
    )(c_all, w_ada, b_ada.reshape(1, n_out))


_SEG_K, _SEG_Q, _SEG_V, _SEG_U, _SEG_VG, _SEG_GA0, _SEG_GA1, _SEG_GB0, _SEG_GB1 = range(9)


def _inproj_kernel(x_ref, sh_ref, sc_ref, g1_ref, w_ref, qg_ref, kg_ref, lng_ref, lnb_ref,
                   *rest, tm, prompt):
    if prompt:
        qT32, k16, vT16, k_out, v_out, u16, vn16, ga16, gb16, kbar, h_scr = rest
        q32 = k32 = v32 = None
    else:
        q32, k32, v32, k_out, v_out, u16, vn16, vn32, ga16, gb16, h_scr = rest
        qT32 = k16 = vT16 = None
    j = pl.program_id(1)

    def normed_input():
        x = x_ref[...]
        ms = jnp.mean(x * x, axis=-1, keepdims=True)
        y = x * lax.rsqrt(ms + EPS) * g1_ref[...]
        h = y * (1.0 + sc_ref[...]) + sh_ref[...]
        return h.reshape(tm, D_MODEL).astype(BF16)

    def matmul(h=None):
        lhs = h_scr[...] if h is None else h
        return jnp.dot(lhs, w_ref[...], preferred_element_type=F32)

    def store_heads(z, y_of_head, dst32, dst16, dst_out, with_means, dst_t=None):
        for h in range(N_HEADS):
            sl = slice(h * HEAD_DIM, (h + 1) * HEAD_DIM)
            yh = y_of_head(z[:, sl])
            if dst32 is not None:
                dst32[:, sl] = yh
            if dst16 is not None:
                dst16[:, sl] = yh.astype(BF16)
            if dst_t is not None:
                dst_t[sl, :] = yh.T.astype(dst_t.dtype)
            if dst_out is not None:
                dst_out[pl.ds(h, tm, stride=N_HEADS), :] = yh
            if with_means:
                for r in range(tm // MOBA_BLOCK):
                    blk = yh[r * MOBA_BLOCK:(r + 1) * MOBA_BLOCK]
                    kbar[r:r + 1, sl] = jnp.mean(blk, axis=0, keepdims=True)

    def head_norm(g_ref):
        def f(zh):
            ms = jnp.mean(zh * zh, axis=-1, keepdims=True)
            return zh * lax.rsqrt(ms + EPS) * g_ref[...]
        return f

    @pl.when(j == _SEG_K)
    def _():
        h = normed_input()
        h_scr[...] = h
        if prompt:
            kbar[...] = jnp.zeros_like(kbar)
        store_heads(matmul(h), head_norm(kg_ref), k32, k16, k_out, prompt)

    @pl.when(j == _SEG_Q)
    def _():
        store_heads(matmul(), head_norm(qg_ref), q32, None, None, False, dst_t=qT32)

    @pl.when(j == _SEG_V)
    def _():
        store_heads(matmul(), lambda zh: zh, v32, None, v_out, False, dst_t=vT16)

    @pl.when(j == _SEG_U)
    def _():
        u16[...] = _gelu_tanh(matmul()).astype(BF16)

    @pl.when(j == _SEG_VG)
    def _():
        a = _gelu_tanh(matmul())
        mu = jnp.mean(a, axis=-1, keepdims=True)
        ac = a - mu
        y = ac * lax.rsqrt(jnp.mean(ac * ac, axis=-1, keepdims=True) + EPS)
        y = y * lng_ref[...] + lnb_ref[...]
        vn16[...] = y.astype(BF16)
        if not prompt:
            vn32[...] = y

    @pl.when((j == _SEG_GA0) | (j == _SEG_GA1))
    def _():
        ga16[...] = _sigmoid(matmul()).astype(BF16)

    @pl.when((j == _SEG_GB0) | (j == _SEG_GB1))
    def _():
        gb16[...] = _sigmoid(matmul()).astype(BF16)


def _mod_spec(g_rows, chunk, prompt, n_grid):
    if n_grid == 2:
        if prompt:
            return pl.BlockSpec((1, 1, D_MODEL), lambda i, j: (_PROMPT_MOD_ROW, 0, chunk))
        return pl.BlockSpec((g_rows, 1, D_MODEL), lambda i, j: (i, 0, chunk))
    if prompt:
        return pl.BlockSpec((1, 1, D_MODEL), lambda i: (_PROMPT_MOD_ROW, 0, chunk))
    return pl.BlockSpec((g_rows, 1, D_MODEL), lambda i: (i, 0, chunk))


_PROMPT_MOD_ROW = 128


def _w_in_col(j):
    return jnp.where(j < 2, 1 - j, j)


def _inproj(x3, mod3, g1, w_in16, qg, kg, lng, lnb, *, prompt, tm):
    n_rows = x3.shape[0] * V7X_SUBLANES
    g_rows = tm // V7X_SUBLANES
    grid = (n_rows // tm, 9)
    row_blk = lambda i, j: (i, 0)
    f32_seg = jax.ShapeDtypeStruct((n_rows, SEG), F32)
    b16_seg = jax.ShapeDtypeStruct((n_rows, SEG), BF16)
    b16_gate = jax.ShapeDtypeStruct((n_rows, D_MODEL), BF16)
    seg_spec = pl.BlockSpec((tm, SEG), row_blk)
    kv_out = jax.ShapeDtypeStruct((n_rows * N_HEADS, HEAD_DIM), F32)
    kv_spec = pl.BlockSpec((tm * N_HEADS, HEAD_DIM), row_blk)
    ga_spec = pl.BlockSpec((tm, SEG), lambda i, j: (i, jnp.clip(j - _SEG_GA0, 0, 1)))
    gb_spec = pl.BlockSpec((tm, SEG), lambda i, j: (i, jnp.clip(j - _SEG_GB0, 0, 1)))
    if prompt:
        assert tm % MOBA_BLOCK == 0 and tm // MOBA_BLOCK <= V7X_SUBLANES
        t_spec = pl.BlockSpec((SEG, tm), lambda i, j: (0, i))
        out_shape = [jax.ShapeDtypeStruct((SEG, n_rows), F32), b16_seg, jax.ShapeDtypeStruct((SEG, n_rows), BF16),
                     kv_out, kv_out, b16_seg, b16_seg, b16_gate, b16_gate,
                     jax.ShapeDtypeStruct((grid[0] * V7X_SUBLANES, SEG), F32)]
        out_specs = ([t_spec, seg_spec, t_spec] + [kv_spec] * 2 + [seg_spec] * 2
                     + [ga_spec, gb_spec, pl.BlockSpec((V7X_SUBLANES, SEG), row_blk)])
    else:
        out_shape = [f32_seg, f32_seg, f32_seg, kv_out, kv_out, b16_seg, b16_seg, f32_seg, b16_gate, b16_gate]
        out_specs = [seg_spec] * 3 + [kv_spec] * 2 + [seg_spec] * 3 + [ga_spec, gb_spec]
    vec = lambda n: pl.BlockSpec((1, n), lambda i, j: (0, 0))
    return pl.pallas_call(
        functools.partial(_inproj_kernel, tm=tm, prompt=prompt),
        out_shape=out_shape,
        grid=grid,
        in_specs=[pl.BlockSpec((g_rows, V7X_SUBLANES, D_MODEL), lambda i, j: (i, 0, 0)),
                  _mod_spec(g_rows, 0, prompt, 2),
                  _mod_spec(g_rows, 1, prompt, 2),
                  pl.BlockSpec((1, 1, D_MODEL), lambda i, j: (0, 0, 0)),
                  pl.BlockSpec((D_MODEL, SEG), lambda i, j: (0, _w_in_col(j))),
                  vec(HEAD_DIM), vec(HEAD_DIM), vec(GM_WIDTH), vec(GM_WIDTH)],
        out_specs=out_specs,
        scratch_shapes=[pltpu.VMEM((tm, D_MODEL), BF16)],
        compiler_params=_cparams(2),
        name="inproj_prompt" if prompt else "inproj_sample",
    )(x3, mod3, mod3, g1.reshape(1, 1, D_MODEL), w_in16, qg.reshape(1, HEAD_DIM), kg.reshape(1, HEAD_DIM),
      lng.reshape(1, GM_WIDTH), lnb.reshape(1, GM_WIDTH))


def _select_topk(scores, n_valid, axis):
    idx = lax.broadcasted_iota(jnp.int32, scores.shape, axis)
    idx_f = idx.astype(F32)
    cand = idx < n_valid
    sel = jnp.zeros(scores.shape, jnp.bool_)
    for _ in range(MOBA_TOPK):
        s_m = jnp.where(cand, scores, -jnp.inf)
        mx = jnp.max(s_m, axis=axis, keepdims=True)
        first = jnp.min(jnp.where(cand & (s_m == mx), idx_f, 1e9), axis=axis, keepdims=True)
        pick = idx_f == first
        sel = sel | pick
        cand = cand & jnp.logical_not(pick)
    return sel


_FAR_GROUP = 4
_SUM_ROWS = 16
_HEADS_PER_STEP = 4


def _attn_prompt_kernel(rb_ref, qT_ref, k_ref, vT_ref, kbar_ref, e_ref, bko_ref, bkp_ref, o_ref,
                        town, tprev, m_scr, r_scr):
    hp = pl.program_id(0)
    j = pl.program_id(1)
    tq = MOBA_BLOCK
    n_sel_rows = kbar_ref.shape[0]
    heads = range(_HEADS_PER_STEP)
    cols = [slice(i * HEAD_DIM, (i + 1) * HEAD_DIM) for i in heads]

    @pl.when(j == 0)
    def _():
        bo = bko_ref[...]
        bp = bkp_ref[...]
        key = lax.broadcasted_iota(jnp.int32, (tq, tq), 0)
        qry = lax.broadcasted_iota(jnp.int32, (tq, tq), 1)
        for i in heads:
            h = hp * _HEADS_PER_STEP + i
            b_far = rb_ref[h, NUM_BUCKETS - 1]
            to = jnp.zeros((tq, tq), F32)
            tp = jnp.zeros((tq, tq), F32)
            for b in range(NUM_BUCKETS - 1):
                val = (rb_ref[h, b] - b_far) * LOG2E
                to = jnp.where(bo == b, val, to)
                tp = jnp.where(bp == b, val, tp)
            town[i] = jnp.where(key <= qry, to, NEG)
            tprev[i] = tp

    q_sT, q_catT, prev_mask = [], [], []
    for i in heads:
        qT = qT_ref[cols[i], :]
        scoresT = jnp.dot(kbar_ref[:, cols[i]], qT, precision=lax.Precision.HIGHEST,
                          preferred_element_type=F32)
        selT = _select_topk(scoresT, j, axis=0)
        blk = lax.broadcasted_iota(jnp.int32, selT.shape, 0)
        qs = (qT * (SCALE * LOG2E)).astype(BF16)
        aug = jnp.where(selT & (blk < j - 1), 0.0, NEG)
        aug = jnp.concatenate([aug, jnp.full((HEAD_DIM - n_sel_rows, tq), NEG, F32)], axis=0)
        q_sT.append(qs)
        q_catT.append(jnp.concatenate([qs, aug.astype(BF16)], axis=0))
        prev_sel = jnp.max(jnp.where(selT & (blk == j - 1), 1.0, 0.0), axis=0, keepdims=True) > 0.0
        prev_mask.append(jnp.where(prev_sel, 0.0, NEG))

    ones_rows = jnp.ones((_SUM_ROWS, tq), BF16)

    def partial_softmax(logits, where):
        ms = [jnp.max(s, axis=0, keepdims=True) for s in logits]
        ps = [jnp.exp2(s - m).astype(BF16) for s, m in zip(logits, ms)]
        rs = []
        for p, (i, start) in zip(ps, where):
            vT_aug = jnp.concatenate([vT_ref[cols[i], pl.ds(start, tq)], ones_rows], axis=0)
            rs.append(jnp.dot(vT_aug, p, preferred_element_type=F32))
        return list(zip(ms, rs))

    def combine(m_run, r_run, parts):
        m_new = m_run
        for m, _ in parts:
            m_new = jnp.maximum(m_new, m)
        r_new = r_run * jnp.exp2(m_run - m_new)
        for m, r in parts:
            r_new = r_new + r * jnp.exp2(m - m_new)
        return m_new, r_new

    own_start = pl.multiple_of(j * tq, tq)
    prev_start = pl.multiple_of(jnp.maximum(j - 1, 0) * tq, tq)
    logits, where = [], []
    for i in heads:
        logits.append(jnp.dot(k_ref[pl.ds(own_start, tq), cols[i]], q_sT[i], preferred_element_type=F32)
                      + town[i])
        logits.append(jnp.dot(k_ref[pl.ds(prev_start, tq), cols[i]], q_sT[i], preferred_element_type=F32)
                      + (tprev[i] + prev_mask[i]))
        where += [(i, own_start), (i, prev_start)]
    parts = partial_softmax(logits, where)
    for i in heads:
        (m_own, r_own), part_prev = parts[2 * i], parts[2 * i + 1]
        m_scr[i], r_scr[i] = combine(m_own, r_own, [part_prev])

    def far_body(g, carry):
        starts = [pl.multiple_of((g * _FAR_GROUP + b) * tq, tq) for b in range(_FAR_GROUP)]
        logits, where = [], []
        for i in heads:
            for start in starts:
                k_cat = jnp.concatenate([k_ref[pl.ds(start, tq), cols[i]], e_ref[pl.ds(start, tq), :]], axis=1)
                logits.append(jnp.dot(k_cat, q_catT[i], preferred_element_type=F32))
                where.append((i, start))
        parts = partial_softmax(logits, where)
        for i in heads:
            m_scr[i], r_scr[i] = combine(m_scr[i], r_scr[i], parts[i * _FAR_GROUP:(i + 1) * _FAR_GROUP])
        return carry

    lax.fori_loop(0, (j + _FAR_GROUP - 2) // _FAR_GROUP, far_body, 0)

    for i in heads:
        r = r_scr[i]
        o_ref[:, cols[i]] = (r[:HEAD_DIM] / r[HEAD_DIM:HEAD_DIM + 1]).T.astype(o_ref.dtype)


def _attn_prompt(qT32, k16, vT16, kbar, rel_bias):
    seq = k16.shape[0]
    tq = MOBA_BLOCK
    n_blocks = seq // tq
    assert n_blocks % _FAR_GROUP == 0 and kbar.shape[0] == n_blocks
    pos = jnp.arange(seq, dtype=jnp.int32)
    onehot = (pos[:, None] // tq == jnp.arange(HEAD_DIM, dtype=jnp.int32)[None, :]).astype(BF16)
    r = jnp.arange(tq, dtype=jnp.int32)
    bucket_own = _t5_bucket(r[None, :] - r[:, None])
    bucket_prev = _t5_bucket(tq + r[None, :] - r[:, None])
    const = lambda h, j: (0, 0)
    hps = _HEADS_PER_STEP
    width = hps * HEAD_DIM
    return pl.pallas_call(
        _attn_prompt_kernel,
        out_shape=jax.ShapeDtypeStruct((seq, ATT_WIDTH), BF16),
        grid=(N_HEADS // hps, n_blocks),
        in_specs=[pl.BlockSpec(memory_space=pltpu.SMEM),
                  pl.BlockSpec((width, tq), lambda h, j: (h, j)),
                  pl.BlockSpec((seq, width), lambda h, j: (0, h)),
                  pl.BlockSpec((width, seq), lambda h, j: (h, 0)),
                  pl.BlockSpec((n_blocks, width), lambda h, j: (0, h)),
                  pl.BlockSpec((seq, HEAD_DIM), const),
                  pl.BlockSpec((tq, tq), const),
                  pl.BlockSpec((tq, tq), const)],
        out_specs=pl.BlockSpec((tq, width), lambda h, j: (j, h)),
        scratch_shapes=[pltpu.VMEM((hps, tq, tq), F32), pltpu.VMEM((hps, tq, tq), F32),
                        pltpu.VMEM((hps, 1, tq), F32), pltpu.VMEM((hps, HEAD_DIM + _SUM_ROWS, tq), F32)],
        compiler_params=_cparams(2),
        name="attn_prompt",
    )(rel_bias.T, qT32, k16, vT16, kbar, onehot, bucket_own, bucket_prev)


def _attn_sample_kernel(pt_ref, q_ref, kn_ref, vn_ref, rbrows_ref, bkp_ref, *rest, n_pages, dec_seq):
    k_pages = rest[:n_pages]
    v_pages = rest[n_pages:2 * n_pages]
    o_ref, k16, v16, kbar_scr, bprev_scr, bown_scr = rest[2 * n_pages:]
    b = pl.program_id(0)
    past = n_pages * PAGE_SIZE
    n_past_blocks = past // MOBA_BLOCK
    rows = N_HEADS * dec_seq
    pages_per_block = MOBA_BLOCK // PAGE_SIZE

    @pl.when(b == 0)
    def _():
        kbar_scr[...] = jnp.zeros_like(kbar_scr)
        rb = rbrows_ref[...]
        bp = bkp_ref[...]
        tp = jnp.zeros((rows, MOBA_BLOCK), F32)
        for bb in range(NUM_BUCKETS - 1):
            tp = jnp.where(bp == bb, rb[:, bb:bb + 1], tp)
        bprev_scr[...] = tp
        t_row = lax.broadcasted_iota(jnp.int32, (rows, V7X_LANES), 0) & (dec_seq - 1)
        c_col = lax.broadcasted_iota(jnp.int32, (rows, V7X_LANES), 1)
        rel = t_row - c_col
        to = jnp.zeros((rows, V7X_LANES), F32)
        for bb in range(dec_seq):
            to = jnp.where(rel == bb, rb[:, bb:bb + 1], to)
        bown_scr[...] = jnp.where(rel >= 0, to, NEG)

    for n in range(n_past_blocks):
        for h in range(N_HEADS):
            sl = slice(h * HEAD_DIM, (h + 1) * HEAD_DIM)
            ksum = None
            for pp in range(pages_per_block):
                p = n * pages_per_block + pp
                page_rows = slice(p * PAGE_SIZE, (p + 1) * PAGE_SIZE)
                kh = k_pages[p][pl.ds(h, PAGE_SIZE, stride=N_HEADS), :]
                k16[page_rows, sl] = kh.astype(BF16)
                v16[page_rows, sl] = v_pages[p][pl.ds(h, PAGE_SIZE, stride=N_HEADS), :].astype(BF16)
                cs = jnp.sum(kh, axis=0, keepdims=True)
                ksum = cs if ksum is None else ksum + cs
            kbar_scr[n:n + 1, sl] = ksum * (1.0 / MOBA_BLOCK)
    pad = jnp.zeros((V7X_LANES - dec_seq, ATT_WIDTH), F32)
    k16[past:past + V7X_LANES, :] = jnp.concatenate([kn_ref[...], pad], axis=0).astype(BF16)
    v16[past:past + V7X_LANES, :] = jnp.concatenate([vn_ref[...], pad], axis=0).astype(BF16)

    q = q_ref[...]
    q_rep = jnp.concatenate([q] * N_HEADS, axis=0)
    row_head = lax.broadcasted_iota(jnp.int32, (rows, ATT_WIDTH), 0) >> int(math.log2(dec_seq))
    col_head = lax.broadcasted_iota(jnp.int32, (rows, ATT_WIDTH), 1) >> int(math.log2(HEAD_DIM))
    on_diag = row_head == col_head
    q_bd = jnp.where(on_diag, q_rep, 0.0)

    scores = lax.dot_general(q_bd, kbar_scr[...], NT_DIMS, precision=lax.Precision.HIGHEST,
                             preferred_element_type=F32)
    sel = _select_topk(scores, n_past_blocks, axis=1)
    mask_add = jnp.where(sel, 0.0, NEG)

    q_s = (q_bd * SCALE).astype(BF16)
    s = lax.dot_general(q_s, k16[...], NT_DIMS, preferred_element_type=F32)
    pieces = []
    for n in range(n_past_blocks):
        piece = s[:, n * MOBA_BLOCK:(n + 1) * MOBA_BLOCK] + mask_add[:, n:n + 1]
        if n == n_past_blocks - 1:
            piece = piece + bprev_scr[...]
        pieces.append(piece)
    pieces.append(s[:, past:] + bown_scr[...])
    s = jnp.concatenate(pieces, axis=1)
    m = jnp.max(s, axis=-1, keepdims=True)
    p = jnp.exp(s - m)
    l = jnp.sum(p, axis=-1, keepdims=True)
    out = jnp.dot(p.astype(BF16), v16[...], preferred_element_type=F32) / l
    for h in range(N_HEADS):
        sl = slice(h * HEAD_DIM, (h + 1) * HEAD_DIM)
        o_ref[:, sl] = out[h * dec_seq:(h + 1) * dec_seq, sl]


def _attn_sample(q3, k3, v3, cache_k, cache_v, page_table, rel_bias):
    n_seq, dec_seq, _ = q3.shape
    n_pages = page_table.shape[1]
    past = n_pages * PAGE_SIZE
    rows = N_HEADS * dec_seq
    page_rows = PAGE_SIZE * N_HEADS
    ck = cache_k.reshape(-1, HEAD_DIM)
    cv = cache_v.reshape(-1, HEAD_DIM)
    rb_rel = rel_bias.T - rel_bias[NUM_BUCKETS - 1][:, None]
    rbrows = jnp.repeat(rb_rel, dec_seq, axis=0)
    t = jnp.arange(rows, dtype=jnp.int32) % dec_seq
    c = jnp.arange(MOBA_BLOCK, dtype=jnp.int32)
    bucket_prev = _t5_bucket(MOBA_BLOCK + t[:, None] - c[None, :])
    tok = pl.BlockSpec((None, dec_seq, ATT_WIDTH), lambda b, pt: (b, 0, 0))
    const2 = lambda shape: pl.BlockSpec(shape, lambda b, pt: (0, 0))

    def page_spec(p):
        return pl.BlockSpec((page_rows, HEAD_DIM), lambda b, pt: (pt[b, p], 0))

    grid_spec = pltpu.PrefetchScalarGridSpec(
        num_scalar_prefetch=1,
        grid=(n_seq,),
        in_specs=[tok, tok, tok, const2((rows, NUM_BUCKETS)), const2((rows, MOBA_BLOCK))]
                 + [page_spec(p) for p in range(n_pages)] * 2,
        out_specs=tok,
        scratch_shapes=[pltpu.VMEM((past + V7X_LANES, ATT_WIDTH), BF16),
                        pltpu.VMEM((past + V7X_LANES, ATT_WIDTH), BF16),
                        pltpu.VMEM((V7X_LANES, ATT_WIDTH), F32),
                        pltpu.VMEM((rows, MOBA_BLOCK), F32),
                        pltpu.VMEM((rows, V7X_LANES), F32)],
    )
    return pl.pallas_call(
        functools.partial(_attn_sample_kernel, n_pages=n_pages, dec_seq=dec_seq),
        out_shape=jax.ShapeDtypeStruct((n_seq, dec_seq, ATT_WIDTH), F32),
        grid_spec=grid_spec,
        compiler_params=_cparams(1),
        name="attn_sample",
    )(page_table, q3, k3, v3, rbrows, bucket_prev, *([ck] * n_pages), *([cv] * n_pages))


def _merge_kernel(x_ref, gt_ref, u_ref, vn_ref, ga_ref, gb_ref, att_ref, wsp_ref, bsp_ref,
                  wpa_ref, wpb_ref, wo_ref, o_ref, gm_scr, *, tm):
    for c in range(tm // GM_CHUNK):
        rows = slice(c * GM_CHUNK, (c + 1) * GM_CHUNK)
        for g in range(GM_GROUPS):
            cols = slice(g * GM_CHUNK, (g + 1) * GM_CHUNK)
            s = jnp.dot(wsp_ref[g], vn_ref[rows, cols], preferred_element_type=F32) + bsp_ref[:, cols]
            gm_scr[rows, cols] = (u_ref[rows, cols].astype(F32) * s).astype(BF16)
    a = jnp.dot(gm_scr[...], wpa_ref[...], preferred_element_type=F32)
    b = jnp.dot(att_ref[...].astype(BF16), wpb_ref[...], preferred_element_type=F32)
    merged = ga_ref[...].astype(F32) * a + gb_ref[...].astype(F32) * b
    c_out = jnp.dot(merged.astype(BF16), wo_ref[...], preferred_element_type=F32)
    o_ref[...] = x_ref[...] + gt_ref[...] * c_out.reshape(o_ref.shape)


def _merge(x3, mod3, u16, vn16, ga16, gb16, att, w_sp16, b_sp, w_pa16, w_pb16, w_o16, *, prompt, tm):
    n_rows = x3.shape[0] * V7X_SUBLANES
    g_rows = tm // V7X_SUBLANES
    row3 = pl.BlockSpec((g_rows, V7X_SUBLANES, D_MODEL), lambda i: (i, 0, 0))
    seg = pl.BlockSpec((tm, SEG), lambda i: (i, 0))
    gate = pl.BlockSpec((tm, D_MODEL), lambda i: (i, 0))
    resident = lambda shape: pl.BlockSpec(shape, lambda i: (0,) * len(shape), pipeline_mode=pl.Buffered(1))
    return pl.pallas_call(
        functools.partial(_merge_kernel, tm=tm),
        out_shape=jax.ShapeDtypeStruct(x3.shape, F32),
        grid=(n_rows // tm,),
        in_specs=[row3, _mod_spec(g_rows, 2, prompt, 1), seg, seg, gate, gate, seg,
                  resident((GM_GROUPS, GM_CHUNK, GM_CHUNK)), resident((GM_CHUNK, GM_WIDTH)),
                  resident((GM_WIDTH, D_MODEL)), resident((ATT_WIDTH, D_MODEL)), resident((D_MODEL, D_MODEL))],
        out_specs=row3,
        scratch_shapes=[pltpu.VMEM((tm, GM_WIDTH), BF16)],
        compiler_params=_cparams(1),
        name="merge_prompt" if prompt else "merge_sample",
    )(x3, mod3, u16, vn16, ga16, gb16, att, w_sp16, b_sp, w_pa16, w_pb16, w_o16)


def _ffn_kernel(x_ref, sh_ref, sc_ref, gt_ref, g2_ref, wg_ref, wu_ref, wd_ref, o_ref, h_scr, acc_scr, *, tm):
    j = pl.program_id(1)
    last = pl.num_programs(1) - 1

    def chunk(h):
        g = jnp.dot(h, wg_ref[...], preferred_element_type=F32)
        u = jnp.dot(h, wu_ref[...], preferred_element_type=F32)
        a = (g * _sigmoid(g) * u).astype(BF16)
        return jnp.dot(a, wd_ref[...], preferred_element_type=F32)

    @pl.when(j == 0)
    def _():
        x = x_ref[...]
        ms = jnp.mean(x * x, axis=-1, keepdims=True)
        y = x * lax.rsqrt(ms + EPS) * g2_ref[...]
        h = (y * (1.0 + sc_ref[...]) + sh_ref[...]).reshape(tm, D_MODEL).astype(BF16)
        h_scr[...] = h
        acc_scr[...] = chunk(h)

    @pl.when((j > 0) & (j < last))
    def _():
        acc_scr[...] += chunk(h_scr[...])

    @pl.when(j == last)
    def _():
        total = acc_scr[...] + chunk(h_scr[...])
        o_ref[...] = x_ref[...] + gt_ref[...] * total.reshape(o_ref.shape)


def _ffn(x3, mod3, g2, w_gu16, w_down16, *, prompt, tm, tf):
    n_rows = x3.shape[0] * V7X_SUBLANES
    g_rows = tm // V7X_SUBLANES
    n_chunks = D_FF // tf
    row3 = pl.BlockSpec((g_rows, V7X_SUBLANES, D_MODEL), lambda i, j: (i, 0, 0))
    return pl.pallas_call(
        functools.partial(_ffn_kernel, tm=tm),
        out_shape=jax.ShapeDtypeStruct(x3.shape, F32),
        grid=(n_rows // tm, n_chunks),
        in_specs=[row3, _mod_spec(g_rows, 3, prompt, 2), _mod_spec(g_rows, 4, prompt, 2),
                  _mod_spec(g_rows, 5, prompt, 2),
                  pl.BlockSpec((1, 1, D_MODEL), lambda i, j: (0, 0, 0)),
                  pl.BlockSpec((D_MODEL, tf), lambda i, j: (0, j)),
                  pl.BlockSpec((D_MODEL, tf), lambda i, j: (0, n_chunks + j)),
                  pl.BlockSpec((tf, D_MODEL), lambda i, j: (j, 0))],
        out_specs=row3,
        scratch_shapes=[pltpu.VMEM((tm, D_MODEL), BF16), pltpu.VMEM((tm, D_MODEL), F32)],
        compiler_params=_cparams(2),
        name="ffn_prompt" if prompt else "ffn_sample",
    )(x3, mod3, mod3, mod3, g2.reshape(1, 1, D_MODEL), w_gu16, w_gu16, w_down16)


def _gm_spatial_weights(gm_ws, gm_bs, dec_seq):
    tril = jnp.tril(jnp.ones((GM_CHUNK, GM_CHUNK), dtype=bool))
    w = jnp.where(tril[None], gm_ws, jnp.zeros_like(gm_ws))
    w_prompt = w.astype(BF16)
    b_prompt = jnp.repeat(gm_bs.T, GM_WIDTH // GM_GROUPS, axis=1)
    reps = GM_CHUNK // dec_seq
    eye = jnp.eye(reps, dtype=F32)
    w_small = w[:, :dec_seq, :dec_seq]
    w_sample = jnp.einsum('ab,gts->gatbs', eye, w_small).reshape(GM_GROUPS, GM_CHUNK, GM_CHUNK).astype(BF16)
    b_sample = jnp.tile(jnp.repeat(gm_bs[:, :dec_seq].T, GM_WIDTH // GM_GROUPS, axis=1), (reps, 1))
    return w_prompt, b_prompt, w_sample, b_sample


def kernel(x_prompt, x_sample, c_prompt, c_sample, cache_k, cache_v, page_table, w_ada, b_ada, norm1_g, w_in,
           q_norm_g, k_norm_g, gm_ln_g, gm_ln_b, gm_ws, gm_bs, w_pa, w_pb, w_o, norm2_g, w_gu, w_down,
           rel_bias):
    depth = w_in.shape[0]
    assert depth == 1, "single-layer step"
    n_seq, dec_seq, _ = x_sample.shape
    batch, seq, _ = x_prompt.shape
    assert batch == 1 and n_seq == _PROMPT_MOD_ROW and dec_seq == V7X_SUBLANES

    pad_rows = V7X_SUBLANES - 1
    c_all = jnp.concatenate([c_sample, c_prompt, jnp.zeros((pad_rows, D_MODEL), F32)], axis=0)
    mod = _modulation(c_all, w_ada[0], b_ada[0])
    mod3 = mod.reshape(mod.shape[0], 1, 6 * D_MODEL)

    w_in16 = w_in[0].astype(BF16)
    w_pa16 = w_pa[0].astype(BF16)
    w_pb16 = w_pb[0].astype(BF16)
    w_o16 = w_o[0].astype(BF16)
    w_gu16 = w_gu[0].astype(BF16)
    w_down16 = w_down[0].astype(BF16)
    wsp_p, bsp_p, wsp_s, bsp_s = _gm_spatial_weights(gm_ws[0], gm_bs[0], dec_seq)

    xp3 = x_prompt.reshape(seq // V7X_SUBLANES, V7X_SUBLANES, D_MODEL)
    xs3 = x_sample

    proj = functools.partial(_inproj, g1=norm1_g[0], w_in16=w_in16, qg=q_norm_g[0], kg=k_norm_g[0],
                             lng=gm_ln_g[0], lnb=gm_ln_b[0])
    tm_prompt = 512
    qT32, k16, vT16, k_out, v_out, u16, vn16, ga16, gb16, kbar_tiles = proj(xp3, mod3, prompt=True, tm=tm_prompt)
    per_tile = tm_prompt // MOBA_BLOCK
    kbar = kbar_tiles.reshape(-1, V7X_SUBLANES, ATT_WIDTH)[:, :per_tile].reshape(-1, ATT_WIDTH)
    sq32, sk32, sv32, sk_out, sv_out, su16, svn16, svn32, sga16, sgb16 = proj(xs3, mod3, prompt=False, tm=256)

    att_p = _attn_prompt(qT32, k16, vT16, kbar, rel_bias)
    tok3 = lambda a: a.reshape(n_seq, dec_seq, ATT_WIDTH)
    att_s = _attn_sample(tok3(sq32), tok3(sk32), tok3(sv32), cache_k[0], cache_v[0], page_table, rel_bias)
    att_s = att_s.reshape(n_seq * dec_seq, ATT_WIDTH)

    mrg = functools.partial(_merge, w_pa16=w_pa16, w_pb16=w_pb16, w_o16=w_o16, tm=256)
    x1p = mrg(xp3, mod3, u16, vn16, ga16, gb16, att_p, wsp_p, bsp_p, prompt=True)
    x1s = mrg(xs3, mod3, su16, svn16, sga16, sgb16, att_s, wsp_s, bsp_s, prompt=False)

    ffn = functools.partial(_ffn, g2=norm2_g[0], w_gu16=w_gu16, w_down16=w_down16, tm=512, tf=512)
    y_p = ffn(x1p, mod3, prompt=True).reshape(batch, seq, D_MODEL)
    y_s = ffn(x1s, mod3, prompt=False)

    kv_p = lambda a: a.reshape(1, batch, seq, N_HEADS, HEAD_DIM)
    kv_s = lambda a: a.reshape(1, n_seq, dec_seq, N_HEADS, HEAD_DIM)
    return (y_p, y_s, kv_p(k_out), kv_p(v_out), kv_s(sk_out), kv_s(sv_out),
            svn32.reshape(1, n_seq, dec_seq, GM_WIDTH))
```

```python
import functools
import math

import jax
import jax.numpy as jnp
from jax import lax
from jax.experimental import pallas as pl
from jax.experimental.pallas import tpu as pltpu

F32 = jnp.float32
BF16 = jnp.bfloat16

D_MODEL = 2048
N_HEADS = 8
HEAD_DIM = 128
ATT_WIDTH = N_HEADS * HEAD_DIM
GM_WIDTH = 1024
GM_GROUPS = 8
GM_CHUNK = 128
MOBA_BLOCK = 256
MOBA_TOPK = 3
NUM_BUCKETS = 32
MAX_DISTANCE = 128
PAGE_SIZE = 128
D_FF = 5632
EPS = 1e-6
SCALE = HEAD_DIM ** -0.5
LOG2E = math.log2(math.e)

V7X_LANES = 128
V7X_SUBLANES = 8
V7X_VMEM_BYTES = 64 * 1024 * 1024
VMEM_LIMIT = V7X_VMEM_BYTES - 8 * 1024 * 1024

NEG = -1e30
SEG = 1024

NT_DIMS = (((1,), (1,)), ((), ()))


def _cparams(n_axes, vmem=VMEM_LIMIT):
    return pltpu.CompilerParams(dimension_semantics=("arbitrary",) * n_axes, vmem_limit_bytes=vmem)


def _sigmoid(x):
    return 1.0 / (1.0 + jnp.exp(-x))


def _gelu_tanh(x):
    c = math.sqrt(2.0 / math.pi)
    return x * (0.5 * (1.0 + jnp.tanh(c * (x + 0.044715 * (x * x * x)))))


def _t5_bucket(rel):
    n = jnp.maximum(rel, 0)
    max_exact = NUM_BUCKETS // 2
    nf = jnp.maximum(n, 1).astype(F32)
    large = max_exact + (jnp.log(nf / max_exact) / math.log(MAX_DISTANCE / max_exact)
                         * (NUM_BUCKETS - max_exact)).astype(jnp.int32)
    large = jnp.minimum(large, NUM_BUCKETS - 1)
    return jnp.where(n < max_exact, n, large)


def _mod_kernel(c_ref, w_ref, b_ref, o_ref):
    c = c_ref[...]
    s = (c * _sigmoid(c)).astype(BF16)
    o_ref[...] = jnp.dot(s, w_ref[...].astype(BF16), preferred_element_type=F32) + b_ref[...]


def _modulation(c_all, w_ada, b_ada):
    rows = c_all.shape[0]
    n_out = w_ada.shape[1]
    tn = 1024
    return pl.pallas_call(
        _mod_kernel,
        out_shape=jax.ShapeDtypeStruct((rows, n_out), F32),
        grid=(n_out // tn,),
        in_specs=[pl.BlockSpec((rows, D_MODEL), lambda j: (0, 0)),
                  pl.BlockSpec((D_MODEL, tn), lambda j: (0, j)),
                  pl.BlockSpec((1, tn), lambda j: (0, j))],
        out_specs=pl.BlockSpec((rows, tn), lambda j: (0, j)),
        compiler_params=_cparams(1),
        name="mod",
    )(c_all, w_ada, b_ada.reshape(1, n_out))


_SEG_K, _SEG_Q, _SEG_V, _SEG_U, _SEG_VG, _SEG_GA0, _SEG_GA1, _SEG_GB0, _SEG_GB1 = range(9)


def _inproj_kernel(x_ref, sh_ref, sc_ref, g1_ref, w_ref, qg_ref, kg_ref, lng_ref, lnb_ref,
                   *rest, tm, prompt):
    if prompt:
        qT32, k16, vT16, k_out, v_out, u16, vn16, ga16, gb16, kbar, h_scr = rest
        q32 = k32 = v32 = None
    else:
        q32, k32, v32, k_out, v_out, u16, vn16, vn32, ga16, gb16, h_scr = rest
        qT32 = k16 = vT16 = None
    j = pl.program_id(1)

    def normed_input():
        x = x_ref[...]
        ms = jnp.mean(x * x, axis=-1, keepdims=True)
        y = x * lax.rsqrt(ms + EPS) * g1_ref[...]
        h = y * (1.0 + sc_ref[...]) + sh_ref[...]
        return h.reshape(tm, D_MODEL).astype(BF16)

    def matmul(h=None):
        lhs = h_scr[...] if h is None else h
        return jnp.dot(lhs, w_ref[...], preferred_element_type=F32)

    def store_heads(z, y_of_head, dst32, dst16, dst_out, with_means, dst_t=None):
        for h in range(N_HEADS):
            sl = slice(h * HEAD_DIM, (h + 1) * HEAD_DIM)
            yh = y_of_head(z[:, sl])
            if dst32 is not None:
                dst32[:, sl] = yh
            if dst16 is not None:
                dst16[:, sl] = yh.astype(BF16)
            if dst_t is not None:
                dst_t[sl, :] = yh.T.astype(dst_t.dtype)
            if dst_out is not None:
                dst_out[pl.ds(h, tm, stride=N_HEADS), :] = yh
            if with_means:
                for r in range(tm // MOBA_BLOCK):
                    blk = yh[r * MOBA_BLOCK:(r + 1) * MOBA_BLOCK]
                    kbar[r:r + 1, sl] = jnp.mean(blk, axis=0, keepdims=True)

    def head_norm(g_ref):
        def f(zh):
            ms = jnp.mean(zh * zh, axis=-1, keepdims=True)
            return zh * lax.rsqrt(ms + EPS) * g_ref[...]
        return f

    @pl.when(j == _SEG_K)
    def _():
        h = normed_input()
        h_scr[...] = h
        if prompt:
            kbar[...] = jnp.zeros_like(kbar)
        store_heads(matmul(h), head_norm(kg_ref), k32, k16, k_out, prompt)

    @pl.when(j == _SEG_Q)
    def _():
        store_heads(matmul(), head_norm(qg_ref), q32, None, None, False, dst_t=qT32)

    @pl.when(j == _SEG_V)
    def _():
        store_heads(matmul(), lambda zh: zh, v32, None, v_out, False, dst_t=vT16)

    @pl.when(j == _SEG_U)
    def _():
        u16[...] = _gelu_tanh(matmul()).astype(BF16)

    @pl.when(j == _SEG_VG)
    def _():
        a = _gelu_tanh(matmul())
        mu = jnp.mean(a, axis=-1, keepdims=True)
        ac = a - mu
        y = ac * lax.rsqrt(jnp.mean(ac * ac, axis=-1, keepdims=True) + EPS)
        y = y * lng_ref[...] + lnb_ref[...]
        vn16[...] = y.astype(BF16)
        if not prompt:
            vn32[...] = y

    @pl.when((j == _SEG_GA0) | (j == _SEG_GA1))
    def _():
        ga16[...] = _sigmoid(matmul()).astype(BF16)

    @pl.when((j == _SEG_GB0) | (j == _SEG_GB1))
    def _():
        gb16[...] = _sigmoid(matmul()).astype(BF16)


def _mod_spec(g_rows, chunk, prompt, n_grid):
    if n_grid == 2:
        if prompt:
            return pl.BlockSpec((1, 1, D_MODEL), lambda i, j: (_PROMPT_MOD_ROW, 0, chunk))
        return pl.BlockSpec((g_rows, 1, D_MODEL), lambda i, j: (i, 0, chunk))
    if prompt:
        return pl.BlockSpec((1, 1, D_MODEL), lambda i: (_PROMPT_MOD_ROW, 0, chunk))
    return pl.BlockSpec((g_rows, 1, D_MODEL), lambda i: (i, 0, chunk))


_PROMPT_MOD_ROW = 128


def _w_in_col(j):
    return jnp.where(j < 2, 1 - j, j)


def _inproj(x3, mod3, g1, w_in16, qg, kg, lng, lnb, *, prompt, tm):
    n_rows = x3.shape[0] * V7X_SUBLANES
    g_rows = tm // V7X_SUBLANES
    grid = (n_rows // tm, 9)
    row_blk = lambda i, j: (i, 0)
    f32_seg = jax.ShapeDtypeStruct((n_rows, SEG), F32)
    b16_seg = jax.ShapeDtypeStruct((n_rows, SEG), BF16)
    b16_gate = jax.ShapeDtypeStruct((n_rows, D_MODEL), BF16)
    seg_spec = pl.BlockSpec((tm, SEG), row_blk)
    kv_out = jax.ShapeDtypeStruct((n_rows * N_HEADS, HEAD_DIM), F32)
    kv_spec = pl.BlockSpec((tm * N_HEADS, HEAD_DIM), row_blk)
    ga_spec = pl.BlockSpec((tm, SEG), lambda i, j: (i, jnp.clip(j - _SEG_GA0, 0, 1)))
    gb_spec = pl.BlockSpec((tm, SEG), lambda i, j: (i, jnp.clip(j - _SEG_GB0, 0, 1)))
    if prompt:
        assert tm % MOBA_BLOCK == 0 and tm // MOBA_BLOCK <= V7X_SUBLANES
        t_spec = pl.BlockSpec((SEG, tm), lambda i, j: (0, i))
        out_shape = [jax.ShapeDtypeStruct((SEG, n_rows), F32), b16_seg, jax.ShapeDtypeStruct((SEG, n_rows), BF16),
                     kv_out, kv_out, b16_seg, b16_seg, b16_gate, b16_gate,
                     jax.ShapeDtypeStruct((grid[0] * V7X_SUBLANES, SEG), F32)]
        out_specs = ([t_spec, seg_spec, t_spec] + [kv_spec] * 2 + [seg_spec] * 2
                     + [ga_spec, gb_spec, pl.BlockSpec((V7X_SUBLANES, SEG), row_blk)])
    else:
        out_shape = [f32_seg, f32_seg, f32_seg, kv_out, kv_out, b16_seg, b16_seg, f32_seg, b16_gate, b16_gate]
        out_specs = [seg_spec] * 3 + [kv_spec] * 2 + [seg_spec] * 3 + [ga_spec, gb_spec]
    vec = lambda n: pl.BlockSpec((1, n), lambda i, j: (0, 0))
    return pl.pallas_call(
        functools.partial(_inproj_kernel, tm=tm, prompt=prompt),
        out_shape=out_shape,
        grid=grid,
        in_specs=[pl.BlockSpec((g_rows, V7X_SUBLANES, D_MODEL), lambda i, j: (i, 0, 0)),
                  _mod_spec(g_rows, 0, prompt, 2),
                  _mod_spec(g_rows, 1, prompt, 2),
                  pl.BlockSpec((1, 1, D_MODEL), lambda i, j: (0, 0, 0)),
                  pl.BlockSpec((D_MODEL, SEG), lambda i, j: (0, _w_in_col(j))),
                  vec(HEAD_DIM), vec(HEAD_DIM), vec(GM_WIDTH), vec(GM_WIDTH)],
        out_specs=out_specs,
        scratch_shapes=[pltpu.VMEM((tm, D_MODEL), BF16)],
        compiler_params=_cparams(2),
        name="inproj_prompt" if prompt else "inproj_sample",
    )(x3, mod3, mod3, g1.reshape(1, 1, D_MODEL), w_in16, qg.reshape(1, HEAD_DIM), kg.reshape(1, HEAD_DIM),
      lng.reshape(1, GM_WIDTH), lnb.reshape(1, GM_WIDTH))


def _select_topk(scores, n_valid, axis):
    idx = lax.broadcasted_iota(jnp.int32, scores.shape, axis)
    idx_f = idx.astype(F32)
    cand = idx < n_valid
    sel = jnp.zeros(scores.shape, jnp.bool_)
    for _ in range(MOBA_TOPK):
        s_m = jnp.where(cand, scores, -jnp.inf)
        mx = jnp.max(s_m, axis=axis, keepdims=True)
        first = jnp.min(jnp.where(cand & (s_m == mx), idx_f, 1e9), axis=axis, keepdims=True)
        pick = idx_f == first
        sel = sel | pick
        cand = cand & jnp.logical_not(pick)
    return sel


_FAR_GROUP = 4
_SUM_ROWS = 16
_HEADS_PER_STEP = 4


def _attn_prompt_kernel(rb_ref, qT_ref, k_ref, vT_ref, kbar_ref, e_ref, bko_ref, bkp_ref, o_ref,
                        town, tprev, m_scr, r_scr):
    hp = pl.program_id(0)
    j = pl.program_id(1)
    tq = MOBA_BLOCK
    n_sel_rows = kbar_ref.shape[0]
    heads = range(_HEADS_PER_STEP)
    cols = [slice(i * HEAD_DIM, (i + 1) * HEAD_DIM) for i in heads]

    @pl.when(j == 0)
    def _():
        bo = bko_ref[...]
        bp = bkp_ref[...]
        key = lax.broadcasted_iota(jnp.int32, (tq, tq), 0)
        qry = lax.broadcasted_iota(jnp.int32, (tq, tq), 1)
        for i in heads:
            h = hp * _HEADS_PER_STEP + i
            b_far = rb_ref[h, NUM_BUCKETS - 1]
            to = jnp.zeros((tq, tq), F32)
            tp = jnp.zeros((tq, tq), F32)
            for b in range(NUM_BUCKETS - 1):
                val = (rb_ref[h, b] - b_far) * LOG2E
                to = jnp.where(bo == b, val, to)
                tp = jnp.where(bp == b, val, tp)
            town[i] = jnp.where(key <= qry, to, NEG)
            tprev[i] = tp

    q_sT, q_catT, prev_mask = [], [], []
    for i in heads:
        qT = qT_ref[cols[i], :]
        scoresT = jnp.dot(kbar_ref[:, cols[i]], qT, precision=lax.Precision.HIGHEST,
                          preferred_element_type=F32)
        selT = _select_topk(scoresT, j, axis=0)
        blk = lax.broadcasted_iota(jnp.int32, selT.shape, 0)
        qs = (qT * (SCALE * LOG2E)).astype(BF16)
        aug = jnp.where(selT & (blk < j - 1), 0.0, NEG)
        aug = jnp.concatenate([aug, jnp.full((HEAD_DIM - n_sel_rows, tq), NEG, F32)], axis=0)
        q_sT.append(qs)
        q_catT.append(jnp.concatenate([qs, aug.astype(BF16)], axis=0))
        prev_sel = jnp.max(jnp.where(selT & (blk == j - 1), 1.0, 0.0), axis=0, keepdims=True) > 0.0
        prev_mask.append(jnp.where(prev_sel, 0.0, NEG))

    ones_rows = jnp.ones((_SUM_ROWS, tq), BF16)

    def partial_softmax(logits, where):
        ms = [jnp.max(s, axis=0, keepdims=True) for s in logits]
        ps = [jnp.exp2(s - m).astype(BF16) for s, m in zip(logits, ms)]
        rs = []
        for p, (i, start) in zip(ps, where):
            vT_aug = jnp.concatenate([vT_ref[cols[i], pl.ds(start, tq)], ones_rows], axis=0)
            rs.append(jnp.dot(vT_aug, p, preferred_element_type=F32))
        return list(zip(ms, rs))

    def combine(m_run, r_run, parts):
        m_new = m_run
        for m, _ in parts:
            m_new = jnp.maximum(m_new, m)
        r_new = r_run * jnp.exp2(m_run - m_new)
        for m, r in parts:
            r_new = r_new + r * jnp.exp2(m - m_new)
        return m_new, r_new

    own_start = pl.multiple_of(j * tq, tq)
    prev_start = pl.multiple_of(jnp.maximum(j - 1, 0) * tq, tq)
    logits, where = [], []
    for i in heads:
        logits.append(jnp.dot(k_ref[pl.ds(own_start, tq), cols[i]], q_sT[i], preferred_element_type=F32)
                      + town[i])
        logits.append(jnp.dot(k_ref[pl.ds(prev_start, tq), cols[i]], q_sT[i], preferred_element_type=F32)
                      + (tprev[i] + prev_mask[i]))
        where += [(i, own_start), (i, prev_start)]
    parts = partial_softmax(logits, where)
    for i in heads:
        (m_own, r_own), part_prev = parts[2 * i], parts[2 * i + 1]
        m_scr[i], r_scr[i] = combine(m_own, r_own, [part_prev])

    def far_body(g, carry):
        starts = [pl.multiple_of((g * _FAR_GROUP + b) * tq, tq) for b in range(_FAR_GROUP)]
        logits, where = [], []
        for i in heads:
            for start in starts:
                k_cat = jnp.concatenate([k_ref[pl.ds(start, tq), cols[i]], e_ref[pl.ds(start, tq), :]], axis=1)
                logits.append(jnp.dot(k_cat, q_catT[i], preferred_element_type=F32))
                where.append((i, start))
        parts = partial_softmax(logits, where)
        for i in heads:
            m_scr[i], r_scr[i] = combine(m_scr[i], r_scr[i], parts[i * _FAR_GROUP:(i + 1) * _FAR_GROUP])
        return carry

    lax.fori_loop(0, (j + _FAR_GROUP - 2) // _FAR_GROUP, far_body, 0)

    for i in heads:
        r = r_scr[i]
        o_ref[:, cols[i]] = (r[:HEAD_DIM] / r[HEAD_DIM:HEAD_DIM + 1]).T.astype(o_ref.dtype)


def _attn_prompt(qT32, k16, vT16, kbar, rel_bias):
    seq = k16.shape[0]
    tq = MOBA_BLOCK
    n_blocks = seq // tq
    assert n_blocks % _FAR_GROUP == 0 and kbar.shape[0] == n_blocks
    pos = jnp.arange(seq, dtype=jnp.int32)
    onehot = (pos[:, None] // tq == jnp.arange(HEAD_DIM, dtype=jnp.int32)[None, :]).astype(BF16)
    r = jnp.arange(tq, dtype=jnp.int32)
    bucket_own = _t5_bucket(r[None, :] - r[:, None])
    bucket_prev = _t5_bucket(tq + r[None, :] - r[:, None])
    const = lambda h, j: (0, 0)
    hps = _HEADS_PER_STEP
    width = hps * HEAD_DIM
    return pl.pallas_call(
        _attn_prompt_kernel,
        out_shape=jax.ShapeDtypeStruct((seq, ATT_WIDTH), BF16),
        grid=(N_HEADS // hps, n_blocks),
        in_specs=[pl.BlockSpec(memory_space=pltpu.SMEM),
                  pl.BlockSpec((width, tq), lambda h, j: (h, j)),
                  pl.BlockSpec((seq, width), lambda h, j: (0, h)),
                  pl.BlockSpec((width, seq), lambda h, j: (h, 0)),
                  pl.BlockSpec((n_blocks, width), lambda h, j: (0, h)),
                  pl.BlockSpec((seq, HEAD_DIM), const),
                  pl.BlockSpec((tq, tq), const),
                  pl.BlockSpec((tq, tq), const)],
        out_specs=pl.BlockSpec((tq, width), lambda h, j: (j, h)),
        scratch_shapes=[pltpu.VMEM((hps, tq, tq), F32), pltpu.VMEM((hps, tq, tq), F32),
                        pltpu.VMEM((hps, 1, tq), F32), pltpu.VMEM((hps, HEAD_DIM + _SUM_ROWS, tq), F32)],
        compiler_params=_cparams(2),
        name="attn_prompt",
    )(rel_bias.T, qT32, k16, vT16, kbar, onehot, bucket_own, bucket_prev)


_SAMPLE_GROUP = 2


def _attn_sample_kernel(pt_ref, q_ref, kn_ref, vn_ref, rbrows_ref, bkp_ref, *rest, n_pages, dec_seq):
    k_pages = rest[:n_pages]
    v_pages = rest[n_pages:2 * n_pages]
    o_ref, k16, v16, kbar_scr, bprev_scr, bown_scr = rest[2 * n_pages:]
    b = pl.program_id(0)
    past = n_pages * PAGE_SIZE
    n_past_blocks = past // MOBA_BLOCK
    rows = N_HEADS * dec_seq
    pages_per_block = MOBA_BLOCK // PAGE_SIZE

    @pl.when(b == 0)
    def _():
        kbar_scr[...] = jnp.zeros_like(kbar_scr)
        rb = rbrows_ref[...]
        bp = bkp_ref[...]
        tp = jnp.zeros((rows, MOBA_BLOCK), F32)
        for bb in range(NUM_BUCKETS - 1):
            tp = jnp.where(bp == bb, rb[:, bb:bb + 1], tp)
        bprev_scr[...] = tp
        t_row = lax.broadcasted_iota(jnp.int32, (rows, V7X_LANES), 0) & (dec_seq - 1)
        c_col = lax.broadcasted_iota(jnp.int32, (rows, V7X_LANES), 1)
        rel = t_row - c_col
        to = jnp.zeros((rows, V7X_LANES), F32)
        for bb in range(dec_seq):
            to = jnp.where(rel == bb, rb[:, bb:bb + 1], to)
        bown_scr[...] = jnp.where(rel >= 0, to, NEG)

    q = q_ref[...]
    q_rep = jnp.concatenate([q] * N_HEADS, axis=0)
    row_head = lax.broadcasted_iota(jnp.int32, (rows, ATT_WIDTH), 0) >> int(math.log2(dec_seq))
    col_head = lax.broadcasted_iota(jnp.int32, (rows, ATT_WIDTH), 1) >> int(math.log2(HEAD_DIM))
    q_bd = jnp.where(row_head == col_head, q_rep, 0.0)
    q_s = (q_bd * SCALE).astype(BF16)

    def to_head_lanes(pages, dst, n, sums_ref):
        for h in range(N_HEADS):
            cols = slice(h * HEAD_DIM, (h + 1) * HEAD_DIM)
            total = None
            for p in range(n * pages_per_block, (n + 1) * pages_per_block):
                xh = pages[p][pl.ds(h, PAGE_SIZE, stride=N_HEADS), :]
                dst[p * PAGE_SIZE:(p + 1) * PAGE_SIZE, cols] = xh.astype(BF16)
                if sums_ref is not None:
                    cs = jnp.sum(xh, axis=0, keepdims=True)
                    total = cs if total is None else total + cs
            if sums_ref is not None:
                sums_ref[n:n + 1, cols] = total * (1.0 / MOBA_BLOCK)

    def new_rows(ref):
        pad = jnp.zeros((V7X_LANES - dec_seq, ATT_WIDTH), F32)
        return jnp.concatenate([ref[...], pad], axis=0).astype(BF16)

    groups = [slice(g * MOBA_BLOCK, (g + _SAMPLE_GROUP) * MOBA_BLOCK)
              for g in range(0, n_past_blocks, _SAMPLE_GROUP)]
    new_keys = slice(past, past + V7X_LANES)
    raw = []
    for keys in groups:
        for n in range(keys.start // MOBA_BLOCK, keys.stop // MOBA_BLOCK):
            to_head_lanes(k_pages, k16, n, kbar_scr)
        raw.append(lax.dot_general(q_s, k16[keys, :], NT_DIMS, preferred_element_type=F32))
    k16[new_keys, :] = new_rows(kn_ref)
    raw.append(lax.dot_general(q_s, k16[new_keys, :], NT_DIMS, preferred_element_type=F32))

    assert n_past_blocks == V7X_SUBLANES
    q_pad = jnp.concatenate([q_bd, jnp.zeros((V7X_LANES - rows, ATT_WIDTH), F32)], axis=0)
    scores_t = lax.dot_general(kbar_scr[0:n_past_blocks, :], q_pad, NT_DIMS, precision=lax.Precision.HIGHEST,
                               preferred_element_type=F32)
    blk = lax.broadcasted_iota(jnp.int32, scores_t.shape, 0)
    rank = jnp.zeros(scores_t.shape, F32)
    for k in range(1, n_past_blocks):
        other = pltpu.roll(scores_t, k, 0)
        ge = jnp.where(other >= scores_t, 1.0, 0.0)
        gt = jnp.where(other > scores_t, 1.0, 0.0)
        rank = rank + jnp.where(blk >= k, ge, gt)
    sel_t = jnp.where(rank < MOBA_TOPK, 1.0, 0.0)
    eye = jnp.where(lax.broadcasted_iota(jnp.int32, (rows, V7X_LANES), 0)
                    == lax.broadcasted_iota(jnp.int32, (rows, V7X_LANES), 1), 1.0, 0.0).astype(BF16)
    sel_pad = jnp.concatenate([sel_t, jnp.zeros((V7X_LANES - n_past_blocks, V7X_LANES), F32)], axis=0)
    sel = lax.dot_general(eye, sel_pad.astype(BF16), NT_DIMS, preferred_element_type=F32)
    mask_add = jnp.where(sel > 0.5, 0.0, NEG)

    pieces = []
    for keys, piece in zip(groups, raw):
        cols = []
        for i, n in enumerate(range(keys.start // MOBA_BLOCK, keys.stop // MOBA_BLOCK)):
            blk = piece[:, i * MOBA_BLOCK:(i + 1) * MOBA_BLOCK] + mask_add[:, n:n + 1]
            if n == n_past_blocks - 1:
                blk = blk + bprev_scr[...]
            cols.append(blk)
        pieces.append(jnp.concatenate(cols, axis=1))
    pieces.append(raw[-1] + bown_scr[...])
    m = pieces[0].max(axis=-1, keepdims=True)
    for piece in pieces[1:]:
        m = jnp.maximum(m, piece.max(axis=-1, keepdims=True))
    probs = [jnp.exp(piece - m) for piece in pieces]
    l = sum(p.sum(axis=-1, keepdims=True) for p in probs)

    out = None
    for keys, p in zip(groups + [new_keys], probs):
        if keys is new_keys:
            v16[keys, :] = new_rows(vn_ref)
        else:
            for n in range(keys.start // MOBA_BLOCK, keys.stop // MOBA_BLOCK):
                to_head_lanes(v_pages, v16, n, None)
        part = jnp.dot(p.astype(BF16), v16[keys, :], preferred_element_type=F32)
        out = part if out is None else out + part
    out = out / l
    for h in range(N_HEADS):
        sl = slice(h * HEAD_DIM, (h + 1) * HEAD_DIM)
        o_ref[:, sl] = out[h * dec_seq:(h + 1) * dec_seq, sl]


def _attn_sample(q3, k3, v3, cache_k, cache_v, page_table, rel_bias):
    n_seq, dec_seq, _ = q3.shape
    n_pages = page_table.shape[1]
    past = n_pages * PAGE_SIZE
    rows = N_HEADS * dec_seq
    page_rows = PAGE_SIZE * N_HEADS
    ck = cache_k.reshape(-1, HEAD_DIM)
    cv = cache_v.reshape(-1, HEAD_DIM)
    rb_rel = rel_bias.T - rel_bias[NUM_BUCKETS - 1][:, None]
    rbrows = jnp.repeat(rb_rel, dec_seq, axis=0)
    t = jnp.arange(rows, dtype=jnp.int32) % dec_seq
    c = jnp.arange(MOBA_BLOCK, dtype=jnp.int32)
    bucket_prev = _t5_bucket(MOBA_BLOCK + t[:, None] - c[None, :])
    tok = pl.BlockSpec((None, dec_seq, ATT_WIDTH), lambda b, pt: (b, 0, 0))
    const2 = lambda shape: pl.BlockSpec(shape, lambda b, pt: (0, 0))

    def page_spec(p):
        return pl.BlockSpec((page_rows, HEAD_DIM), lambda b, pt: (pt[b, p], 0))

    grid_spec = pltpu.PrefetchScalarGridSpec(
        num_scalar_prefetch=1,
        grid=(n_seq,),
        in_specs=[tok, tok, tok, const2((rows, NUM_BUCKETS)), const2((rows, MOBA_BLOCK))]
                 + [page_spec(p) for p in range(n_pages)] * 2,
        out_specs=tok,
        scratch_shapes=[pltpu.VMEM((past + V7X_LANES, ATT_WIDTH), BF16),
                        pltpu.VMEM((past + V7X_LANES, ATT_WIDTH), BF16),
                        pltpu.VMEM((V7X_LANES, ATT_WIDTH), F32),
                        pltpu.VMEM((rows, MOBA_BLOCK), F32),
                        pltpu.VMEM((rows, V7X_LANES), F32)],
    )
    return pl.pallas_call(
        functools.partial(_attn_sample_kernel, n_pages=n_pages, dec_seq=dec_seq),
        out_shape=jax.ShapeDtypeStruct((n_seq, dec_seq, ATT_WIDTH), F32),
        grid_spec=grid_spec,
        compiler_params=_cparams(1),
        name="attn_sample",
    )(page_table, q3, k3, v3, rbrows, bucket_prev, *([ck] * n_pages), *([cv] * n_pages))


def _merge_kernel(x_ref, gt_ref, u_ref, vn_ref, ga_ref, gb_ref, att_ref, wsp_ref, bsp_ref,
                  wpa_ref, wpb_ref, wo_ref, o_ref, gm_scr, *, tm):
    for c in range(tm // GM_CHUNK):
        rows = slice(c * GM_CHUNK, (c + 1) * GM_CHUNK)
        for g in range(GM_GROUPS):
            cols = slice(g * GM_CHUNK, (g + 1) * GM_CHUNK)
            s = jnp.dot(wsp_ref[g], vn_ref[rows, cols], preferred_element_type=F32) + bsp_ref[:, cols]
            gm_scr[rows, cols] = (u_ref[rows, cols].astype(F32) * s).astype(BF16)
    a = jnp.dot(gm_scr[...], wpa_ref[...], preferred_element_type=F32)
    b = jnp.dot(att_ref[...].astype(BF16), wpb_ref[...], preferred_element_type=F32)
    merged = ga_ref[...].astype(F32) * a + gb_ref[...].astype(F32) * b
    c_out = jnp.dot(merged.astype(BF16), wo_ref[...], preferred_element_type=F32)
    o_ref[...] = x_ref[...] + gt_ref[...] * c_out.reshape(o_ref.shape)


def _merge(x3, mod3, u16, vn16, ga16, gb16, att, w_sp16, b_sp, w_pa16, w_pb16, w_o16, *, prompt, tm):
    n_rows = x3.shape[0] * V7X_SUBLANES
    g_rows = tm // V7X_SUBLANES
    row3 = pl.BlockSpec((g_rows, V7X_SUBLANES, D_MODEL), lambda i: (i, 0, 0))
    seg = pl.BlockSpec((tm, SEG), lambda i: (i, 0))
    gate = pl.BlockSpec((tm, D_MODEL), lambda i: (i, 0))
    resident = lambda shape: pl.BlockSpec(shape, lambda i: (0,) * len(shape), pipeline_mode=pl.Buffered(1))
    return pl.pallas_call(
        functools.partial(_merge_kernel, tm=tm),
        out_shape=jax.ShapeDtypeStruct(x3.shape, F32),
        grid=(n_rows // tm,),
        in_specs=[row3, _mod_spec(g_rows, 2, prompt, 1), seg, seg, gate, gate, seg,
                  resident((GM_GROUPS, GM_CHUNK, GM_CHUNK)), resident((GM_CHUNK, GM_WIDTH)),
                  resident((GM_WIDTH, D_MODEL)), resident((ATT_WIDTH, D_MODEL)), resident((D_MODEL, D_MODEL))],
        out_specs=row3,
        scratch_shapes=[pltpu.VMEM((tm, GM_WIDTH), BF16)],
        compiler_params=_cparams(1),
        name="merge_prompt" if prompt else "merge_sample",
    )(x3, mod3, u16, vn16, ga16, gb16, att, w_sp16, b_sp, w_pa16, w_pb16, w_o16)


def _ffn_kernel(x_ref, sh_ref, sc_ref, gt_ref, g2_ref, wg_ref, wu_ref, wd_ref, o_ref, h_scr, acc_scr, *, tm):
    j = pl.program_id(1)
    last = pl.num_programs(1) - 1

    def chunk(h):
        g = jnp.dot(h, wg_ref[...], preferred_element_type=F32)
        u = jnp.dot(h, wu_ref[...], preferred_element_type=F32)
        a = (g * _sigmoid(g) * u).astype(BF16)
        return jnp.dot(a, wd_ref[...], preferred_element_type=F32)

    @pl.when(j == 0)
    def _():
        x = x_ref[...]
        ms = jnp.mean(x * x, axis=-1, keepdims=True)
        y = x * lax.rsqrt(ms + EPS) * g2_ref[...]
        h = (y * (1.0 + sc_ref[...]) + sh_ref[...]).reshape(tm, D_MODEL).astype(BF16)
        h_scr[...] = h
        acc_scr[...] = chunk(h)

    @pl.when((j > 0) & (j < last))
    def _():
        acc_scr[...] += chunk(h_scr[...])

    @pl.when(j == last)
    def _():
        total = acc_scr[...] + chunk(h_scr[...])
        o_ref[...] = x_ref[...] + gt_ref[...] * total.reshape(o_ref.shape)


def _ffn(x3, mod3, g2, w_gu16, w_down16, *, prompt, tm, tf):
    n_rows = x3.shape[0] * V7X_SUBLANES
    g_rows = tm // V7X_SUBLANES
    n_chunks = D_FF // tf
    row3 = pl.BlockSpec((g_rows, V7X_SUBLANES, D_MODEL), lambda i, j: (i, 0, 0))
    return pl.pallas_call(
        functools.partial(_ffn_kernel, tm=tm),
        out_shape=jax.ShapeDtypeStruct(x3.shape, F32),
        grid=(n_rows // tm, n_chunks),
        in_specs=[row3, _mod_spec(g_rows, 3, prompt, 2), _mod_spec(g_rows, 4, prompt, 2),
                  _mod_spec(g_rows, 5, prompt, 2),
                  pl.BlockSpec((1, 1, D_MODEL), lambda i, j: (0, 0, 0)),
                  pl.BlockSpec((D_MODEL, tf), lambda i, j: (0, j)),
                  pl.BlockSpec((D_MODEL, tf), lambda i, j: (0, n_chunks + j)),
                  pl.BlockSpec((tf, D_MODEL), lambda i, j: (j, 0))],
        out_specs=row3,
        scratch_shapes=[pltpu.VMEM((tm, D_MODEL), BF16), pltpu.VMEM((tm, D_MODEL), F32)],
        compiler_params=_cparams(2),
        name="ffn_prompt" if prompt else "ffn_sample",
    )(x3, mod3, mod3, mod3, g2.reshape(1, 1, D_MODEL), w_gu16, w_gu16, w_down16)


def _gm_spatial_weights(gm_ws, gm_bs, dec_seq):
    tril = jnp.tril(jnp.ones((GM_CHUNK, GM_CHUNK), dtype=bool))
    w = jnp.where(tril[None], gm_ws, jnp.zeros_like(gm_ws))
    w_prompt = w.astype(BF16)
    b_prompt = jnp.repeat(gm_bs.T, GM_WIDTH // GM_GROUPS, axis=1)
    reps = GM_CHUNK // dec_seq
    eye = jnp.eye(reps, dtype=F32)
    w_small = w[:, :dec_seq, :dec_seq]
    w_sample = jnp.einsum('ab,gts->gatbs', eye, w_small).reshape(GM_GROUPS, GM_CHUNK, GM_CHUNK).astype(BF16)
    b_sample = jnp.tile(jnp.repeat(gm_bs[:, :dec_seq].T, GM_WIDTH // GM_GROUPS, axis=1), (reps, 1))
    return w_prompt, b_prompt, w_sample, b_sample


def kernel(x_prompt, x_sample, c_prompt, c_sample, cache_k, cache_v, page_table, w_ada, b_ada, norm1_g, w_in,
           q_norm_g, k_norm_g, gm_ln_g, gm_ln_b, gm_ws, gm_bs, w_pa, w_pb, w_o, norm2_g, w_gu, w_down,
           rel_bias):
    depth = w_in.shape[0]
    assert depth == 1, "single-layer step"
    n_seq, dec_seq, _ = x_sample.shape
    batch, seq, _ = x_prompt.shape
    assert batch == 1 and n_seq == _PROMPT_MOD_ROW and dec_seq == V7X_SUBLANES

    pad_rows = V7X_SUBLANES - 1
    c_all = jnp.concatenate([c_sample, c_prompt, jnp.zeros((pad_rows, D_MODEL), F32)], axis=0)
    mod = _modulation(c_all, w_ada[0], b_ada[0])
    mod3 = mod.reshape(mod.shape[0], 1, 6 * D_MODEL)

    w_in16 = w_in[0].astype(BF16)
    w_pa16 = w_pa[0].astype(BF16)
    w_pb16 = w_pb[0].astype(BF16)
    w_o16 = w_o[0].astype(BF16)
    w_gu16 = w_gu[0].astype(BF16)
    w_down16 = w_down[0].astype(BF16)
    wsp_p, bsp_p, wsp_s, bsp_s = _gm_spatial_weights(gm_ws[0], gm_bs[0], dec_seq)

    xp3 = x_prompt.reshape(seq // V7X_SUBLANES, V7X_SUBLANES, D_MODEL)
    xs3 = x_sample

    proj = functools.partial(_inproj, g1=norm1_g[0], w_in16=w_in16, qg=q_norm_g[0], kg=k_norm_g[0],
                             lng=gm_ln_g[0], lnb=gm_ln_b[0])
    tm_prompt = 512
    qT32, k16, vT16, k_out, v_out, u16, vn16, ga16, gb16, kbar_tiles = proj(xp3, mod3, prompt=True, tm=tm_prompt)
    per_tile = tm_prompt // MOBA_BLOCK
    kbar = kbar_tiles.reshape(-1, V7X_SUBLANES, ATT_WIDTH)[:, :per_tile].reshape(-1, ATT_WIDTH)
    sq32, sk32, sv32, sk_out, sv_out, su16, svn16, svn32, sga16, sgb16 = proj(xs3, mod3, prompt=False, tm=256)

    att_p = _attn_prompt(qT32, k16, vT16, kbar, rel_bias)
    tok3 = lambda a: a.reshape(n_seq, dec_seq, ATT_WIDTH)
    att_s = _attn_sample(tok3(sq32), tok3(sk32), tok3(sv32), cache_k[0], cache_v[0], page_table, rel_bias)
    att_s = att_s.reshape(n_seq * dec_seq, ATT_WIDTH)

    mrg = functools.partial(_merge, w_pa16=w_pa16, w_pb16=w_pb16, w_o16=w_o16, tm=256)
    x1p = mrg(xp3, mod3, u16, vn16, ga16, gb16, att_p, wsp_p, bsp_p, prompt=True)
    x1s = mrg(xs3, mod3, su16, svn16, sga16, sgb16, att_s, wsp_s, bsp_s, prompt=False)

    ffn = functools.partial(_ffn, g2=norm2_g[0], w_gu16=w_gu16, w_down16=w_down16, tm=512, tf=512)
    y_p = ffn(x1p, mod3, prompt=True).reshape(batch, seq, D_MODEL)
    y_s = ffn(x1s, mod3, prompt=False)

    kv_p = lambda a: a.reshape(1, batch, seq, N_HEADS, HEAD_DIM)
    kv_s = lambda a: a.reshape(1, n_seq, dec_seq, N_HEADS, HEAD_DIM)
    return (y_p, y_s, kv_p(k_out), kv_p(v_out), kv_s(sk_out), kv_s(sv_out),
            svn32.reshape(1, n_seq, dec_seq, GM_WIDTH))
```

```python
import functools
import math

import jax
import jax.numpy as jnp
from jax import lax
from jax.experimental import pallas as pl
from jax.experimental.pallas import tpu as pltpu

F32 = jnp.float32
BF16 = jnp.bfloat16

D_MODEL = 2048
N_HEADS = 8
HEAD_DIM = 128
ATT_WIDTH = N_HEADS * HEAD_DIM
GM_WIDTH = 1024
GM_GROUPS = 8
GM_CHUNK = 128
MOBA_BLOCK = 256
MOBA_TOPK = 3
NUM_BUCKETS = 32
MAX_DISTANCE = 128
PAGE_SIZE = 128
D_FF = 5632
EPS = 1e-6
SCALE = HEAD_DIM ** -0.5
LOG2E = math.log2(math.e)

V7X_LANES = 128
V7X_SUBLANES = 8
V7X_VMEM_BYTES = 64 * 1024 * 1024
VMEM_LIMIT = V7X_VMEM_BYTES - 8 * 1024 * 1024

NEG = -1e30
SEG = 1024

NT_DIMS = (((1,), (1,)), ((), ()))


def _cparams(n_axes, vmem=VMEM_LIMIT):
    return pltpu.CompilerParams(dimension_semantics=("arbitrary",) * n_axes, vmem_limit_bytes=vmem)


def _sigmoid(x):
    return 1.0 / (1.0 + jnp.exp(-x))


def _gelu_tanh(x):
    c = math.sqrt(2.0 / math.pi)
    return x * (0.5 * (1.0 + jnp.tanh(c * (x + 0.044715 * (x * x * x)))))


def _t5_bucket(rel):
    n = jnp.maximum(rel, 0)
    max_exact = NUM_BUCKETS // 2
    nf = jnp.maximum(n, 1).astype(F32)
    large = max_exact + (jnp.log(nf / max_exact) / math.log(MAX_DISTANCE / max_exact)
                         * (NUM_BUCKETS - max_exact)).astype(jnp.int32)
    large = jnp.minimum(large, NUM_BUCKETS - 1)
    return jnp.where(n < max_exact, n, large)


def _mod_kernel(c_ref, w_ref, b_ref, o_ref):
    c = c_ref[...]
    s = (c * _sigmoid(c)).astype(BF16)
    o_ref[...] = jnp.dot(s, w_ref[...].astype(BF16), preferred_element_type=F32) + b_ref[...]


def _modulation(c_all, w_ada, b_ada):
    rows = c_all.shape[0]
    n_out = w_ada.shape[1]
    tn = 1024
    return pl.pallas_call(
        _mod_kernel,
        out_shape=jax.ShapeDtypeStruct((rows, n_out), F32),
        grid=(n_out // tn,),
        in_specs=[pl.BlockSpec((rows, D_MODEL), lambda j: (0, 0)),
                  pl.BlockSpec((D_MODEL, tn), lambda j: (0, j)),
                  pl.BlockSpec((1, tn), lambda j: (0, j))],
        out_specs=pl.BlockSpec((rows, tn), lambda j: (0, j)),
        compiler_params=_cparams(1),
        name="mod",
    )(c_all, w_ada, b_ada.reshape(1, n_out))


_SEG_K, _SEG_Q, _SEG_V, _SEG_U, _SEG_VG, _SEG_GA0, _SEG_GA1, _SEG_GB0, _SEG_GB1 = range(9)


def _inproj_kernel(x_ref, sh_ref, sc_ref, g1_ref, w_ref, qg_ref, kg_ref, lng_ref, lnb_ref,
                   *rest, tm, prompt):
    if prompt:
        qT32, k16, vT16, k_out, v_out, u16, vn16, ga16, gb16, kbar, h_scr = rest
        q32 = k32 = v32 = None
    else:
        q32, k32, v32, k_out, v_out, u16, vn16, vn32, ga16, gb16, h_scr = rest
        qT32 = k16 = vT16 = None
    j = pl.program_id(1)

    def normed_input():
        x = x_ref[...]
        ms = jnp.mean(x * x, axis=-1, keepdims=True)
        y = x * lax.rsqrt(ms + EPS) * g1_ref[...]
        h = y * (1.0 + sc_ref[...]) + sh_ref[...]
        return h.reshape(tm, D_MODEL).astype(BF16)

    def matmul(h=None):
        lhs = h_scr[...] if h is None else h
        return jnp.dot(lhs, w_ref[...], preferred_element_type=F32)

    def store_heads(z, y_of_head, dst32, dst16, dst_out, with_means, dst_t=None):
        for h in range(N_HEADS):
            sl = slice(h * HEAD_DIM, (h + 1) * HEAD_DIM)
            yh = y_of_head(z[:, sl])
            if dst32 is not None:
                dst32[:, sl] = yh
            if dst16 is not None:
                dst16[:, sl] = yh.astype(BF16)
            if dst_t is not None:
                dst_t[sl, :] = yh.T.astype(dst_t.dtype)
            if dst_out is not None:
                dst_out[pl.ds(h, tm, stride=N_HEADS), :] = yh
            if with_means:
                for r in range(tm // MOBA_BLOCK):
                    blk = yh[r * MOBA_BLOCK:(r + 1) * MOBA_BLOCK]
                    kbar[r:r + 1, sl] = jnp.mean(blk, axis=0, keepdims=True)

    def head_norm(g_ref):
        def f(zh):
            ms = jnp.mean(zh * zh, axis=-1, keepdims=True)
            return zh * lax.rsqrt(ms + EPS) * g_ref[...]
        return f

    @pl.when(j == _SEG_K)
    def _():
        h = normed_input()
        h_scr[...] = h
        if prompt:
            kbar[...] = jnp.zeros_like(kbar)
        store_heads(matmul(h), head_norm(kg_ref), k32, k16, k_out, prompt)

    @pl.when(j == _SEG_Q)
    def _():
        store_heads(matmul(), head_norm(qg_ref), q32, None, None, False, dst_t=qT32)

    @pl.when(j == _SEG_V)
    def _():
        store_heads(matmul(), lambda zh: zh, v32, None, v_out, False, dst_t=vT16)

    @pl.when(j == _SEG_U)
    def _():
        u16[...] = _gelu_tanh(matmul()).astype(BF16)

    @pl.when(j == _SEG_VG)
    def _():
        a = _gelu_tanh(matmul())
        mu = jnp.mean(a, axis=-1, keepdims=True)
        ac = a - mu
        y = ac * lax.rsqrt(jnp.mean(ac * ac, axis=-1, keepdims=True) + EPS)
        y = y * lng_ref[...] + lnb_ref[...]
        vn16[...] = y.astype(BF16)
        if not prompt:
            vn32[...] = y

    @pl.when((j == _SEG_GA0) | (j == _SEG_GA1))
    def _():
        ga16[...] = _sigmoid(matmul()).astype(BF16)

    @pl.when((j == _SEG_GB0) | (j == _SEG_GB1))
    def _():
        gb16[...] = _sigmoid(matmul()).astype(BF16)


def _mod_spec(g_rows, chunk, prompt, n_grid):
    if n_grid == 2:
        if prompt:
            return pl.BlockSpec((1, 1, D_MODEL), lambda i, j: (_PROMPT_MOD_ROW, 0, chunk))
        return pl.BlockSpec((g_rows, 1, D_MODEL), lambda i, j: (i, 0, chunk))
    if prompt:
        return pl.BlockSpec((1, 1, D_MODEL), lambda i: (_PROMPT_MOD_ROW, 0, chunk))
    return pl.BlockSpec((g_rows, 1, D_MODEL), lambda i: (i, 0, chunk))


_PROMPT_MOD_ROW = 128


def _w_in_col(j):
    return jnp.where(j < 2, 1 - j, j)


def _inproj(x3, mod3, g1, w_in16, qg, kg, lng, lnb, *, prompt, tm):
    n_rows = x3.shape[0] * V7X_SUBLANES
    g_rows = tm // V7X_SUBLANES
    grid = (n_rows // tm, 9)
    row_blk = lambda i, j: (i, 0)
    f32_seg = jax.ShapeDtypeStruct((n_rows, SEG), F32)
    b16_seg = jax.ShapeDtypeStruct((n_rows, SEG), BF16)
    b16_gate = jax.ShapeDtypeStruct((n_rows, D_MODEL), BF16)
    seg_spec = pl.BlockSpec((tm, SEG), row_blk)
    kv_out = jax.ShapeDtypeStruct((n_rows * N_HEADS, HEAD_DIM), F32)
    kv_spec = pl.BlockSpec((tm * N_HEADS, HEAD_DIM), row_blk)
    ga_spec = pl.BlockSpec((tm, SEG), lambda i, j: (i, jnp.clip(j - _SEG_GA0, 0, 1)))
    gb_spec = pl.BlockSpec((tm, SEG), lambda i, j: (i, jnp.clip(j - _SEG_GB0, 0, 1)))
    if prompt:
        assert tm % MOBA_BLOCK == 0 and tm // MOBA_BLOCK <= V7X_SUBLANES
        t_spec = pl.BlockSpec((SEG, tm), lambda i, j: (0, i))
        out_shape = [jax.ShapeDtypeStruct((SEG, n_rows), F32), b16_seg, jax.ShapeDtypeStruct((SEG, n_rows), BF16),
                     kv_out, kv_out, b16_seg, b16_seg, b16_gate, b16_gate,
                     jax.ShapeDtypeStruct((grid[0] * V7X_SUBLANES, SEG), F32)]
        out_specs = ([t_spec, seg_spec, t_spec] + [kv_spec] * 2 + [seg_spec] * 2
                     + [ga_spec, gb_spec, pl.BlockSpec((V7X_SUBLANES, SEG), row_blk)])
    else:
        out_shape = [f32_seg, f32_seg, f32_seg, kv_out, kv_out, b16_seg, b16_seg, f32_seg, b16_gate, b16_gate]
        out_specs = [seg_spec] * 3 + [kv_spec] * 2 + [seg_spec] * 3 + [ga_spec, gb_spec]
    vec = lambda n: pl.BlockSpec((1, n), lambda i, j: (0, 0))
    return pl.pallas_call(
        functools.partial(_inproj_kernel, tm=tm, prompt=prompt),
        out_shape=out_shape,
        grid=grid,
        in_specs=[pl.BlockSpec((g_rows, V7X_SUBLANES, D_MODEL), lambda i, j: (i, 0, 0)),
                  _mod_spec(g_rows, 0, prompt, 2),
                  _mod_spec(g_rows, 1, prompt, 2),
                  pl.BlockSpec((1, 1, D_MODEL), lambda i, j: (0, 0, 0)),
                  pl.BlockSpec((D_MODEL, SEG), lambda i, j: (0, _w_in_col(j))),
                  vec(HEAD_DIM), vec(HEAD_DIM), vec(GM_WIDTH), vec(GM_WIDTH)],
        out_specs=out_specs,
        scratch_shapes=[pltpu.VMEM((tm, D_MODEL), BF16)],
        compiler_params=_cparams(2),
        name="inproj_prompt" if prompt else "inproj_sample",
    )(x3, mod3, mod3, g1.reshape(1, 1, D_MODEL), w_in16, qg.reshape(1, HEAD_DIM), kg.reshape(1, HEAD_DIM),
      lng.reshape(1, GM_WIDTH), lnb.reshape(1, GM_WIDTH))


def _select_topk(scores, n_valid, axis):
    idx = lax.broadcasted_iota(jnp.int32, scores.shape, axis)
    idx_f = idx.astype(F32)
    cand = idx < n_valid
    sel = jnp.zeros(scores.shape, jnp.bool_)
    for _ in range(MOBA_TOPK):
        s_m = jnp.where(cand, scores, -jnp.inf)
        mx = jnp.max(s_m, axis=axis, keepdims=True)
        first = jnp.min(jnp.where(cand & (s_m == mx), idx_f, 1e9), axis=axis, keepdims=True)
        pick = idx_f == first
        sel = sel | pick
        cand = cand & jnp.logical_not(pick)
    return sel


_FAR_GROUP = 4
_SUM_ROWS = 16
_HEADS_PER_STEP = 8


def _attn_prompt_kernel(rb_ref, qT_ref, k_ref, vT_ref, kbar_ref, e_ref, bko_ref, bkp_ref, o_ref,
                        town, tprev, m_scr, r_scr):
    hp = pl.program_id(0)
    j = pl.program_id(1)
    tq = MOBA_BLOCK
    n_sel_rows = kbar_ref.shape[0]
    heads = range(_HEADS_PER_STEP)
    cols = [slice(i * HEAD_DIM, (i + 1) * HEAD_DIM) for i in heads]

    @pl.when(j == 0)
    def _():
        bo = bko_ref[...]
        bp = bkp_ref[...]
        key = lax.broadcasted_iota(jnp.int32, (tq, tq), 0)
        qry = lax.broadcasted_iota(jnp.int32, (tq, tq), 1)
        for i in heads:
            h = hp * _HEADS_PER_STEP + i
            b_far = rb_ref[h, NUM_BUCKETS - 1]
            to = jnp.zeros((tq, tq), F32)
            tp = jnp.zeros((tq, tq), F32)
            for b in range(NUM_BUCKETS - 1):
                val = (rb_ref[h, b] - b_far) * LOG2E
                to = jnp.where(bo == b, val, to)
                tp = jnp.where(bp == b, val, tp)
            town[i] = jnp.where(key <= qry, to, NEG)
            tprev[i] = tp

    q_sT, q_catT, prev_mask = [], [], []
    for i in heads:
        qT = qT_ref[cols[i], :]
        scoresT = jnp.dot(kbar_ref[:, cols[i]], qT, precision=lax.Precision.HIGHEST,
                          preferred_element_type=F32)
        selT = _select_topk(scoresT, j, axis=0)
        blk = lax.broadcasted_iota(jnp.int32, selT.shape, 0)
        qs = (qT * (SCALE * LOG2E)).astype(BF16)
        aug = jnp.where(selT & (blk < j - 1), 0.0, NEG)
        aug = jnp.concatenate([aug, jnp.full((HEAD_DIM - n_sel_rows, tq), NEG, F32)], axis=0)
        q_sT.append(qs)
        q_catT.append(jnp.concatenate([qs, aug.astype(BF16)], axis=0))
        prev_sel = jnp.max(jnp.where(selT & (blk == j - 1), 1.0, 0.0), axis=0, keepdims=True) > 0.0
        prev_mask.append(jnp.where(prev_sel, 0.0, NEG))

    ones_rows = jnp.ones((_SUM_ROWS, tq), BF16)

    def partial_softmax(logits, where):
        ms = [jnp.max(s, axis=0, keepdims=True) for s in logits]
        ps = [jnp.exp2(s - m).astype(BF16) for s, m in zip(logits, ms)]
        rs = []
        for p, (i, start) in zip(ps, where):
            vT_aug = jnp.concatenate([vT_ref[cols[i], pl.ds(start, tq)], ones_rows], axis=0)
            rs.append(jnp.dot(vT_aug, p, preferred_element_type=F32))
        return list(zip(ms, rs))

    def combine(m_run, r_run, parts):
        m_new = m_run
        for m, _ in parts:
            m_new = jnp.maximum(m_new, m)
        r_new = r_run * jnp.exp2(m_run - m_new)
        for m, r in parts:
            r_new = r_new + r * jnp.exp2(m - m_new)
        return m_new, r_new

    own_start = pl.multiple_of(j * tq, tq)
    prev_start = pl.multiple_of(jnp.maximum(j - 1, 0) * tq, tq)
    logits, where = [], []
    for i in heads:
        logits.append(jnp.dot(k_ref[pl.ds(own_start, tq), cols[i]], q_sT[i], preferred_element_type=F32)
                      + town[i])
        logits.append(jnp.dot(k_ref[pl.ds(prev_start, tq), cols[i]], q_sT[i], preferred_element_type=F32)
                      + (tprev[i] + prev_mask[i]))
        where += [(i, own_start), (i, prev_start)]
    parts = partial_softmax(logits, where)
    for i in heads:
        (m_own, r_own), part_prev = parts[2 * i], parts[2 * i + 1]
        m_scr[i], r_scr[i] = combine(m_own, r_own, [part_prev])

    def far_body(g, carry):
        starts = [pl.multiple_of((g * _FAR_GROUP + b) * tq, tq) for b in range(_FAR_GROUP)]
        logits, where = [], []
        for i in heads:
            for start in starts:
                k_cat = jnp.concatenate([k_ref[pl.ds(start, tq), cols[i]], e_ref[pl.ds(start, tq), :]], axis=1)
                logits.append(jnp.dot(k_cat, q_catT[i], preferred_element_type=F32))
                where.append((i, start))
        parts = partial_softmax(logits, where)
        for i in heads:
            m_scr[i], r_scr[i] = combine(m_scr[i], r_scr[i], parts[i * _FAR_GROUP:(i + 1) * _FAR_GROUP])
        return carry

    lax.fori_loop(0, (j + _FAR_GROUP - 2) // _FAR_GROUP, far_body, 0)

    for i in heads:
        r = r_scr[i]
        o_ref[:, cols[i]] = (r[:HEAD_DIM] / r[HEAD_DIM:HEAD_DIM + 1]).T.astype(o_ref.dtype)


def _attn_prompt(qT32, k16, vT16, kbar, rel_bias):
    seq = k16.shape[0]
    tq = MOBA_BLOCK
    n_blocks = seq // tq
    assert n_blocks % _FAR_GROUP == 0 and kbar.shape[0] == n_blocks
    pos = jnp.arange(seq, dtype=jnp.int32)
    onehot = (pos[:, None] // tq == jnp.arange(HEAD_DIM, dtype=jnp.int32)[None, :]).astype(BF16)
    r = jnp.arange(tq, dtype=jnp.int32)
    bucket_own = _t5_bucket(r[None, :] - r[:, None])
    bucket_prev = _t5_bucket(tq + r[None, :] - r[:, None])
    const = lambda h, j: (0, 0)
    hps = _HEADS_PER_STEP
    width = hps * HEAD_DIM
    return pl.pallas_call(
        _attn_prompt_kernel,
        out_shape=jax.ShapeDtypeStruct((seq, ATT_WIDTH), BF16),
        grid=(N_HEADS // hps, n_blocks),
        in_specs=[pl.BlockSpec(memory_space=pltpu.SMEM),
                  pl.BlockSpec((width, tq), lambda h, j: (h, j)),
                  pl.BlockSpec((seq, width), lambda h, j: (0, h)),
                  pl.BlockSpec((width, seq), lambda h, j: (h, 0)),
                  pl.BlockSpec((n_blocks, width), lambda h, j: (0, h)),
                  pl.BlockSpec((seq, HEAD_DIM), const),
                  pl.BlockSpec((tq, tq), const),
                  pl.BlockSpec((tq, tq), const)],
        out_specs=pl.BlockSpec((tq, width), lambda h, j: (j, h)),
        scratch_shapes=[pltpu.VMEM((hps, tq, tq), F32), pltpu.VMEM((hps, tq, tq), F32),
                        pltpu.VMEM((hps, 1, tq), F32), pltpu.VMEM((hps, HEAD_DIM + _SUM_ROWS, tq), F32)],
        compiler_params=_cparams(2),
        name="attn_prompt",
    )(rel_bias.T, qT32, k16, vT16, kbar, onehot, bucket_own, bucket_prev)


_SAMPLE_GROUP = 2


def _attn_sample_kernel(pt_ref, q_ref, kn_ref, vn_ref, rbrows_ref, bkp_ref, *rest, n_pages, dec_seq):
    k_pages = rest[:n_pages]
    v_pages = rest[n_pages:2 * n_pages]
    o_ref, k16, v16, kbar_scr, bprev_scr, bown_scr = rest[2 * n_pages:]
    b = pl.program_id(0)
    past = n_pages * PAGE_SIZE
    n_past_blocks = past // MOBA_BLOCK
    rows = N_HEADS * dec_seq
    pages_per_block = MOBA_BLOCK // PAGE_SIZE

    @pl.when(b == 0)
    def _():
        kbar_scr[...] = jnp.zeros_like(kbar_scr)
        rb = rbrows_ref[...]
        bp = bkp_ref[...]
        tp = jnp.zeros((rows, MOBA_BLOCK), F32)
        for bb in range(NUM_BUCKETS - 1):
            tp = jnp.where(bp == bb, rb[:, bb:bb + 1], tp)
        bprev_scr[...] = tp
        t_row = lax.broadcasted_iota(jnp.int32, (rows, V7X_LANES), 0) & (dec_seq - 1)
        c_col = lax.broadcasted_iota(jnp.int32, (rows, V7X_LANES), 1)
        rel = t_row - c_col
        to = jnp.zeros((rows, V7X_LANES), F32)
        for bb in range(dec_seq):
            to = jnp.where(rel == bb, rb[:, bb:bb + 1], to)
        bown_scr[...] = jnp.where(rel >= 0, to, NEG)

    q = q_ref[...]
    q_rep = jnp.concatenate([q] * N_HEADS, axis=0)
    row_head = lax.broadcasted_iota(jnp.int32, (rows, ATT_WIDTH), 0) >> int(math.log2(dec_seq))
    col_head = lax.broadcasted_iota(jnp.int32, (rows, ATT_WIDTH), 1) >> int(math.log2(HEAD_DIM))
    q_bd = jnp.where(row_head == col_head, q_rep, 0.0)
    q_s = (q_bd * SCALE).astype(BF16)

    def to_head_lanes(pages, dst, n, sums_ref):
        for h in range(N_HEADS):
            cols = slice(h * HEAD_DIM, (h + 1) * HEAD_DIM)
            total = None
            for p in range(n * pages_per_block, (n + 1) * pages_per_block):
                xh = pages[p][pl.ds(h, PAGE_SIZE, stride=N_HEADS), :]
                dst[p * PAGE_SIZE:(p + 1) * PAGE_SIZE, cols] = xh.astype(BF16)
                if sums_ref is not None:
                    cs = jnp.sum(xh, axis=0, keepdims=True)
                    total = cs if total is None else total + cs
            if sums_ref is not None:
                sums_ref[n:n + 1, cols] = total * (1.0 / MOBA_BLOCK)

    def new_rows(ref):
        pad = jnp.zeros((V7X_LANES - dec_seq, ATT_WIDTH), F32)
        return jnp.concatenate([ref[...], pad], axis=0).astype(BF16)

    groups = [slice(g * MOBA_BLOCK, (g + _SAMPLE_GROUP) * MOBA_BLOCK)
              for g in range(0, n_past_blocks, _SAMPLE_GROUP)]
    new_keys = slice(past, past + V7X_LANES)
    raw = []
    for keys in groups:
        for n in range(keys.start // MOBA_BLOCK, keys.stop // MOBA_BLOCK):
            to_head_lanes(k_pages, k16, n, kbar_scr)
        raw.append(lax.dot_general(q_s, k16[keys, :], NT_DIMS, preferred_element_type=F32))
    k16[new_keys, :] = new_rows(kn_ref)
    raw.append(lax.dot_general(q_s, k16[new_keys, :], NT_DIMS, preferred_element_type=F32))

    assert n_past_blocks == V7X_SUBLANES
    q_pad = jnp.concatenate([q_bd, jnp.zeros((V7X_LANES - rows, ATT_WIDTH), F32)], axis=0)
    scores_t = lax.dot_general(kbar_scr[0:n_past_blocks, :], q_pad, NT_DIMS, precision=lax.Precision.HIGHEST,
                               preferred_element_type=F32)
    blk = lax.broadcasted_iota(jnp.int32, scores_t.shape, 0)
    rank = jnp.zeros(scores_t.shape, F32)
    for k in range(1, n_past_blocks):
        other = pltpu.roll(scores_t, k, 0)
        ge = jnp.where(other >= scores_t, 1.0, 0.0)
        gt = jnp.where(other > scores_t, 1.0, 0.0)
        rank = rank + jnp.where(blk >= k, ge, gt)
    sel_t = jnp.where(rank < MOBA_TOPK, 1.0, 0.0)
    eye = jnp.where(lax.broadcasted_iota(jnp.int32, (rows, V7X_LANES), 0)
                    == lax.broadcasted_iota(jnp.int32, (rows, V7X_LANES), 1), 1.0, 0.0).astype(BF16)
    sel_pad = jnp.concatenate([sel_t, jnp.zeros((V7X_LANES - n_past_blocks, V7X_LANES), F32)], axis=0)
    sel = lax.dot_general(eye, sel_pad.astype(BF16), NT_DIMS, preferred_element_type=F32)
    mask_add = jnp.where(sel > 0.5, 0.0, NEG)

    pieces = []
    for keys, piece in zip(groups, raw):
        cols = []
        for i, n in enumerate(range(keys.start // MOBA_BLOCK, keys.stop // MOBA_BLOCK)):
            blk = piece[:, i * MOBA_BLOCK:(i + 1) * MOBA_BLOCK] + mask_add[:, n:n + 1]
            if n == n_past_blocks - 1:
                blk = blk + bprev_scr[...]
            cols.append(blk)
        pieces.append(jnp.concatenate(cols, axis=1))
    pieces.append(raw[-1] + bown_scr[...])
    m = pieces[0].max(axis=-1, keepdims=True)
    for piece in pieces[1:]:
        m = jnp.maximum(m, piece.max(axis=-1, keepdims=True))
    probs = [jnp.exp(piece - m) for piece in pieces]
    l = sum(p.sum(axis=-1, keepdims=True) for p in probs)

    out = None
    for keys, p in zip(groups + [new_keys], probs):
        if keys is new_keys:
            v16[keys, :] = new_rows(vn_ref)
        else:
            for n in range(keys.start // MOBA_BLOCK, keys.stop // MOBA_BLOCK):
                to_head_lanes(v_pages, v16, n, None)
        part = jnp.dot(p.astype(BF16), v16[keys, :], preferred_element_type=F32)
        out = part if out is None else out + part
    out = out / l
    for h in range(N_HEADS):
        sl = slice(h * HEAD_DIM, (h + 1) * HEAD_DIM)
        o_ref[:, sl] = out[h * dec_seq:(h + 1) * dec_seq, sl]


def _attn_sample(q3, k3, v3, cache_k, cache_v, page_table, rel_bias):
    n_seq, dec_seq, _ = q3.shape
    n_pages = page_table.shape[1]
    past = n_pages * PAGE_SIZE
    rows = N_HEADS * dec_seq
    page_rows = PAGE_SIZE * N_HEADS
    ck = cache_k.reshape(-1, HEAD_DIM)
    cv = cache_v.reshape(-1, HEAD_DIM)
    rb_rel = rel_bias.T - rel_bias[NUM_BUCKETS - 1][:, None]
    rbrows = jnp.repeat(rb_rel, dec_seq, axis=0)
    t = jnp.arange(rows, dtype=jnp.int32) % dec_seq
    c = jnp.arange(MOBA_BLOCK, dtype=jnp.int32)
    bucket_prev = _t5_bucket(MOBA_BLOCK + t[:, None] - c[None, :])
    tok = pl.BlockSpec((None, dec_seq, ATT_WIDTH), lambda b, pt: (b, 0, 0))
    const2 = lambda shape: pl.BlockSpec(shape, lambda b, pt: (0, 0))

    def page_spec(p):
        return pl.BlockSpec((page_rows, HEAD_DIM), lambda b, pt: (pt[b, p], 0))

    grid_spec = pltpu.PrefetchScalarGridSpec(
        num_scalar_prefetch=1,
        grid=(n_seq,),
        in_specs=[tok, tok, tok, const2((rows, NUM_BUCKETS)), const2((rows, MOBA_BLOCK))]
                 + [page_spec(p) for p in range(n_pages)] * 2,
        out_specs=tok,
        scratch_shapes=[pltpu.VMEM((past + V7X_LANES, ATT_WIDTH), BF16),
                        pltpu.VMEM((past + V7X_LANES, ATT_WIDTH), BF16),
                        pltpu.VMEM((V7X_LANES, ATT_WIDTH), F32),
                        pltpu.VMEM((rows, MOBA_BLOCK), F32),
                        pltpu.VMEM((rows, V7X_LANES), F32)],
    )
    return pl.pallas_call(
        functools.partial(_attn_sample_kernel, n_pages=n_pages, dec_seq=dec_seq),
        out_shape=jax.ShapeDtypeStruct((n_seq, dec_seq, ATT_WIDTH), F32),
        grid_spec=grid_spec,
        compiler_params=_cparams(1),
        name="attn_sample",
    )(page_table, q3, k3, v3, rbrows, bucket_prev, *([ck] * n_pages), *([cv] * n_pages))


def _merge_kernel(x_ref, gt_ref, u_ref, vn_ref, ga_ref, gb_ref, att_ref, wsp_ref, bsp_ref,
                  wpa_ref, wpb_ref, wo_ref, o_ref, gm_scr, *, tm):
    for c in range(tm // GM_CHUNK):
        rows = slice(c * GM_CHUNK, (c + 1) * GM_CHUNK)
        for g in range(GM_GROUPS):
            cols = slice(g * GM_CHUNK, (g + 1) * GM_CHUNK)
            s = jnp.dot(wsp_ref[g], vn_ref[rows, cols], preferred_element_type=F32) + bsp_ref[:, cols]
            gm_scr[rows, cols] = (u_ref[rows, cols].astype(F32) * s).astype(BF16)
    a = jnp.dot(gm_scr[...], wpa_ref[...], preferred_element_type=F32)
    b = jnp.dot(att_ref[...].astype(BF16), wpb_ref[...], preferred_element_type=F32)
    merged = ga_ref[...].astype(F32) * a + gb_ref[...].astype(F32) * b
    c_out = jnp.dot(merged.astype(BF16), wo_ref[...], preferred_element_type=F32)
    o_ref[...] = x_ref[...] + gt_ref[...] * c_out.reshape(o_ref.shape)


def _merge(x3, mod3, u16, vn16, ga16, gb16, att, w_sp16, b_sp, w_pa16, w_pb16, w_o16, *, prompt, tm):
    n_rows = x3.shape[0] * V7X_SUBLANES
    g_rows = tm // V7X_SUBLANES
    row3 = pl.BlockSpec((g_rows, V7X_SUBLANES, D_MODEL), lambda i: (i, 0, 0))
    seg = pl.BlockSpec((tm, SEG), lambda i: (i, 0))
    gate = pl.BlockSpec((tm, D_MODEL), lambda i: (i, 0))
    resident = lambda shape: pl.BlockSpec(shape, lambda i: (0,) * len(shape), pipeline_mode=pl.Buffered(1))
    return pl.pallas_call(
        functools.partial(_merge_kernel, tm=tm),
        out_shape=jax.ShapeDtypeStruct(x3.shape, F32),
        grid=(n_rows // tm,),
        in_specs=[row3, _mod_spec(g_rows, 2, prompt, 1), seg, seg, gate, gate, seg,
                  resident((GM_GROUPS, GM_CHUNK, GM_CHUNK)), resident((GM_CHUNK, GM_WIDTH)),
                  resident((GM_WIDTH, D_MODEL)), resident((ATT_WIDTH, D_MODEL)), resident((D_MODEL, D_MODEL))],
        out_specs=row3,
        scratch_shapes=[pltpu.VMEM((tm, GM_WIDTH), BF16)],
        compiler_params=_cparams(1),
        name="merge_prompt" if prompt else "merge_sample",
    )(x3, mod3, u16, vn16, ga16, gb16, att, w_sp16, b_sp, w_pa16, w_pb16, w_o16)


def _ffn_kernel(x_ref, sh_ref, sc_ref, gt_ref, g2_ref, wg_ref, wu_ref, wd_ref, o_ref, h_scr, *, tm):
    j = pl.program_id(1)
    last = pl.num_programs(1) - 1
    acc_ref = o_ref

    def chunk(h):
        g = jnp.dot(h, wg_ref[...], preferred_element_type=F32)
        u = jnp.dot(h, wu_ref[...], preferred_element_type=F32)
        a = (g * _sigmoid(g) * u).astype(BF16)
        return jnp.dot(a, wd_ref[...], preferred_element_type=F32)

    @pl.when(j == 0)
    def _():
        x = x_ref[...]
        ms = jnp.mean(x * x, axis=-1, keepdims=True)
        y = x * lax.rsqrt(ms + EPS) * g2_ref[...]
        h = (y * (1.0 + sc_ref[...]) + sh_ref[...]).reshape(tm, D_MODEL).astype(BF16)
        h_scr[...] = h
        acc_ref[...] = chunk(h).reshape(o_ref.shape)

    @pl.when((j > 0) & (j < last))
    def _():
        acc_ref[...] += chunk(h_scr[...]).reshape(o_ref.shape)

    @pl.when(j == last)
    def _():
        total = acc_ref[...] + chunk(h_scr[...]).reshape(o_ref.shape)
        o_ref[...] = x_ref[...] + gt_ref[...] * total


def _ffn(x3, mod3, g2, w_gu16, w_down16, *, prompt, tm, tf):
    n_rows = x3.shape[0] * V7X_SUBLANES
    g_rows = tm // V7X_SUBLANES
    n_chunks = D_FF // tf
    row3 = pl.BlockSpec((g_rows, V7X_SUBLANES, D_MODEL), lambda i, j: (i, 0, 0))
    return pl.pallas_call(
        functools.partial(_ffn_kernel, tm=tm),
        out_shape=jax.ShapeDtypeStruct(x3.shape, F32),
        grid=(n_rows // tm, n_chunks),
        in_specs=[row3, _mod_spec(g_rows, 3, prompt, 2), _mod_spec(g_rows, 4, prompt, 2),
                  _mod_spec(g_rows, 5, prompt, 2),
                  pl.BlockSpec((1, 1, D_MODEL), lambda i, j: (0, 0, 0)),
                  pl.BlockSpec((D_MODEL, tf), lambda i, j: (0, j)),
                  pl.BlockSpec((D_MODEL, tf), lambda i, j: (0, n_chunks + j)),
                  pl.BlockSpec((tf, D_MODEL), lambda i, j: (j, 0))],
        out_specs=row3,
        scratch_shapes=[pltpu.VMEM((tm, D_MODEL), BF16)],
        compiler_params=_cparams(2),
        name="ffn_prompt" if prompt else "ffn_sample",
    )(x3, mod3, mod3, mod3, g2.reshape(1, 1, D_MODEL), w_gu16, w_gu16, w_down16)


def _gm_spatial_weights(gm_ws, gm_bs, dec_seq):
    tril = jnp.tril(jnp.ones((GM_CHUNK, GM_CHUNK), dtype=bool))
    w = jnp.where(tril[None], gm_ws, jnp.zeros_like(gm_ws))
    w_prompt = w.astype(BF16)
    b_prompt = jnp.repeat(gm_bs.T, GM_WIDTH // GM_GROUPS, axis=1)
    reps = GM_CHUNK // dec_seq
    eye = jnp.eye(reps, dtype=F32)
    w_small = w[:, :dec_seq, :dec_seq]
    w_sample = jnp.einsum('ab,gts->gatbs', eye, w_small).reshape(GM_GROUPS, GM_CHUNK, GM_CHUNK).astype(BF16)
    b_sample = jnp.tile(jnp.repeat(gm_bs[:, :dec_seq].T, GM_WIDTH // GM_GROUPS, axis=1), (reps, 1))
    return w_prompt, b_prompt, w_sample, b_sample


def kernel(x_prompt, x_sample, c_prompt, c_sample, cache_k, cache_v, page_table, w_ada, b_ada, norm1_g, w_in,
           q_norm_g, k_norm_g, gm_ln_g, gm_ln_b, gm_ws, gm_bs, w_pa, w_pb, w_o, norm2_g, w_gu, w_down,
           rel_bias):
    depth = w_in.shape[0]
    assert depth == 1, "single-layer step"
    n_seq, dec_seq, _ = x_sample.shape
    batch, seq, _ = x_prompt.shape
    assert batch == 1 and n_seq == _PROMPT_MOD_ROW and dec_seq == V7X_SUBLANES

    pad_rows = V7X_SUBLANES - 1
    c_all = jnp.concatenate([c_sample, c_prompt, jnp.zeros((pad_rows, D_MODEL), F32)], axis=0)
    mod = _modulation(c_all, w_ada[0], b_ada[0])
    mod3 = mod.reshape(mod.shape[0], 1, 6 * D_MODEL)

    w_in16 = w_in[0].astype(BF16)
    w_pa16 = w_pa[0].astype(BF16)
    w_pb16 = w_pb[0].astype(BF16)
    w_o16 = w_o[0].astype(BF16)
    w_gu16 = w_gu[0].astype(BF16)
    w_down16 = w_down[0].astype(BF16)
    wsp_p, bsp_p, wsp_s, bsp_s = _gm_spatial_weights(gm_ws[0], gm_bs[0], dec_seq)

    xp3 = x_prompt.reshape(seq // V7X_SUBLANES, V7X_SUBLANES, D_MODEL)
    xs3 = x_sample

    proj = functools.partial(_inproj, g1=norm1_g[0], w_in16=w_in16, qg=q_norm_g[0], kg=k_norm_g[0],
                             lng=gm_ln_g[0], lnb=gm_ln_b[0])
    tm_prompt = 512
    qT32, k16, vT16, k_out, v_out, u16, vn16, ga16, gb16, kbar_tiles = proj(xp3, mod3, prompt=True, tm=tm_prompt)
    per_tile = tm_prompt // MOBA_BLOCK
    kbar = kbar_tiles.reshape(-1, V7X_SUBLANES, ATT_WIDTH)[:, :per_tile].reshape(-1, ATT_WIDTH)
    sq32, sk32, sv32, sk_out, sv_out, su16, svn16, svn32, sga16, sgb16 = proj(xs3, mod3, prompt=False, tm=256)

    att_p = _attn_prompt(qT32, k16, vT16, kbar, rel_bias)
    tok3 = lambda a: a.reshape(n_seq, dec_seq, ATT_WIDTH)
    att_s = _attn_sample(tok3(sq32), tok3(sk32), tok3(sv32), cache_k[0], cache_v[0], page_table, rel_bias)
    att_s = att_s.reshape(n_seq * dec_seq, ATT_WIDTH)

    mrg = functools.partial(_merge, w_pa16=w_pa16, w_pb16=w_pb16, w_o16=w_o16, tm=256)
    x1p = mrg(xp3, mod3, u16, vn16, ga16, gb16, att_p, wsp_p, bsp_p, prompt=True)
    x1s = mrg(xs3, mod3, su16, svn16, sga16, sgb16, att_s, wsp_s, bsp_s, prompt=False)

    ffn = functools.partial(_ffn, g2=norm2_g[0], w_gu16=w_gu16, w_down16=w_down16, tf=512)
    y_p = ffn(x1p, mod3, prompt=True, tm=1024).reshape(batch, seq, D_MODEL)
    y_s = ffn(x1s, mod3, prompt=False, tm=1024)

    kv_p = lambda a: a.reshape(1, batch, seq, N_HEADS, HEAD_DIM)
    kv_s = lambda a: a.reshape(1, n_seq, dec_seq, N_HEADS, HEAD_DIM)
    return (y_p, y_s, kv_p(k_out), kv_p(v_out), kv_s(sk_out), kv_s(sv_out),
            svn32.reshape(1, n_seq, dec_seq, GM_WIDTH))
```

```python
import functools
import math

import jax
import jax.numpy as jnp
from jax import lax
from jax.experimental import pallas as pl
from jax.experimental.pallas import tpu as pltpu

F32 = jnp.float32
BF16 = jnp.bfloat16

D_MODEL = 2048
N_HEADS = 8
HEAD_DIM = 128
ATT_WIDTH = N_HEADS * HEAD_DIM
GM_WIDTH = 1024
GM_GROUPS = 8
GM_CHUNK = 128
MOBA_BLOCK = 256
MOBA_TOPK = 3
NUM_BUCKETS = 32
MAX_DISTANCE = 128
PAGE_SIZE = 128
D_FF = 5632
EPS = 1e-6
SCALE = HEAD_DIM ** -0.5
LOG2E = math.log2(math.e)

V7X_LANES = 128
V7X_SUBLANES = 8
V7X_VMEM_BYTES = 64 * 1024 * 1024
VMEM_LIMIT = V7X_VMEM_BYTES - 8 * 1024 * 1024

NEG = -1e30
SEG = 1024

NT_DIMS = (((1,), (1,)), ((), ()))


def _cparams(n_axes, vmem=VMEM_LIMIT):
    return pltpu.CompilerParams(dimension_semantics=("arbitrary",) * n_axes, vmem_limit_bytes=vmem)


def _sigmoid(x):
    return 1.0 / (1.0 + jnp.exp(-x))


def _gelu_tanh(x):
    c = math.sqrt(2.0 / math.pi)
    return x * (0.5 * (1.0 + jnp.tanh(c * (x + 0.044715 * (x * x * x)))))


def _t5_bucket(rel):
    n = jnp.maximum(rel, 0)
    max_exact = NUM_BUCKETS // 2
    nf = jnp.maximum(n, 1).astype(F32)
    large = max_exact + (jnp.log(nf / max_exact) / math.log(MAX_DISTANCE / max_exact)
                         * (NUM_BUCKETS - max_exact)).astype(jnp.int32)
    large = jnp.minimum(large, NUM_BUCKETS - 1)
    return jnp.where(n < max_exact, n, large)


def _mod_kernel(c_ref, w_ref, b_ref, o_ref):
    c = c_ref[...]
    s = (c * _sigmoid(c)).astype(BF16)
    o_ref[...] = jnp.dot(s, w_ref[...].astype(BF16), preferred_element_type=F32) + b_ref[...]


def _modulation(c_all, w_ada, b_ada):
    rows = c_all.shape[0]
    n_out = w_ada.shape[1]
    tn = 1024
    return pl.pallas_call(
        _mod_kernel,
        out_shape=jax.ShapeDtypeStruct((rows, n_out), F32),
        grid=(n_out // tn,),
        in_specs=[pl.BlockSpec((rows, D_MODEL), lambda j: (0, 0)),
                  pl.BlockSpec((D_MODEL, tn), lambda j: (0, j)),
                  pl.BlockSpec((1, tn), lambda j: (0, j))],
        out_specs=pl.BlockSpec((rows, tn), lambda j: (0, j)),
        compiler_params=_cparams(1),
        name="mod",
    )(c_all, w_ada, b_ada.reshape(1, n_out))


_SEG_K, _SEG_Q, _SEG_V, _SEG_U, _SEG_VG, _SEG_GA0, _SEG_GA1, _SEG_GB0, _SEG_GB1 = range(9)


def _inproj_kernel(x_ref, sh_ref, sc_ref, g1_ref, w_ref, qg_ref, kg_ref, lng_ref, lnb_ref,
                   *rest, tm, prompt):
    if prompt:
        qT32, k16, vT16, k_out, v_out, u16, vn16, ga16, gb16, kbar, h_scr = rest
        q32 = k32 = v32 = None
    else:
        q32, k32, v32, k_out, v_out, u16, vn16, vn32, ga16, gb16, h_scr = rest
        qT32 = k16 = vT16 = None
    j = pl.program_id(1)

    def normed_input():
        x = x_ref[...]
        ms = jnp.mean(x * x, axis=-1, keepdims=True)
        y = x * lax.rsqrt(ms + EPS) * g1_ref[...]
        h = y * (1.0 + sc_ref[...]) + sh_ref[...]
        return h.reshape(tm, D_MODEL).astype(BF16)

    def matmul(h=None):
        lhs = h_scr[...] if h is None else h
        return jnp.dot(lhs, w_ref[...], preferred_element_type=F32)

    def store_heads(z, y_of_head, dst32, dst16, dst_out, with_means, dst_t=None):
        for h in range(N_HEADS):
            sl = slice(h * HEAD_DIM, (h + 1) * HEAD_DIM)
            yh = y_of_head(z[:, sl])
            if dst32 is not None:
                dst32[:, sl] = yh
            if dst16 is not None:
                dst16[:, sl] = yh.astype(BF16)
            if dst_t is not None:
                dst_t[sl, :] = yh.T.astype(dst_t.dtype)
            if dst_out is not None:
                dst_out[pl.ds(h, tm, stride=N_HEADS), :] = yh
            if with_means:
                for r in range(tm // MOBA_BLOCK):
                    blk = yh[r * MOBA_BLOCK:(r + 1) * MOBA_BLOCK]
                    kbar[r:r + 1, sl] = jnp.mean(blk, axis=0, keepdims=True)

    def head_norm(g_ref):
        def f(zh):
            ms = jnp.mean(zh * zh, axis=-1, keepdims=True)
            return zh * lax.rsqrt(ms + EPS) * g_ref[...]
        return f

    @pl.when(j == _SEG_K)
    def _():
        h = normed_input()
        h_scr[...] = h
        if prompt:
            kbar[...] = jnp.zeros_like(kbar)
        store_heads(matmul(h), head_norm(kg_ref), k32, k16, k_out, prompt)

    @pl.when(j == _SEG_Q)
    def _():
        store_heads(matmul(), head_norm(qg_ref), q32, None, None, False, dst_t=qT32)

    @pl.when(j == _SEG_V)
    def _():
        store_heads(matmul(), lambda zh: zh, v32, None, v_out, False, dst_t=vT16)

    @pl.when(j == _SEG_U)
    def _():
        u16[...] = _gelu_tanh(matmul()).astype(BF16)

    @pl.when(j == _SEG_VG)
    def _():
        a = _gelu_tanh(matmul())
        mu = jnp.mean(a, axis=-1, keepdims=True)
        ac = a - mu
        y = ac * lax.rsqrt(jnp.mean(ac * ac, axis=-1, keepdims=True) + EPS)
        y = y * lng_ref[...] + lnb_ref[...]
        vn16[...] = y.astype(BF16)
        if not prompt:
            vn32[...] = y

    @pl.when((j == _SEG_GA0) | (j == _SEG_GA1))
    def _():
        ga16[...] = _sigmoid(matmul()).astype(BF16)

    @pl.when((j == _SEG_GB0) | (j == _SEG_GB1))
    def _():
        gb16[...] = _sigmoid(matmul()).astype(BF16)


def _mod_spec(g_rows, chunk, prompt, n_grid):
    if n_grid == 2:
        if prompt:
            return pl.BlockSpec((1, 1, D_MODEL), lambda i, j: (_PROMPT_MOD_ROW, 0, chunk))
        return pl.BlockSpec((g_rows, 1, D_MODEL), lambda i, j: (i, 0, chunk))
    if prompt:
        return pl.BlockSpec((1, 1, D_MODEL), lambda i: (_PROMPT_MOD_ROW, 0, chunk))
    return pl.BlockSpec((g_rows, 1, D_MODEL), lambda i: (i, 0, chunk))


_PROMPT_MOD_ROW = 128


def _w_in_col(j):
    return jnp.where(j < 2, 1 - j, j)


def _inproj(x3, mod3, g1, w_in16, qg, kg, lng, lnb, *, prompt, tm):
    n_rows = x3.shape[0] * V7X_SUBLANES
    g_rows = tm // V7X_SUBLANES
    grid = (n_rows // tm, 9)
    row_blk = lambda i, j: (i, 0)
    f32_seg = jax.ShapeDtypeStruct((n_rows, SEG), F32)
    b16_seg = jax.ShapeDtypeStruct((n_rows, SEG), BF16)
    b16_gate = jax.ShapeDtypeStruct((n_rows, D_MODEL), BF16)
    seg_spec = pl.BlockSpec((tm, SEG), row_blk)
    kv_out = jax.ShapeDtypeStruct((n_rows * N_HEADS, HEAD_DIM), F32)
    kv_spec = pl.BlockSpec((tm * N_HEADS, HEAD_DIM), row_blk)
    ga_spec = pl.BlockSpec((tm, SEG), lambda i, j: (i, jnp.clip(j - _SEG_GA0, 0, 1)))
    gb_spec = pl.BlockSpec((tm, SEG), lambda i, j: (i, jnp.clip(j - _SEG_GB0, 0, 1)))
    if prompt:
        assert tm % MOBA_BLOCK == 0 and tm // MOBA_BLOCK <= V7X_SUBLANES
        t_spec = pl.BlockSpec((SEG, tm), lambda i, j: (0, i))
        out_shape = [jax.ShapeDtypeStruct((SEG, n_rows), F32), b16_seg, jax.ShapeDtypeStruct((SEG, n_rows), BF16),
                     kv_out, kv_out, b16_seg, b16_seg, b16_gate, b16_gate,
                     jax.ShapeDtypeStruct((grid[0] * V7X_SUBLANES, SEG), F32)]
        out_specs = ([t_spec, seg_spec, t_spec] + [kv_spec] * 2 + [seg_spec] * 2
                     + [ga_spec, gb_spec, pl.BlockSpec((V7X_SUBLANES, SEG), row_blk)])
    else:
        out_shape = [f32_seg, f32_seg, f32_seg, kv_out, kv_out, b16_seg, b16_seg, f32_seg, b16_gate, b16_gate]
        out_specs = [seg_spec] * 3 + [kv_spec] * 2 + [seg_spec] * 3 + [ga_spec, gb_spec]
    vec = lambda n: pl.BlockSpec((1, n), lambda i, j: (0, 0))
    return pl.pallas_call(
        functools.partial(_inproj_kernel, tm=tm, prompt=prompt),
        out_shape=out_shape,
        grid=grid,
        in_specs=[pl.BlockSpec((g_rows, V7X_SUBLANES, D_MODEL), lambda i, j: (i, 0, 0)),
                  _mod_spec(g_rows, 0, prompt, 2),
                  _mod_spec(g_rows, 1, prompt, 2),
                  pl.BlockSpec((1, 1, D_MODEL), lambda i, j: (0, 0, 0)),
                  pl.BlockSpec((D_MODEL, SEG), lambda i, j: (0, _w_in_col(j))),
                  vec(HEAD_DIM), vec(HEAD_DIM), vec(GM_WIDTH), vec(GM_WIDTH)],
        out_specs=out_specs,
        scratch_shapes=[pltpu.VMEM((tm, D_MODEL), BF16)],
        compiler_params=_cparams(2),
        name="inproj_prompt" if prompt else "inproj_sample",
    )(x3, mod3, mod3, g1.reshape(1, 1, D_MODEL), w_in16, qg.reshape(1, HEAD_DIM), kg.reshape(1, HEAD_DIM),
      lng.reshape(1, GM_WIDTH), lnb.reshape(1, GM_WIDTH))


def _select_topk(scores, n_valid, axis):
    idx = lax.broadcasted_iota(jnp.int32, scores.shape, axis)
    idx_f = idx.astype(F32)
    cand = idx < n_valid
    sel = jnp.zeros(scores.shape, jnp.bool_)
    for _ in range(MOBA_TOPK):
        s_m = jnp.where(cand, scores, -jnp.inf)
        mx = jnp.max(s_m, axis=axis, keepdims=True)
        first = jnp.min(jnp.where(cand & (s_m == mx), idx_f, 1e9), axis=axis, keepdims=True)
        pick = idx_f == first
        sel = sel | pick
        cand = cand & jnp.logical_not(pick)
    return sel


_FAR_GROUP = 4
_SUM_ROWS = 16
_HEADS_PER_STEP = 8


def _attn_prompt_kernel(rb_ref, qT_ref, k_ref, vT_ref, kbar_ref, e_ref, bko_ref, bkp_ref, o_ref,
                        town, tprev, m_scr, r_scr):
    hp = pl.program_id(0)
    j = pl.program_id(1)
    tq = MOBA_BLOCK
    n_sel_rows = kbar_ref.shape[0]
    heads = range(_HEADS_PER_STEP)
    cols = [slice(i * HEAD_DIM, (i + 1) * HEAD_DIM) for i in heads]

    @pl.when(j == 0)
    def _():
        bo = bko_ref[...]
        bp = bkp_ref[...]
        key = lax.broadcasted_iota(jnp.int32, (tq, tq), 0)
        qry = lax.broadcasted_iota(jnp.int32, (tq, tq), 1)
        for i in heads:
            h = hp * _HEADS_PER_STEP + i
            b_far = rb_ref[h, NUM_BUCKETS - 1]
            to = jnp.zeros((tq, tq), F32)
            tp = jnp.zeros((tq, tq), F32)
            for b in range(NUM_BUCKETS - 1):
                val = (rb_ref[h, b] - b_far) * LOG2E
                to = jnp.where(bo == b, val, to)
                tp = jnp.where(bp == b, val, tp)
            town[i] = jnp.where(key <= qry, to, NEG)
            tprev[i] = tp

    q_sT, q_catT, prev_mask = [], [], []
    for i in heads:
        qT = qT_ref[cols[i], :]
        scoresT = jnp.dot(kbar_ref[:, cols[i]], qT, precision=lax.Precision.HIGHEST,
                          preferred_element_type=F32)
        selT = _select_topk(scoresT, j, axis=0)
        blk = lax.broadcasted_iota(jnp.int32, selT.shape, 0)
        qs = (qT * (SCALE * LOG2E)).astype(BF16)
        aug = jnp.where(selT & (blk < j - 1), 0.0, NEG)
        aug = jnp.concatenate([aug, jnp.full((HEAD_DIM - n_sel_rows, tq), NEG, F32)], axis=0)
        q_sT.append(qs)
        q_catT.append(jnp.concatenate([qs, aug.astype(BF16)], axis=0))
        prev_sel = jnp.max(jnp.where(selT & (blk == j - 1), 1.0, 0.0), axis=0, keepdims=True) > 0.0
        prev_mask.append(jnp.where(prev_sel, 0.0, NEG))

    ones_rows = jnp.ones((_SUM_ROWS, tq), BF16)

    def partial_softmax(logits, where):
        logits = [s.astype(BF16) for s in logits]
        ms = [jnp.max(s, axis=0, keepdims=True) for s in logits]
        ps = [jnp.exp2(s - m) for s, m in zip(logits, ms)]
        rs = []
        for p, (i, start) in zip(ps, where):
            vT_aug = jnp.concatenate([vT_ref[cols[i], pl.ds(start, tq)], ones_rows], axis=0)
            rs.append(jnp.dot(vT_aug, p, preferred_element_type=F32))
        return [(m.astype(F32), r) for m, r in zip(ms, rs)]

    def combine(m_run, r_run, parts):
        m_new = m_run
        for m, _ in parts:
            m_new = jnp.maximum(m_new, m)
        r_new = r_run * jnp.exp2(m_run - m_new)
        for m, r in parts:
            r_new = r_new + r * jnp.exp2(m - m_new)
        return m_new, r_new

    own_start = pl.multiple_of(j * tq, tq)
    prev_start = pl.multiple_of(jnp.maximum(j - 1, 0) * tq, tq)
    logits, where = [], []
    for i in heads:
        logits.append(jnp.dot(k_ref[pl.ds(own_start, tq), cols[i]], q_sT[i], preferred_element_type=F32)
                      + town[i])
        logits.append(jnp.dot(k_ref[pl.ds(prev_start, tq), cols[i]], q_sT[i], preferred_element_type=F32)
                      + (tprev[i] + prev_mask[i]))
        where += [(i, own_start), (i, prev_start)]
    parts = partial_softmax(logits, where)
    for i in heads:
        (m_own, r_own), part_prev = parts[2 * i], parts[2 * i + 1]
        m_scr[i], r_scr[i] = combine(m_own, r_own, [part_prev])

    def far_body(g, carry):
        starts = [pl.multiple_of((g * _FAR_GROUP + b) * tq, tq) for b in range(_FAR_GROUP)]
        logits, where = [], []
        for i in heads:
            for start in starts:
                k_cat = jnp.concatenate([k_ref[pl.ds(start, tq), cols[i]], e_ref[pl.ds(start, tq), :]], axis=1)
                logits.append(jnp.dot(k_cat, q_catT[i], preferred_element_type=F32))
                where.append((i, start))
        parts = partial_softmax(logits, where)
        for i in heads:
            m_scr[i], r_scr[i] = combine(m_scr[i], r_scr[i], parts[i * _FAR_GROUP:(i + 1) * _FAR_GROUP])
        return carry

    lax.fori_loop(0, (j + _FAR_GROUP - 2) // _FAR_GROUP, far_body, 0)

    for i in heads:
        r = r_scr[i]
        o_ref[:, cols[i]] = (r[:HEAD_DIM] / r[HEAD_DIM:HEAD_DIM + 1]).T.astype(o_ref.dtype)


def _attn_prompt(qT32, k16, vT16, kbar, rel_bias):
    seq = k16.shape[0]
    tq = MOBA_BLOCK
    n_blocks = seq // tq
    assert n_blocks % _FAR_GROUP == 0 and kbar.shape[0] == n_blocks
    pos = jnp.arange(seq, dtype=jnp.int32)
    onehot = (pos[:, None] // tq == jnp.arange(HEAD_DIM, dtype=jnp.int32)[None, :]).astype(BF16)
    r = jnp.arange(tq, dtype=jnp.int32)
    bucket_own = _t5_bucket(r[None, :] - r[:, None])
    bucket_prev = _t5_bucket(tq + r[None, :] - r[:, None])
    const = lambda h, j: (0, 0)
    hps = _HEADS_PER_STEP
    width = hps * HEAD_DIM
    return pl.pallas_call(
        _attn_prompt_kernel,
        out_shape=jax.ShapeDtypeStruct((seq, ATT_WIDTH), BF16),
        grid=(N_HEADS // hps, n_blocks),
        in_specs=[pl.BlockSpec(memory_space=pltpu.SMEM),
                  pl.BlockSpec((width, tq), lambda h, j: (h, j)),
                  pl.BlockSpec((seq, width), lambda h, j: (0, h)),
                  pl.BlockSpec((width, seq), lambda h, j: (h, 0)),
                  pl.BlockSpec((n_blocks, width), lambda h, j: (0, h)),
                  pl.BlockSpec((seq, HEAD_DIM), const),
                  pl.BlockSpec((tq, tq), const),
                  pl.BlockSpec((tq, tq), const)],
        out_specs=pl.BlockSpec((tq, width), lambda h, j: (j, h)),
        scratch_shapes=[pltpu.VMEM((hps, tq, tq), F32), pltpu.VMEM((hps, tq, tq), F32),
                        pltpu.VMEM((hps, 1, tq), F32), pltpu.VMEM((hps, HEAD_DIM + _SUM_ROWS, tq), F32)],
        compiler_params=_cparams(2),
        name="attn_prompt",
    )(rel_bias.T, qT32, k16, vT16, kbar, onehot, bucket_own, bucket_prev)


_SAMPLE_GROUP = 2


def _attn_sample_kernel(pt_ref, q_ref, kn_ref, vn_ref, rbrows_ref, bkp_ref, *rest, n_pages, dec_seq):
    k_pages = rest[:n_pages]
    v_pages = rest[n_pages:2 * n_pages]
    o_ref, k16, v16, kbar_scr, bprev_scr, bown_scr = rest[2 * n_pages:]
    b = pl.program_id(0)
    past = n_pages * PAGE_SIZE
    n_past_blocks = past // MOBA_BLOCK
    rows = N_HEADS * dec_seq
    pages_per_block = MOBA_BLOCK // PAGE_SIZE

    @pl.when(b == 0)
    def _():
        kbar_scr[...] = jnp.zeros_like(kbar_scr)
        rb = rbrows_ref[...]
        bp = bkp_ref[...]
        tp = jnp.zeros((rows, MOBA_BLOCK), F32)
        for bb in range(NUM_BUCKETS - 1):
            tp = jnp.where(bp == bb, rb[:, bb:bb + 1], tp)
        bprev_scr[...] = tp
        t_row = lax.broadcasted_iota(jnp.int32, (rows, V7X_LANES), 0) & (dec_seq - 1)
        c_col = lax.broadcasted_iota(jnp.int32, (rows, V7X_LANES), 1)
        rel = t_row - c_col
        to = jnp.zeros((rows, V7X_LANES), F32)
        for bb in range(dec_seq):
            to = jnp.where(rel == bb, rb[:, bb:bb + 1], to)
        bown_scr[...] = jnp.where(rel >= 0, to, NEG)

    q = q_ref[...]
    q_rep = jnp.concatenate([q] * N_HEADS, axis=0)
    row_head = lax.broadcasted_iota(jnp.int32, (rows, ATT_WIDTH), 0) >> int(math.log2(dec_seq))
    col_head = lax.broadcasted_iota(jnp.int32, (rows, ATT_WIDTH), 1) >> int(math.log2(HEAD_DIM))
    q_bd = jnp.where(row_head == col_head, q_rep, 0.0)
    q_s = (q_bd * SCALE).astype(BF16)

    def to_head_lanes(pages, dst, n, sums_ref):
        for h in range(N_HEADS):
            cols = slice(h * HEAD_DIM, (h + 1) * HEAD_DIM)
            total = None
            for p in range(n * pages_per_block, (n + 1) * pages_per_block):
                xh = pages[p][pl.ds(h, PAGE_SIZE, stride=N_HEADS), :]
                dst[p * PAGE_SIZE:(p + 1) * PAGE_SIZE, cols] = xh.astype(BF16)
                if sums_ref is not None:
                    cs = jnp.sum(xh, axis=0, keepdims=True)
                    total = cs if total is None else total + cs
            if sums_ref is not None:
                sums_ref[n:n + 1, cols] = total * (1.0 / MOBA_BLOCK)

    def new_rows(ref):
        pad = jnp.zeros((V7X_LANES - dec_seq, ATT_WIDTH), F32)
        return jnp.concatenate([ref[...], pad], axis=0).astype(BF16)

    groups = [slice(g * MOBA_BLOCK, (g + _SAMPLE_GROUP) * MOBA_BLOCK)
              for g in range(0, n_past_blocks, _SAMPLE_GROUP)]
    new_keys = slice(past, past + V7X_LANES)
    raw = []
    for keys in groups:
        for n in range(keys.start // MOBA_BLOCK, keys.stop // MOBA_BLOCK):
            to_head_lanes(k_pages, k16, n, kbar_scr)
        raw.append(lax.dot_general(q_s, k16[keys, :], NT_DIMS, preferred_element_type=F32))
    k16[new_keys, :] = new_rows(kn_ref)
    raw.append(lax.dot_general(q_s, k16[new_keys, :], NT_DIMS, preferred_element_type=F32))

    assert n_past_blocks == V7X_SUBLANES
    q_pad = jnp.concatenate([q_bd, jnp.zeros((V7X_LANES - rows, ATT_WIDTH), F32)], axis=0)
    scores_t = lax.dot_general(kbar_scr[0:n_past_blocks, :], q_pad, NT_DIMS, precision=lax.Precision.HIGHEST,
                               preferred_element_type=F32)
    blk = lax.broadcasted_iota(jnp.int32, scores_t.shape, 0)
    rank = jnp.zeros(scores_t.shape, F32)
    for k in range(1, n_past_blocks):
        other = pltpu.roll(scores_t, k, 0)
        ge = jnp.where(other >= scores_t, 1.0, 0.0)
        gt = jnp.where(other > scores_t, 1.0, 0.0)
        rank = rank + jnp.where(blk >= k, ge, gt)
    sel_t = jnp.where(rank < MOBA_TOPK, 1.0, 0.0)
    eye = jnp.where(lax.broadcasted_iota(jnp.int32, (rows, V7X_LANES), 0)
                    == lax.broadcasted_iota(jnp.int32, (rows, V7X_LANES), 1), 1.0, 0.0).astype(BF16)
    sel_pad = jnp.concatenate([sel_t, jnp.zeros((V7X_LANES - n_past_blocks, V7X_LANES), F32)], axis=0)
    sel = lax.dot_general(eye, sel_pad.astype(BF16), NT_DIMS, preferred_element_type=F32)
    mask_add = jnp.where(sel > 0.5, 0.0, NEG)

    pieces = []
    for keys, piece in zip(groups, raw):
        cols = []
        for i, n in enumerate(range(keys.start // MOBA_BLOCK, keys.stop // MOBA_BLOCK)):
            blk = piece[:, i * MOBA_BLOCK:(i + 1) * MOBA_BLOCK] + mask_add[:, n:n + 1]
            if n == n_past_blocks - 1:
                blk = blk + bprev_scr[...]
            cols.append(blk)
        pieces.append(jnp.concatenate(cols, axis=1))
    pieces.append(raw[-1] + bown_scr[...])
    m = pieces[0].max(axis=-1, keepdims=True)
    for piece in pieces[1:]:
        m = jnp.maximum(m, piece.max(axis=-1, keepdims=True))
    probs = [jnp.exp(piece - m) for piece in pieces]
    l = sum(p.sum(axis=-1, keepdims=True) for p in probs)

    out = None
    for keys, p in zip(groups + [new_keys], probs):
        if keys is new_keys:
            v16[keys, :] = new_rows(vn_ref)
        else:
            for n in range(keys.start // MOBA_BLOCK, keys.stop // MOBA_BLOCK):
                to_head_lanes(v_pages, v16, n, None)
        part = jnp.dot(p.astype(BF16), v16[keys, :], preferred_element_type=F32)
        out = part if out is None else out + part
    out = out / l
    for h in range(N_HEADS):
        sl = slice(h * HEAD_DIM, (h + 1) * HEAD_DIM)
        o_ref[:, sl] = out[h * dec_seq:(h + 1) * dec_seq, sl]


def _attn_sample(q3, k3, v3, cache_k, cache_v, page_table, rel_bias):
    n_seq, dec_seq, _ = q3.shape
    n_pages = page_table.shape[1]
    past = n_pages * PAGE_SIZE
    rows = N_HEADS * dec_seq
    page_rows = PAGE_SIZE * N_HEADS
    ck = cache_k.reshape(-1, HEAD_DIM)
    cv = cache_v.reshape(-1, HEAD_DIM)
    rb_rel = rel_bias.T - rel_bias[NUM_BUCKETS - 1][:, None]
    rbrows = jnp.repeat(rb_rel, dec_seq, axis=0)
    t = jnp.arange(rows, dtype=jnp.int32) % dec_seq
    c = jnp.arange(MOBA_BLOCK, dtype=jnp.int32)
    bucket_prev = _t5_bucket(MOBA_BLOCK + t[:, None] - c[None, :])
    tok = pl.BlockSpec((None, dec_seq, ATT_WIDTH), lambda b, pt: (b, 0, 0))
    const2 = lambda shape: pl.BlockSpec(shape, lambda b, pt: (0, 0))

    def page_spec(p):
        return pl.BlockSpec((page_rows, HEAD_DIM), lambda b, pt: (pt[b, p], 0))

    grid_spec = pltpu.PrefetchScalarGridSpec(
        num_scalar_prefetch=1,
        grid=(n_seq,),
        in_specs=[tok, tok, tok, const2((rows, NUM_BUCKETS)), const2((rows, MOBA_BLOCK))]
                 + [page_spec(p) for p in range(n_pages)] * 2,
        out_specs=tok,
        scratch_shapes=[pltpu.VMEM((past + V7X_LANES, ATT_WIDTH), BF16),
                        pltpu.VMEM((past + V7X_LANES, ATT_WIDTH), BF16),
                        pltpu.VMEM((V7X_LANES, ATT_WIDTH), F32),
                        pltpu.VMEM((rows, MOBA_BLOCK), F32),
                        pltpu.VMEM((rows, V7X_LANES), F32)],
    )
    return pl.pallas_call(
        functools.partial(_attn_sample_kernel, n_pages=n_pages, dec_seq=dec_seq),
        out_shape=jax.ShapeDtypeStruct((n_seq, dec_seq, ATT_WIDTH), F32),
        grid_spec=grid_spec,
        compiler_params=_cparams(1),
        name="attn_sample",
    )(page_table, q3, k3, v3, rbrows, bucket_prev, *([ck] * n_pages), *([cv] * n_pages))


def _merge_kernel(x_ref, gt_ref, u_ref, vn_ref, ga_ref, gb_ref, att_ref, wsp_ref, bsp_ref,
                  wpa_ref, wpb_ref, wo_ref, o_ref, gm_scr, *, tm):
    for c in range(tm // GM_CHUNK):
        rows = slice(c * GM_CHUNK, (c + 1) * GM_CHUNK)
        for g in range(GM_GROUPS):
            cols = slice(g * GM_CHUNK, (g + 1) * GM_CHUNK)
            s = jnp.dot(wsp_ref[g], vn_ref[rows, cols], preferred_element_type=F32) + bsp_ref[:, cols]
            gm_scr[rows, cols] = (u_ref[rows, cols].astype(F32) * s).astype(BF16)
    a = jnp.dot(gm_scr[...], wpa_ref[...], preferred_element_type=F32)
    b = jnp.dot(att_ref[...].astype(BF16), wpb_ref[...], preferred_element_type=F32)
    merged = ga_ref[...].astype(F32) * a + gb_ref[...].astype(F32) * b
    c_out = jnp.dot(merged.astype(BF16), wo_ref[...], preferred_element_type=F32)
    o_ref[...] = x_ref[...] + gt_ref[...] * c_out.reshape(o_ref.shape)


def _merge(x3, mod3, u16, vn16, ga16, gb16, att, w_sp16, b_sp, w_pa16, w_pb16, w_o16, *, prompt, tm):
    n_rows = x3.shape[0] * V7X_SUBLANES
    g_rows = tm // V7X_SUBLANES
    row3 = pl.BlockSpec((g_rows, V7X_SUBLANES, D_MODEL), lambda i: (i, 0, 0))
    seg = pl.BlockSpec((tm, SEG), lambda i: (i, 0))
    gate = pl.BlockSpec((tm, D_MODEL), lambda i: (i, 0))
    resident = lambda shape: pl.BlockSpec(shape, lambda i: (0,) * len(shape), pipeline_mode=pl.Buffered(1))
    return pl.pallas_call(
        functools.partial(_merge_kernel, tm=tm),
        out_shape=jax.ShapeDtypeStruct(x3.shape, F32),
        grid=(n_rows // tm,),
        in_specs=[row3, _mod_spec(g_rows, 2, prompt, 1), seg, seg, gate, gate, seg,
                  resident((GM_GROUPS, GM_CHUNK, GM_CHUNK)), resident((GM_CHUNK, GM_WIDTH)),
                  resident((GM_WIDTH, D_MODEL)), resident((ATT_WIDTH, D_MODEL)), resident((D_MODEL, D_MODEL))],
        out_specs=row3,
        scratch_shapes=[pltpu.VMEM((tm, GM_WIDTH), BF16)],
        compiler_params=_cparams(1),
        name="merge_prompt" if prompt else "merge_sample",
    )(x3, mod3, u16, vn16, ga16, gb16, att, w_sp16, b_sp, w_pa16, w_pb16, w_o16)


def _ffn_kernel(x_ref, sh_ref, sc_ref, gt_ref, g2_ref, wg_ref, wu_ref, wd_ref, o_ref, h_scr, *, tm):
    j = pl.program_id(1)
    last = pl.num_programs(1) - 1
    acc_ref = o_ref

    def chunk(h):
        g = jnp.dot(h, wg_ref[...], preferred_element_type=F32)
        u = jnp.dot(h, wu_ref[...], preferred_element_type=F32)
        a = (g * _sigmoid(g) * u).astype(BF16)
        return jnp.dot(a, wd_ref[...], preferred_element_type=F32)

    @pl.when(j == 0)
    def _():
        x = x_ref[...]
        ms = jnp.mean(x * x, axis=-1, keepdims=True)
        y = x * lax.rsqrt(ms + EPS) * g2_ref[...]
        h = (y * (1.0 + sc_ref[...]) + sh_ref[...]).reshape(tm, D_MODEL).astype(BF16)
        h_scr[...] = h
        acc_ref[...] = chunk(h).reshape(o_ref.shape)

    @pl.when((j > 0) & (j < last))
    def _():
        acc_ref[...] += chunk(h_scr[...]).reshape(o_ref.shape)

    @pl.when(j == last)
    def _():
        total = acc_ref[...] + chunk(h_scr[...]).reshape(o_ref.shape)
        o_ref[...] = x_ref[...] + gt_ref[...] * total


def _ffn(x3, mod3, g2, w_gu16, w_down16, *, prompt, tm, tf):
    n_rows = x3.shape[0] * V7X_SUBLANES
    g_rows = tm // V7X_SUBLANES
    n_chunks = D_FF // tf
    row3 = pl.BlockSpec((g_rows, V7X_SUBLANES, D_MODEL), lambda i, j: (i, 0, 0))
    return pl.pallas_call(
        functools.partial(_ffn_kernel, tm=tm),
        out_shape=jax.ShapeDtypeStruct(x3.shape, F32),
        grid=(n_rows // tm, n_chunks),
        in_specs=[row3, _mod_spec(g_rows, 3, prompt, 2), _mod_spec(g_rows, 4, prompt, 2),
                  _mod_spec(g_rows, 5, prompt, 2),
                  pl.BlockSpec((1, 1, D_MODEL), lambda i, j: (0, 0, 0)),
                  pl.BlockSpec((D_MODEL, tf), lambda i, j: (0, j)),
                  pl.BlockSpec((D_MODEL, tf), lambda i, j: (0, n_chunks + j)),
                  pl.BlockSpec((tf, D_MODEL), lambda i, j: (j, 0))],
        out_specs=row3,
        scratch_shapes=[pltpu.VMEM((tm, D_MODEL), BF16)],
        compiler_params=_cparams(2),
        name="ffn_prompt" if prompt else "ffn_sample",
    )(x3, mod3, mod3, mod3, g2.reshape(1, 1, D_MODEL), w_gu16, w_gu16, w_down16)


def _gm_spatial_weights(gm_ws, gm_bs, dec_seq):
    tril = jnp.tril(jnp.ones((GM_CHUNK, GM_CHUNK), dtype=bool))
    w = jnp.where(tril[None], gm_ws, jnp.zeros_like(gm_ws))
    w_prompt = w.astype(BF16)
    b_prompt = jnp.repeat(gm_bs.T, GM_WIDTH // GM_GROUPS, axis=1)
    reps = GM_CHUNK // dec_seq
    eye = jnp.eye(reps, dtype=F32)
    w_small = w[:, :dec_seq, :dec_seq]
    w_sample = jnp.einsum('ab,gts->gatbs', eye, w_small).reshape(GM_GROUPS, GM_CHUNK, GM_CHUNK).astype(BF16)
    b_sample = jnp.tile(jnp.repeat(gm_bs[:, :dec_seq].T, GM_WIDTH // GM_GROUPS, axis=1), (reps, 1))
    return w_prompt, b_prompt, w_sample, b_sample


def kernel(x_prompt, x_sample, c_prompt, c_sample, cache_k, cache_v, page_table, w_ada, b_ada, norm1_g, w_in,
           q_norm_g, k_norm_g, gm_ln_g, gm_ln_b, gm_ws, gm_bs, w_pa, w_pb, w_o, norm2_g, w_gu, w_down,
           rel_bias):
    depth = w_in.shape[0]
    assert depth == 1, "single-layer step"
    n_seq, dec_seq, _ = x_sample.shape
    batch, seq, _ = x_prompt.shape
    assert batch == 1 and n_seq == _PROMPT_MOD_ROW and dec_seq == V7X_SUBLANES

    pad_rows = V7X_SUBLANES - 1
    c_all = jnp.concatenate([c_sample, c_prompt, jnp.zeros((pad_rows, D_MODEL), F32)], axis=0)
    mod = _modulation(c_all, w_ada[0], b_ada[0])
    mod3 = mod.reshape(mod.shape[0], 1, 6 * D_MODEL)

    w_in16 = w_in[0].astype(BF16)
    w_pa16 = w_pa[0].astype(BF16)
    w_pb16 = w_pb[0].astype(BF16)
    w_o16 = w_o[0].astype(BF16)
    w_gu16 = w_gu[0].astype(BF16)
    w_down16 = w_down[0].astype(BF16)
    wsp_p, bsp_p, wsp_s, bsp_s = _gm_spatial_weights(gm_ws[0], gm_bs[0], dec_seq)

    xp3 = x_prompt.reshape(seq // V7X_SUBLANES, V7X_SUBLANES, D_MODEL)
    xs3 = x_sample

    proj = functools.partial(_inproj, g1=norm1_g[0], w_in16=w_in16, qg=q_norm_g[0], kg=k_norm_g[0],
                             lng=gm_ln_g[0], lnb=gm_ln_b[0])
    tm_prompt = 512
    qT32, k16, vT16, k_out, v_out, u16, vn16, ga16, gb16, kbar_tiles = proj(xp3, mod3, prompt=True, tm=tm_prompt)
    per_tile = tm_prompt // MOBA_BLOCK
    kbar = kbar_tiles.reshape(-1, V7X_SUBLANES, ATT_WIDTH)[:, :per_tile].reshape(-1, ATT_WIDTH)
    sq32, sk32, sv32, sk_out, sv_out, su16, svn16, svn32, sga16, sgb16 = proj(xs3, mod3, prompt=False, tm=256)

    att_p = _attn_prompt(qT32, k16, vT16, kbar, rel_bias)
    tok3 = lambda a: a.reshape(n_seq, dec_seq, ATT_WIDTH)
    att_s = _attn_sample(tok3(sq32), tok3(sk32), tok3(sv32), cache_k[0], cache_v[0], page_table, rel_bias)
    att_s = att_s.reshape(n_seq * dec_seq, ATT_WIDTH)

    mrg = functools.partial(_merge, w_pa16=w_pa16, w_pb16=w_pb16, w_o16=w_o16, tm=256)
    x1p = mrg(xp3, mod3, u16, vn16, ga16, gb16, att_p, wsp_p, bsp_p, prompt=True)
    x1s = mrg(xs3, mod3, su16, svn16, sga16, sgb16, att_s, wsp_s, bsp_s, prompt=False)

    ffn = functools.partial(_ffn, g2=norm2_g[0], w_gu16=w_gu16, w_down16=w_down16, tf=512)
    y_p = ffn(x1p, mod3, prompt=True, tm=1024).reshape(batch, seq, D_MODEL)
    y_s = ffn(x1s, mod3, prompt=False, tm=1024)

    kv_p = lambda a: a.reshape(1, batch, seq, N_HEADS, HEAD_DIM)
    kv_s = lambda a: a.reshape(1, n_seq, dec_seq, N_HEADS, HEAD_DIM)
    return (y_p, y_s, kv_p(k_out), kv_p(v_out), kv_s(sk_out), kv_s(sv_out),
            svn32.reshape(1, n_seq, dec_seq, GM_WIDTH))
```

```python
import functools
import math

import jax
import jax.numpy as jnp
from jax import lax
from jax.experimental import pallas as pl
from jax.experimental.pallas import tpu as pltpu

F32 = jnp.float32
BF16 = jnp.bfloat16

D_MODEL = 2048
N_HEADS = 8
HEAD_DIM = 128
ATT_WIDTH = N_HEADS * HEAD_DIM
GM_WIDTH = 1024
GM_GROUPS = 8
GM_CHUNK = 128
MOBA_BLOCK = 256
MOBA_TOPK = 3
NUM_BUCKETS = 32
MAX_DISTANCE = 128
PAGE_SIZE = 128
D_FF = 5632
EPS = 1e-6
SCALE = HEAD_DIM ** -0.5
LOG2E = math.log2(math.e)

V7X_LANES = 128
V7X_SUBLANES = 8
V7X_VMEM_BYTES = 64 * 1024 * 1024
VMEM_LIMIT = V7X_VMEM_BYTES - 8 * 1024 * 1024

NEG = -1e30
SEG = 1024

NT_DIMS = (((1,), (1,)), ((), ()))


def _cparams(n_axes, vmem=VMEM_LIMIT):
    return pltpu.CompilerParams(dimension_semantics=("arbitrary",) * n_axes, vmem_limit_bytes=vmem)


def _sigmoid(x):
    return 1.0 / (1.0 + jnp.exp(-x))


def _gelu_tanh(x):
    c = math.sqrt(2.0 / math.pi)
    return x * (0.5 * (1.0 + jnp.tanh(c * (x + 0.044715 * (x * x * x)))))


def _t5_bucket(rel):
    n = jnp.maximum(rel, 0)
    max_exact = NUM_BUCKETS // 2
    nf = jnp.maximum(n, 1).astype(F32)
    large = max_exact + (jnp.log(nf / max_exact) / math.log(MAX_DISTANCE / max_exact)
                         * (NUM_BUCKETS - max_exact)).astype(jnp.int32)
    large = jnp.minimum(large, NUM_BUCKETS - 1)
    return jnp.where(n < max_exact, n, large)


def _mod_kernel(c_ref, w_ref, b_ref, o_ref):
    c = c_ref[...]
    s = (c * _sigmoid(c)).astype(BF16)
    o_ref[...] = jnp.dot(s, w_ref[...].astype(BF16), preferred_element_type=F32) + b_ref[...]


def _modulation(c_all, w_ada, b_ada):
    rows = c_all.shape[0]
    n_out = w_ada.shape[1]
    tn = 1024
    return pl.pallas_call(
        _mod_kernel,
        out_shape=jax.ShapeDtypeStruct((rows, n_out), F32),
        grid=(n_out // tn,),
        in_specs=[pl.BlockSpec((rows, D_MODEL), lambda j: (0, 0)),
                  pl.BlockSpec((D_MODEL, tn), lambda j: (0, j)),
                  pl.BlockSpec((1, tn), lambda j: (0, j))],
        out_specs=pl.BlockSpec((rows, tn), lambda j: (0, j)),
        compiler_params=_cparams(1),
        name="mod",
    )(c_all, w_ada, b_ada.reshape(1, n_out))


_SEG_K, _SEG_Q, _SEG_V, _SEG_U, _SEG_VG, _SEG_GA0, _SEG_GA1, _SEG_GB0, _SEG_GB1 = range(9)


def _inproj_kernel(x_ref, sh_ref, sc_ref, g1_ref, w_ref, qg_ref, kg_ref, lng_ref, lnb_ref,
                   *rest, tm, prompt):
    if prompt:
        qT32, k16, vT16, k_out, v_out, u16, vn16, ga16, gb16, kbar, h_scr = rest
        q32 = k32 = v32 = None
    else:
        q32, k32, v32, k_out, v_out, u16, vn16, vn32, ga16, gb16, h_scr = rest
        qT32 = k16 = vT16 = None
    j = pl.program_id(1)

    def normed_input():
        x = x_ref[...]
        ms = jnp.mean(x * x, axis=-1, keepdims=True)
        y = x * lax.rsqrt(ms + EPS) * g1_ref[...]
        h = y * (1.0 + sc_ref[...]) + sh_ref[...]
        return h.reshape(tm, D_MODEL).astype(BF16)

    sub = MOBA_BLOCK
    sub_tiles = [slice(r * sub, (r + 1) * sub) for r in range(tm // sub)]

    def matmul(rows, h=None):
        lhs = h_scr[rows, :] if h is None else h[rows]
        return jnp.dot(lhs, w_ref[...], preferred_element_type=F32)

    def store_heads(h_val, y_of_head, dst32, dst16, dst_out, with_means, dst_t=None):
        for r, rows in enumerate(sub_tiles):
            z = matmul(rows, h_val)
            for h in range(N_HEADS):
                sl = slice(h * HEAD_DIM, (h + 1) * HEAD_DIM)
                yh = y_of_head(z[:, sl])
                if dst32 is not None:
                    dst32[rows, sl] = yh
                if dst16 is not None:
                    dst16[rows, sl] = yh.astype(BF16)
                if dst_t is not None:
                    dst_t[sl, rows] = yh.T.astype(dst_t.dtype)
                if dst_out is not None:
                    dst_out[pl.ds(rows.start * N_HEADS + h, sub, stride=N_HEADS), :] = yh
                if with_means:
                    kbar[r:r + 1, sl] = jnp.mean(yh, axis=0, keepdims=True)

    def head_norm(g_ref):
        def f(zh):
            ms = jnp.mean(zh * zh, axis=-1, keepdims=True)
            return zh * lax.rsqrt(ms + EPS) * g_ref[...]
        return f

    def elementwise(f, dsts):
        for rows in sub_tiles:
            y = f(matmul(rows))
            for dst in dsts:
                dst[rows, :] = y.astype(dst.dtype)

    def gelu_layernorm(z):
        a = _gelu_tanh(z)
        mu = jnp.mean(a, axis=-1, keepdims=True)
        ac = a - mu
        y = ac * lax.rsqrt(jnp.mean(ac * ac, axis=-1, keepdims=True) + EPS)
        return y * lng_ref[...] + lnb_ref[...]

    @pl.when(j == _SEG_K)
    def _():
        h = normed_input()
        h_scr[...] = h
        if prompt:
            kbar[...] = jnp.zeros_like(kbar)
        store_heads(h, head_norm(kg_ref), k32, k16, k_out, prompt)

    @pl.when(j == _SEG_Q)
    def _():
        store_heads(None, head_norm(qg_ref), q32, None, None, False, dst_t=qT32)

    @pl.when(j == _SEG_V)
    def _():
        store_heads(None, lambda zh: zh, v32, None, v_out, False, dst_t=vT16)

    @pl.when(j == _SEG_U)
    def _():
        elementwise(_gelu_tanh, [u16])

    @pl.when(j == _SEG_VG)
    def _():
        elementwise(gelu_layernorm, [vn16] if prompt else [vn16, vn32])

    @pl.when((j == _SEG_GA0) | (j == _SEG_GA1))
    def _():
        elementwise(_sigmoid, [ga16])

    @pl.when((j == _SEG_GB0) | (j == _SEG_GB1))
    def _():
        elementwise(_sigmoid, [gb16])


def _mod_spec(g_rows, chunk, prompt, n_grid):
    if n_grid == 2:
        if prompt:
            return pl.BlockSpec((1, 1, D_MODEL), lambda i, j: (_PROMPT_MOD_ROW, 0, chunk))
        return pl.BlockSpec((g_rows, 1, D_MODEL), lambda i, j: (i, 0, chunk))
    if prompt:
        return pl.BlockSpec((1, 1, D_MODEL), lambda i: (_PROMPT_MOD_ROW, 0, chunk))
    return pl.BlockSpec((g_rows, 1, D_MODEL), lambda i: (i, 0, chunk))


_PROMPT_MOD_ROW = 128


def _w_in_col(j):
    return jnp.where(j < 2, 1 - j, j)


def _inproj(x3, mod3, g1, w_in16, qg, kg, lng, lnb, *, prompt, tm):
    n_rows = x3.shape[0] * V7X_SUBLANES
    g_rows = tm // V7X_SUBLANES
    grid = (n_rows // tm, 9)
    row_blk = lambda i, j: (i, 0)
    f32_seg = jax.ShapeDtypeStruct((n_rows, SEG), F32)
    b16_seg = jax.ShapeDtypeStruct((n_rows, SEG), BF16)
    b16_gate = jax.ShapeDtypeStruct((n_rows, D_MODEL), BF16)
    seg_spec = pl.BlockSpec((tm, SEG), row_blk)
    kv_out = jax.ShapeDtypeStruct((n_rows * N_HEADS, HEAD_DIM), F32)
    kv_spec = pl.BlockSpec((tm * N_HEADS, HEAD_DIM), row_blk)
    ga_spec = pl.BlockSpec((tm, SEG), lambda i, j: (i, jnp.clip(j - _SEG_GA0, 0, 1)))
    gb_spec = pl.BlockSpec((tm, SEG), lambda i, j: (i, jnp.clip(j - _SEG_GB0, 0, 1)))
    if prompt:
        assert tm % MOBA_BLOCK == 0 and tm // MOBA_BLOCK <= V7X_SUBLANES
        t_spec = pl.BlockSpec((SEG, tm), lambda i, j: (0, i))
        out_shape = [jax.ShapeDtypeStruct((SEG, n_rows), F32), b16_seg, jax.ShapeDtypeStruct((SEG, n_rows), BF16),
                     kv_out, kv_out, b16_seg, b16_seg, b16_gate, b16_gate,
                     jax.ShapeDtypeStruct((grid[0] * V7X_SUBLANES, SEG), F32)]
        out_specs = ([t_spec, seg_spec, t_spec] + [kv_spec] * 2 + [seg_spec] * 2
                     + [ga_spec, gb_spec, pl.BlockSpec((V7X_SUBLANES, SEG), row_blk)])
    else:
        out_shape = [f32_seg, f32_seg, f32_seg, kv_out, kv_out, b16_seg, b16_seg, f32_seg, b16_gate, b16_gate]
        out_specs = [seg_spec] * 3 + [kv_spec] * 2 + [seg_spec] * 3 + [ga_spec, gb_spec]
    vec = lambda n: pl.BlockSpec((1, n), lambda i, j: (0, 0))
    return pl.pallas_call(
        functools.partial(_inproj_kernel, tm=tm, prompt=prompt),
        out_shape=out_shape,
        grid=grid,
        in_specs=[pl.BlockSpec((g_rows, V7X_SUBLANES, D_MODEL), lambda i, j: (i, 0, 0)),
                  _mod_spec(g_rows, 0, prompt, 2),
                  _mod_spec(g_rows, 1, prompt, 2),
                  pl.BlockSpec((1, 1, D_MODEL), lambda i, j: (0, 0, 0)),
                  pl.BlockSpec((D_MODEL, SEG), lambda i, j: (0, _w_in_col(j))),
                  vec(HEAD_DIM), vec(HEAD_DIM), vec(GM_WIDTH), vec(GM_WIDTH)],
        out_specs=out_specs,
        scratch_shapes=[pltpu.VMEM((tm, D_MODEL), BF16)],
        compiler_params=_cparams(2),
        name="inproj_prompt" if prompt else "inproj_sample",
    )(x3, mod3, mod3, g1.reshape(1, 1, D_MODEL), w_in16, qg.reshape(1, HEAD_DIM), kg.reshape(1, HEAD_DIM),
      lng.reshape(1, GM_WIDTH), lnb.reshape(1, GM_WIDTH))


def _select_topk(scores, n_valid, axis):
    idx = lax.broadcasted_iota(jnp.int32, scores.shape, axis)
    idx_f = idx.astype(F32)
    cand = idx < n_valid
    sel = jnp.zeros(scores.shape, jnp.bool_)
    for _ in range(MOBA_TOPK):
        s_m = jnp.where(cand, scores, -jnp.inf)
        mx = jnp.max(s_m, axis=axis, keepdims=True)
        first = jnp.min(jnp.where(cand & (s_m == mx), idx_f, 1e9), axis=axis, keepdims=True)
        pick = idx_f == first
        sel = sel | pick
        cand = cand & jnp.logical_not(pick)
    return sel


_FAR_GROUP = 4
_SUM_ROWS = 16
_HEADS_PER_STEP = 8


def _attn_prompt_kernel(rb_ref, qT_ref, k_ref, vT_ref, kbar_ref, e_ref, bko_ref, bkp_ref, o_ref,
                        town, tprev, m_scr, r_scr):
    hp = pl.program_id(0)
    j = pl.program_id(1)
    tq = MOBA_BLOCK
    n_sel_rows = kbar_ref.shape[0]
    heads = range(_HEADS_PER_STEP)
    cols = [slice(i * HEAD_DIM, (i + 1) * HEAD_DIM) for i in heads]

    @pl.when(j == 0)
    def _():
        bo = bko_ref[...]
        bp = bkp_ref[...]
        key = lax.broadcasted_iota(jnp.int32, (tq, tq), 0)
        qry = lax.broadcasted_iota(jnp.int32, (tq, tq), 1)
        for i in heads:
            h = hp * _HEADS_PER_STEP + i
            b_far = rb_ref[h, NUM_BUCKETS - 1]
            to = jnp.zeros((tq, tq), F32)
            tp = jnp.zeros((tq, tq), F32)
            for b in range(NUM_BUCKETS - 1):
                val = (rb_ref[h, b] - b_far) * LOG2E
                to = jnp.where(bo == b, val, to)
                tp = jnp.where(bp == b, val, tp)
            town[i] = jnp.where(key <= qry, to, NEG)
            tprev[i] = tp

    q_sT, q_catT, prev_mask = [], [], []
    for i in heads:
        qT = qT_ref[cols[i], :]
        scoresT = jnp.dot(kbar_ref[:, cols[i]], qT, precision=lax.Precision.HIGHEST,
                          preferred_element_type=F32)
        selT = _select_topk(scoresT, j, axis=0)
        blk = lax.broadcasted_iota(jnp.int32, selT.shape, 0)
        qs = (qT * (SCALE * LOG2E)).astype(BF16)
        aug = jnp.where(selT & (blk < j - 1), 0.0, NEG)
        aug = jnp.concatenate([aug, jnp.full((HEAD_DIM - n_sel_rows, tq), NEG, F32)], axis=0)
        q_sT.append(qs)
        q_catT.append(jnp.concatenate([qs, aug.astype(BF16)], axis=0))
        prev_sel = jnp.max(jnp.where(selT & (blk == j - 1), 1.0, 0.0), axis=0, keepdims=True) > 0.0
        prev_mask.append(jnp.where(prev_sel, 0.0, NEG))

    def group_softmax(logits, key_slices):
        logits = [[s.astype(BF16) for s in per_head] for per_head in logits]
        ms = []
        for per_head in logits:
            m = jnp.max(per_head[0], axis=0, keepdims=True)
            for s in per_head[1:]:
                m = jnp.maximum(m, jnp.max(s, axis=0, keepdims=True))
            ms.append(m)
        ps = [jnp.concatenate([jnp.exp2(s - m) for s in per_head], axis=0) for per_head, m in zip(logits, ms)]
        parts = []
        for i in heads:
            vT = jnp.concatenate([vT_ref[cols[i], ks] for ks in key_slices[i]], axis=1)
            vT_aug = jnp.concatenate([vT, jnp.ones((_SUM_ROWS, vT.shape[1]), BF16)], axis=0)
            parts.append((ms[i].astype(F32), jnp.dot(vT_aug, ps[i], preferred_element_type=F32)))
        return parts

    def combine(m_run, r_run, part):
        m, r = part
        m_new = jnp.maximum(m_run, m)
        return m_new, r_run * jnp.exp2(m_run - m_new) + r * jnp.exp2(m - m_new)

    own_start = pl.multiple_of(j * tq, tq)
    prev_start = pl.multiple_of(jnp.maximum(j - 1, 0) * tq, tq)
    first = [pl.ds(own_start, tq), pl.ds(prev_start, tq)]
    logits = []
    for i in heads:
        logits.append([jnp.dot(k_ref[first[0], cols[i]], q_sT[i], preferred_element_type=F32) + town[i],
                       jnp.dot(k_ref[first[1], cols[i]], q_sT[i], preferred_element_type=F32)
                       + (tprev[i] + prev_mask[i])])
    for i, (m, r) in enumerate(group_softmax(logits, [first] * len(heads))):
        m_scr[i] = m
        r_scr[i] = r

    def far_body(g, carry):
        starts = [pl.multiple_of((g * _FAR_GROUP + b) * tq, tq) for b in range(_FAR_GROUP)]
        group_keys = [pl.ds(starts[0], _FAR_GROUP * tq)]
        logits = []
        for i in heads:
            per_head = []
            for start in starts:
                k_cat = jnp.concatenate([k_ref[pl.ds(start, tq), cols[i]], e_ref[pl.ds(start, tq), :]], axis=1)
                per_head.append(jnp.dot(k_cat, q_catT[i], preferred_element_type=F32))
            logits.append(per_head)
        for i, part in enumerate(group_softmax(logits, [group_keys] * len(heads))):
            m_scr[i], r_scr[i] = combine(m_scr[i], r_scr[i], part)
        return carry

    lax.fori_loop(0, (j + _FAR_GROUP - 2) // _FAR_GROUP, far_body, 0)

    for i in heads:
        r = r_scr[i]
        o_ref[:, cols[i]] = (r[:HEAD_DIM] / r[HEAD_DIM:HEAD_DIM + 1]).T.astype(o_ref.dtype)


def _attn_prompt(qT32, k16, vT16, kbar, rel_bias):
    seq = k16.shape[0]
    tq = MOBA_BLOCK
    n_blocks = seq // tq
    assert n_blocks % _FAR_GROUP == 0 and kbar.shape[0] == n_blocks
    pos = jnp.arange(seq, dtype=jnp.int32)
    onehot = (pos[:, None] // tq == jnp.arange(HEAD_DIM, dtype=jnp.int32)[None, :]).astype(BF16)
    r = jnp.arange(tq, dtype=jnp.int32)
    bucket_own = _t5_bucket(r[None, :] - r[:, None])
    bucket_prev = _t5_bucket(tq + r[None, :] - r[:, None])
    const = lambda h, j: (0, 0)
    hps = _HEADS_PER_STEP
    width = hps * HEAD_DIM
    return pl.pallas_call(
        _attn_prompt_kernel,
        out_shape=jax.ShapeDtypeStruct((seq, ATT_WIDTH), BF16),
        grid=(N_HEADS // hps, n_blocks),
        in_specs=[pl.BlockSpec(memory_space=pltpu.SMEM),
                  pl.BlockSpec((width, tq), lambda h, j: (h, j)),
                  pl.BlockSpec((seq, width), lambda h, j: (0, h)),
                  pl.BlockSpec((width, seq), lambda h, j: (h, 0)),
                  pl.BlockSpec((n_blocks, width), lambda h, j: (0, h)),
                  pl.BlockSpec((seq, HEAD_DIM), const),
                  pl.BlockSpec((tq, tq), const),
                  pl.BlockSpec((tq, tq), const)],
        out_specs=pl.BlockSpec((tq, width), lambda h, j: (j, h)),
        scratch_shapes=[pltpu.VMEM((hps, tq, tq), F32), pltpu.VMEM((hps, tq, tq), F32),
                        pltpu.VMEM((hps, 1, tq), F32), pltpu.VMEM((hps, HEAD_DIM + _SUM_ROWS, tq), F32)],
        compiler_params=_cparams(2),
        name="attn_prompt",
    )(rel_bias.T, qT32, k16, vT16, kbar, onehot, bucket_own, bucket_prev)


_SAMPLE_GROUP = 2


def _attn_sample_kernel(pt_ref, q_ref, kn_ref, vn_ref, rbrows_ref, bkp_ref, *rest, n_pages, dec_seq):
    k_pages = rest[:n_pages]
    v_pages = rest[n_pages:2 * n_pages]
    o_ref, k16, v16, kbar_scr, bprev_scr, bown_scr = rest[2 * n_pages:]
    b = pl.program_id(0)
    past = n_pages * PAGE_SIZE
    n_past_blocks = past // MOBA_BLOCK
    rows = N_HEADS * dec_seq
    pages_per_block = MOBA_BLOCK // PAGE_SIZE

    @pl.when(b == 0)
    def _():
        kbar_scr[...] = jnp.zeros_like(kbar_scr)
        rb = rbrows_ref[...]
        bp = bkp_ref[...]
        tp = jnp.zeros((rows, MOBA_BLOCK), F32)
        for bb in range(NUM_BUCKETS - 1):
            tp = jnp.where(bp == bb, rb[:, bb:bb + 1], tp)
        bprev_scr[...] = tp
        t_row = lax.broadcasted_iota(jnp.int32, (rows, V7X_LANES), 0) & (dec_seq - 1)
        c_col = lax.broadcasted_iota(jnp.int32, (rows, V7X_LANES), 1)
        rel = t_row - c_col
        to = jnp.zeros((rows, V7X_LANES), F32)
        for bb in range(dec_seq):
            to = jnp.where(rel == bb, rb[:, bb:bb + 1], to)
        bown_scr[...] = jnp.where(rel >= 0, to, NEG)

    q = q_ref[...]
    q_rep = jnp.concatenate([q] * N_HEADS, axis=0)
    row_head = lax.broadcasted_iota(jnp.int32, (rows, ATT_WIDTH), 0) >> int(math.log2(dec_seq))
    col_head = lax.broadcasted_iota(jnp.int32, (rows, ATT_WIDTH), 1) >> int(math.log2(HEAD_DIM))
    q_bd = jnp.where(row_head == col_head, q_rep, 0.0)
    q_s = (q_bd * SCALE).astype(BF16)

    def to_head_lanes(pages, dst, n, sums_ref):
        for h in range(N_HEADS):
            cols = slice(h * HEAD_DIM, (h + 1) * HEAD_DIM)
            total = None
            for p in range(n * pages_per_block, (n + 1) * pages_per_block):
                xh = pages[p][pl.ds(h, PAGE_SIZE, stride=N_HEADS), :]
                dst[p * PAGE_SIZE:(p + 1) * PAGE_SIZE, cols] = xh.astype(BF16)
                if sums_ref is not None:
                    cs = jnp.sum(xh, axis=0, keepdims=True)
                    total = cs if total is None else total + cs
            if sums_ref is not None:
                sums_ref[n:n + 1, cols] = total * (1.0 / MOBA_BLOCK)

    def new_rows(ref):
        pad = jnp.zeros((V7X_LANES - dec_seq, ATT_WIDTH), F32)
        return jnp.concatenate([ref[...], pad], axis=0).astype(BF16)

    groups = [slice(g * MOBA_BLOCK, (g + _SAMPLE_GROUP) * MOBA_BLOCK)
              for g in range(0, n_past_blocks, _SAMPLE_GROUP)]
    new_keys = slice(past, past + V7X_LANES)
    raw = []
    for keys in groups:
        for n in range(keys.start // MOBA_BLOCK, keys.stop // MOBA_BLOCK):
            to_head_lanes(k_pages, k16, n, kbar_scr)
        raw.append(lax.dot_general(q_s, k16[keys, :], NT_DIMS, preferred_element_type=F32))
    k16[new_keys, :] = new_rows(kn_ref)
    raw.append(lax.dot_general(q_s, k16[new_keys, :], NT_DIMS, preferred_element_type=F32))

    assert n_past_blocks == V7X_SUBLANES
    q_pad = jnp.concatenate([q_bd, jnp.zeros((V7X_LANES - rows, ATT_WIDTH), F32)], axis=0)
    scores_t = lax.dot_general(kbar_scr[0:n_past_blocks, :], q_pad, NT_DIMS, precision=lax.Precision.HIGHEST,
                               preferred_element_type=F32)
    blk = lax.broadcasted_iota(jnp.int32, scores_t.shape, 0)
    rank = jnp.zeros(scores_t.shape, F32)
    for k in range(1, n_past_blocks):
        other = pltpu.roll(scores_t, k, 0)
        ge = jnp.where(other >= scores_t, 1.0, 0.0)
        gt = jnp.where(other > scores_t, 1.0, 0.0)
        rank = rank + jnp.where(blk >= k, ge, gt)
    sel_t = jnp.where(rank < MOBA_TOPK, 1.0, 0.0)
    eye = jnp.where(lax.broadcasted_iota(jnp.int32, (rows, V7X_LANES), 0)
                    == lax.broadcasted_iota(jnp.int32, (rows, V7X_LANES), 1), 1.0, 0.0).astype(BF16)
    sel_pad = jnp.concatenate([sel_t, jnp.zeros((V7X_LANES - n_past_blocks, V7X_LANES), F32)], axis=0)
    sel = lax.dot_general(eye, sel_pad.astype(BF16), NT_DIMS, preferred_element_type=F32)
    mask_add = jnp.where(sel > 0.5, 0.0, NEG)

    pieces = []
    for keys, piece in zip(groups, raw):
        cols = []
        for i, n in enumerate(range(keys.start // MOBA_BLOCK, keys.stop // MOBA_BLOCK)):
            blk = piece[:, i * MOBA_BLOCK:(i + 1) * MOBA_BLOCK] + mask_add[:, n:n + 1]
            if n == n_past_blocks - 1:
                blk = blk + bprev_scr[...]
            cols.append(blk)
        pieces.append(jnp.concatenate(cols, axis=1))
    pieces.append(raw[-1] + bown_scr[...])
    m = pieces[0].max(axis=-1, keepdims=True)
    for piece in pieces[1:]:
        m = jnp.maximum(m, piece.max(axis=-1, keepdims=True))
    probs = [jnp.exp(piece - m) for piece in pieces]
    l = sum(p.sum(axis=-1, keepdims=True) for p in probs)

    out = None
    for keys, p in zip(groups + [new_keys], probs):
        if keys is new_keys:
            v16[keys, :] = new_rows(vn_ref)
        else:
            for n in range(keys.start // MOBA_BLOCK, keys.stop // MOBA_BLOCK):
                to_head_lanes(v_pages, v16, n, None)
        part = jnp.dot(p.astype(BF16), v16[keys, :], preferred_element_type=F32)
        out = part if out is None else out + part
    out = out / l
    for h in range(N_HEADS):
        sl = slice(h * HEAD_DIM, (h + 1) * HEAD_DIM)
        o_ref[:, sl] = out[h * dec_seq:(h + 1) * dec_seq, sl]


def _attn_sample(q3, k3, v3, cache_k, cache_v, page_table, rel_bias):
    n_seq, dec_seq, _ = q3.shape
    n_pages = page_table.shape[1]
    past = n_pages * PAGE_SIZE
    rows = N_HEADS * dec_seq
    page_rows = PAGE_SIZE * N_HEADS
    ck = cache_k.reshape(-1, HEAD_DIM)
    cv = cache_v.reshape(-1, HEAD_DIM)
    rb_rel = rel_bias.T - rel_bias[NUM_BUCKETS - 1][:, None]
    rbrows = jnp.repeat(rb_rel, dec_seq, axis=0)
    t = jnp.arange(rows, dtype=jnp.int32) % dec_seq
    c = jnp.arange(MOBA_BLOCK, dtype=jnp.int32)
    bucket_prev = _t5_bucket(MOBA_BLOCK + t[:, None] - c[None, :])
    tok = pl.BlockSpec((None, dec_seq, ATT_WIDTH), lambda b, pt: (b, 0, 0))
    const2 = lambda shape: pl.BlockSpec(shape, lambda b, pt: (0, 0))

    def page_spec(p):
        return pl.BlockSpec((page_rows, HEAD_DIM), lambda b, pt: (pt[b, p], 0))

    grid_spec = pltpu.PrefetchScalarGridSpec(
        num_scalar_prefetch=1,
        grid=(n_seq,),
        in_specs=[tok, tok, tok, const2((rows, NUM_BUCKETS)), const2((rows, MOBA_BLOCK))]
                 + [page_spec(p) for p in range(n_pages)] * 2,
        out_specs=tok,
        scratch_shapes=[pltpu.VMEM((past + V7X_LANES, ATT_WIDTH), BF16),
                        pltpu.VMEM((past + V7X_LANES, ATT_WIDTH), BF16),
                        pltpu.VMEM((V7X_LANES, ATT_WIDTH), F32),
                        pltpu.VMEM((rows, MOBA_BLOCK), F32),
                        pltpu.VMEM((rows, V7X_LANES), F32)],
    )
    return pl.pallas_call(
        functools.partial(_attn_sample_kernel, n_pages=n_pages, dec_seq=dec_seq),
        out_shape=jax.ShapeDtypeStruct((n_seq, dec_seq, ATT_WIDTH), F32),
        grid_spec=grid_spec,
        compiler_params=_cparams(1),
        name="attn_sample",
    )(page_table, q3, k3, v3, rbrows, bucket_prev, *([ck] * n_pages), *([cv] * n_pages))


def _merge_kernel(x_ref, gt_ref, u_ref, vn_ref, ga_ref, gb_ref, att_ref, wsp_ref, bsp_ref,
                  wpa_ref, wpb_ref, wo_ref, o_ref, gm_scr, *, tm):
    for c in range(tm // GM_CHUNK):
        rows = slice(c * GM_CHUNK, (c + 1) * GM_CHUNK)
        for g in range(GM_GROUPS):
            cols = slice(g * GM_CHUNK, (g + 1) * GM_CHUNK)
            s = jnp.dot(wsp_ref[g], vn_ref[rows, cols], preferred_element_type=F32) + bsp_ref[:, cols]
            gm_scr[rows, cols] = (u_ref[rows, cols].astype(F32) * s).astype(BF16)
    a = jnp.dot(gm_scr[...], wpa_ref[...], preferred_element_type=F32)
    b = jnp.dot(att_ref[...].astype(BF16), wpb_ref[...], preferred_element_type=F32)
    merged = ga_ref[...].astype(F32) * a + gb_ref[...].astype(F32) * b
    c_out = jnp.dot(merged.astype(BF16), wo_ref[...], preferred_element_type=F32)
    o_ref[...] = x_ref[...] + gt_ref[...] * c_out.reshape(o_ref.shape)


def _merge(x3, mod3, u16, vn16, ga16, gb16, att, w_sp16, b_sp, w_pa16, w_pb16, w_o16, *, prompt, tm):
    n_rows = x3.shape[0] * V7X_SUBLANES
    g_rows = tm // V7X_SUBLANES
    row3 = pl.BlockSpec((g_rows, V7X_SUBLANES, D_MODEL), lambda i: (i, 0, 0))
    seg = pl.BlockSpec((tm, SEG), lambda i: (i, 0))
    gate = pl.BlockSpec((tm, D_MODEL), lambda i: (i, 0))
    resident = lambda shape: pl.BlockSpec(shape, lambda i: (0,) * len(shape), pipeline_mode=pl.Buffered(1))
    return pl.pallas_call(
        functools.partial(_merge_kernel, tm=tm),
        out_shape=jax.ShapeDtypeStruct(x3.shape, F32),
        grid=(n_rows // tm,),
        in_specs=[row3, _mod_spec(g_rows, 2, prompt, 1), seg, seg, gate, gate, seg,
                  resident((GM_GROUPS, GM_CHUNK, GM_CHUNK)), resident((GM_CHUNK, GM_WIDTH)),
                  resident((GM_WIDTH, D_MODEL)), resident((ATT_WIDTH, D_MODEL)), resident((D_MODEL, D_MODEL))],
        out_specs=row3,
        scratch_shapes=[pltpu.VMEM((tm, GM_WIDTH), BF16)],
        compiler_params=_cparams(1),
        name="merge_prompt" if prompt else "merge_sample",
    )(x3, mod3, u16, vn16, ga16, gb16, att, w_sp16, b_sp, w_pa16, w_pb16, w_o16)


def _ffn_kernel(x_ref, sh_ref, sc_ref, gt_ref, g2_ref, wg_ref, wu_ref, wd_ref, o_ref, h_scr, *, tm):
    j = pl.program_id(1)
    last = pl.num_programs(1) - 1
    acc_ref = o_ref

    def chunk(h):
        g = jnp.dot(h, wg_ref[...], preferred_element_type=F32)
        u = jnp.dot(h, wu_ref[...], preferred_element_type=F32)
        a = (g * _sigmoid(g) * u).astype(BF16)
        return jnp.dot(a, wd_ref[...], preferred_element_type=F32)

    @pl.when(j == 0)
    def _():
        x = x_ref[...]
        ms = jnp.mean(x * x, axis=-1, keepdims=True)
        y = x * lax.rsqrt(ms + EPS) * g2_ref[...]
        h = (y * (1.0 + sc_ref[...]) + sh_ref[...]).reshape(tm, D_MODEL).astype(BF16)
        h_scr[...] = h
        acc_ref[...] = chunk(h).reshape(o_ref.shape)

    @pl.when((j > 0) & (j < last))
    def _():
        acc_ref[...] += chunk(h_scr[...]).reshape(o_ref.shape)

    @pl.when(j == last)
    def _():
        total = acc_ref[...] + chunk(h_scr[...]).reshape(o_ref.shape)
        o_ref[...] = x_ref[...] + gt_ref[...] * total


def _ffn(x3, mod3, g2, w_gu16, w_down16, *, prompt, tm, tf):
    n_rows = x3.shape[0] * V7X_SUBLANES
    g_rows = tm // V7X_SUBLANES
    n_chunks = D_FF // tf
    row3 = pl.BlockSpec((g_rows, V7X_SUBLANES, D_MODEL), lambda i, j: (i, 0, 0))
    return pl.pallas_call(
        functools.partial(_ffn_kernel, tm=tm),
        out_shape=jax.ShapeDtypeStruct(x3.shape, F32),
        grid=(n_rows // tm, n_chunks),
        in_specs=[row3, _mod_spec(g_rows, 3, prompt, 2), _mod_spec(g_rows, 4, prompt, 2),
                  _mod_spec(g_rows, 5, prompt, 2),
                  pl.BlockSpec((1, 1, D_MODEL), lambda i, j: (0, 0, 0)),
                  pl.BlockSpec((D_MODEL, tf), lambda i, j: (0, j)),
                  pl.BlockSpec((D_MODEL, tf), lambda i, j: (0, n_chunks + j)),
                  pl.BlockSpec((tf, D_MODEL), lambda i, j: (j, 0))],
        out_specs=row3,
        scratch_shapes=[pltpu.VMEM((tm, D_MODEL), BF16)],
        compiler_params=_cparams(2),
        name="ffn_prompt" if prompt else "ffn_sample",
    )(x3, mod3, mod3, mod3, g2.reshape(1, 1, D_MODEL), w_gu16, w_gu16, w_down16)


def _gm_spatial_weights(gm_ws, gm_bs, dec_seq):
    tril = jnp.tril(jnp.ones((GM_CHUNK, GM_CHUNK), dtype=bool))
    w = jnp.where(tril[None], gm_ws, jnp.zeros_like(gm_ws))
    w_prompt = w.astype(BF16)
    b_prompt = jnp.repeat(gm_bs.T, GM_WIDTH // GM_GROUPS, axis=1)
    reps = GM_CHUNK // dec_seq
    eye = jnp.eye(reps, dtype=F32)
    w_small = w[:, :dec_seq, :dec_seq]
    w_sample = jnp.einsum('ab,gts->gatbs', eye, w_small).reshape(GM_GROUPS, GM_CHUNK, GM_CHUNK).astype(BF16)
    b_sample = jnp.tile(jnp.repeat(gm_bs[:, :dec_seq].T, GM_WIDTH // GM_GROUPS, axis=1), (reps, 1))
    return w_prompt, b_prompt, w_sample, b_sample


def kernel(x_prompt, x_sample, c_prompt, c_sample, cache_k, cache_v, page_table, w_ada, b_ada, norm1_g, w_in,
           q_norm_g, k_norm_g, gm_ln_g, gm_ln_b, gm_ws, gm_bs, w_pa, w_pb, w_o, norm2_g, w_gu, w_down,
           rel_bias):
    depth = w_in.shape[0]
    assert depth == 1, "single-layer step"
    n_seq, dec_seq, _ = x_sample.shape
    batch, seq, _ = x_prompt.shape
    assert batch == 1 and n_seq == _PROMPT_MOD_ROW and dec_seq == V7X_SUBLANES

    pad_rows = V7X_SUBLANES - 1
    c_all = jnp.concatenate([c_sample, c_prompt, jnp.zeros((pad_rows, D_MODEL), F32)], axis=0)
    mod = _modulation(c_all, w_ada[0], b_ada[0])
    mod3 = mod.reshape(mod.shape[0], 1, 6 * D_MODEL)

    w_in16 = w_in[0].astype(BF16)
    w_pa16 = w_pa[0].astype(BF16)
    w_pb16 = w_pb[0].astype(BF16)
    w_o16 = w_o[0].astype(BF16)
    w_gu16 = w_gu[0].astype(BF16)
    w_down16 = w_down[0].astype(BF16)
    wsp_p, bsp_p, wsp_s, bsp_s = _gm_spatial_weights(gm_ws[0], gm_bs[0], dec_seq)

    xp3 = x_prompt.reshape(seq // V7X_SUBLANES, V7X_SUBLANES, D_MODEL)
    xs3 = x_sample

    proj = functools.partial(_inproj, g1=norm1_g[0], w_in16=w_in16, qg=q_norm_g[0], kg=k_norm_g[0],
                             lng=gm_ln_g[0], lnb=gm_ln_b[0])
    tm_prompt = 512
    qT32, k16, vT16, k_out, v_out, u16, vn16, ga16, gb16, kbar_tiles = proj(xp3, mod3, prompt=True, tm=tm_prompt)
    per_tile = tm_prompt // MOBA_BLOCK
    kbar = kbar_tiles.reshape(-1, V7X_SUBLANES, ATT_WIDTH)[:, :per_tile].reshape(-1, ATT_WIDTH)
    sq32, sk32, sv32, sk_out, sv_out, su16, svn16, svn32, sga16, sgb16 = proj(xs3, mod3, prompt=False, tm=256)

    att_p = _attn_prompt(qT32, k16, vT16, kbar, rel_bias)
    tok3 = lambda a: a.reshape(n_seq, dec_seq, ATT_WIDTH)
    att_s = _attn_sample(tok3(sq32), tok3(sk32), tok3(sv32), cache_k[0], cache_v[0], page_table, rel_bias)
    att_s = att_s.reshape(n_seq * dec_seq, ATT_WIDTH)

    mrg = functools.partial(_merge, w_pa16=w_pa16, w_pb16=w_pb16, w_o16=w_o16, tm=256)
    x1p = mrg(xp3, mod3, u16, vn16, ga16, gb16, att_p, wsp_p, bsp_p, prompt=True)
    x1s = mrg(xs3, mod3, su16, svn16, sga16, sgb16, att_s, wsp_s, bsp_s, prompt=False)

    ffn = functools.partial(_ffn, g2=norm2_g[0], w_gu16=w_gu16, w_down16=w_down16, tf=512)
    y_p = ffn(x1p, mod3, prompt=True, tm=1024).reshape(batch, seq, D_MODEL)
    y_s = ffn(x1s, mod3, prompt=False, tm=1024)

    kv_p = lambda a: a.reshape(1, batch, seq, N_HEADS, HEAD_DIM)
    kv_s = lambda a: a.reshape(1, n_seq, dec_seq, N_HEADS, HEAD_DIM)
    return (y_p, y_s, kv_p(k_out), kv_p(v_out), kv_s(sk_out), kv_s(sv_out),
            svn32.reshape(1, n_seq, dec_seq, GM_WIDTH))
```

```python
import functools
import math

import jax
import jax.numpy as jnp
from jax import lax
from jax.experimental import pallas as pl
from jax.experimental.pallas import tpu as pltpu

F32 = jnp.float32
BF16 = jnp.bfloat16

D_MODEL = 2048
N_HEADS = 8
HEAD_DIM = 128
ATT_WIDTH = N_HEADS * HEAD_DIM
GM_WIDTH = 1024
GM_GROUPS = 8
GM_CHUNK = 128
MOBA_BLOCK = 256
MOBA_TOPK = 3
NUM_BUCKETS = 32
MAX_DISTANCE = 128
PAGE_SIZE = 128
D_FF = 5632
EPS = 1e-6
SCALE = HEAD_DIM ** -0.5
LOG2E = math.log2(math.e)

V7X_LANES = 128
V7X_SUBLANES = 8
V7X_VMEM_BYTES = 64 * 1024 * 1024
VMEM_LIMIT = V7X_VMEM_BYTES - 8 * 1024 * 1024

NEG = -1e30
SEG = 1024

NT_DIMS = (((1,), (1,)), ((), ()))


def _cparams(n_axes, vmem=VMEM_LIMIT):
    return pltpu.CompilerParams(dimension_semantics=("arbitrary",) * n_axes, vmem_limit_bytes=vmem)


def _sigmoid(x):
    return 1.0 / (1.0 + jnp.exp(-x))


def _gelu_tanh(x):
    c = math.sqrt(2.0 / math.pi)
    return x * (0.5 * (1.0 + jnp.tanh(c * (x + 0.044715 * (x * x * x)))))


def _t5_bucket(rel):
    n = jnp.maximum(rel, 0)
    max_exact = NUM_BUCKETS // 2
    nf = jnp.maximum(n, 1).astype(F32)
    large = max_exact + (jnp.log(nf / max_exact) / math.log(MAX_DISTANCE / max_exact)
                         * (NUM_BUCKETS - max_exact)).astype(jnp.int32)
    large = jnp.minimum(large, NUM_BUCKETS - 1)
    return jnp.where(n < max_exact, n, large)


def _mod_kernel(c_ref, w_ref, b_ref, o_ref):
    c = c_ref[...]
    s = (c * _sigmoid(c)).astype(BF16)
    o_ref[...] = jnp.dot(s, w_ref[...].astype(BF16), preferred_element_type=F32) + b_ref[...]


def _modulation(c_all, w_ada, b_ada):
    rows = c_all.shape[0]
    n_out = w_ada.shape[1]
    tn = 1024
    return pl.pallas_call(
        _mod_kernel,
        out_shape=jax.ShapeDtypeStruct((rows, n_out), F32),
        grid=(n_out // tn,),
        in_specs=[pl.BlockSpec((rows, D_MODEL), lambda j: (0, 0)),
                  pl.BlockSpec((D_MODEL, tn), lambda j: (0, j)),
                  pl.BlockSpec((1, tn), lambda j: (0, j))],
        out_specs=pl.BlockSpec((rows, tn), lambda j: (0, j)),
        compiler_params=_cparams(1),
        name="mod",
    )(c_all, w_ada, b_ada.reshape(1, n_out))


_SEG_K, _SEG_Q, _SEG_V, _SEG_U, _SEG_VG, _SEG_GA0, _SEG_GA1, _SEG_GB0, _SEG_GB1 = range(9)


def _inproj_kernel(x_ref, sh_ref, sc_ref, g1_ref, w_ref, qg_ref, kg_ref, lng_ref, lnb_ref,
                   *rest, tm, prompt):
    if prompt:
        qT32, k16, vT16, k_out, v_out, u16, vn16, ga16, gb16, kbar, h_scr = rest
        q32 = k32 = v32 = None
    else:
        q32, k32, v32, k_out, v_out, u16, vn16, vn32, ga16, gb16, h_scr = rest
        qT32 = k16 = vT16 = None
    j = pl.program_id(1)

    def normed_input():
        x = x_ref[...]
        ms = jnp.mean(x * x, axis=-1, keepdims=True)
        y = x * lax.rsqrt(ms + EPS) * g1_ref[...]
        h = y * (1.0 + sc_ref[...]) + sh_ref[...]
        return h.reshape(tm, D_MODEL).astype(BF16)

    sub = MOBA_BLOCK
    sub_tiles = [slice(r * sub, (r + 1) * sub) for r in range(tm // sub)]

    def matmul(rows, h=None):
        lhs = h_scr[rows, :] if h is None else h[rows]
        return jnp.dot(lhs, w_ref[...], preferred_element_type=F32)

    def store_heads(h_val, y_of_head, dst32, dst16, dst_out, with_means, dst_t=None):
        for r, rows in enumerate(sub_tiles):
            z = matmul(rows, h_val)
            for h in range(N_HEADS):
                sl = slice(h * HEAD_DIM, (h + 1) * HEAD_DIM)
                yh = y_of_head(z[:, sl])
                if dst32 is not None:
                    dst32[rows, sl] = yh
                if dst16 is not None:
                    dst16[rows, sl] = yh.astype(BF16)
                if dst_t is not None:
                    dst_t[sl, rows] = yh.T.astype(dst_t.dtype)
                if dst_out is not None:
                    dst_out[pl.ds(rows.start * N_HEADS + h, sub, stride=N_HEADS), :] = yh
                if with_means:
                    kbar[r:r + 1, sl] = jnp.mean(yh, axis=0, keepdims=True)

    def head_norm(g_ref):
        def f(zh):
            ms = jnp.mean(zh * zh, axis=-1, keepdims=True)
            return zh * lax.rsqrt(ms + EPS) * g_ref[...]
        return f

    def elementwise(f, dsts):
        for rows in sub_tiles:
            y = f(matmul(rows))
            for dst in dsts:
                dst[rows, :] = y.astype(dst.dtype)

    def gelu_layernorm(z):
        a = _gelu_tanh(z)
        mu = jnp.mean(a, axis=-1, keepdims=True)
        ac = a - mu
        y = ac * lax.rsqrt(jnp.mean(ac * ac, axis=-1, keepdims=True) + EPS)
        return y * lng_ref[...] + lnb_ref[...]

    @pl.when(j == _SEG_K)
    def _():
        h = normed_input()
        h_scr[...] = h
        if prompt:
            kbar[...] = jnp.zeros_like(kbar)
        store_heads(h, head_norm(kg_ref), k32, k16, k_out, prompt)

    @pl.when(j == _SEG_Q)
    def _():
        store_heads(None, head_norm(qg_ref), q32, None, None, False, dst_t=qT32)

    @pl.when(j == _SEG_V)
    def _():
        store_heads(None, lambda zh: zh, v32, None, v_out, False, dst_t=vT16)

    @pl.when(j == _SEG_U)
    def _():
        elementwise(_gelu_tanh, [u16])

    @pl.when(j == _SEG_VG)
    def _():
        elementwise(gelu_layernorm, [vn16] if prompt else [vn16, vn32])

    @pl.when((j == _SEG_GA0) | (j == _SEG_GA1))
    def _():
        elementwise(_sigmoid, [ga16])

    @pl.when((j == _SEG_GB0) | (j == _SEG_GB1))
    def _():
        elementwise(_sigmoid, [gb16])


def _mod_spec(g_rows, chunk, prompt, n_grid):
    if n_grid == 2:
        if prompt:
            return pl.BlockSpec((1, 1, D_MODEL), lambda i, j: (_PROMPT_MOD_ROW, 0, chunk))
        return pl.BlockSpec((g_rows, 1, D_MODEL), lambda i, j: (i, 0, chunk))
    if prompt:
        return pl.BlockSpec((1, 1, D_MODEL), lambda i: (_PROMPT_MOD_ROW, 0, chunk))
    return pl.BlockSpec((g_rows, 1, D_MODEL), lambda i: (i, 0, chunk))


_PROMPT_MOD_ROW = 128


def _w_in_col(j):
    return jnp.where(j < 2, 1 - j, j)


def _inproj(x3, mod3, g1, w_in16, qg, kg, lng, lnb, *, prompt, tm):
    n_rows = x3.shape[0] * V7X_SUBLANES
    g_rows = tm // V7X_SUBLANES
    grid = (n_rows // tm, 9)
    row_blk = lambda i, j: (i, 0)
    f32_seg = jax.ShapeDtypeStruct((n_rows, SEG), F32)
    b16_seg = jax.ShapeDtypeStruct((n_rows, SEG), BF16)
    b16_gate = jax.ShapeDtypeStruct((n_rows, D_MODEL), BF16)
    seg_spec = pl.BlockSpec((tm, SEG), row_blk)
    kv_out = jax.ShapeDtypeStruct((n_rows * N_HEADS, HEAD_DIM), F32)
    kv_spec = pl.BlockSpec((tm * N_HEADS, HEAD_DIM), row_blk)
    ga_spec = pl.BlockSpec((tm, SEG), lambda i, j: (i, jnp.clip(j - _SEG_GA0, 0, 1)))
    gb_spec = pl.BlockSpec((tm, SEG), lambda i, j: (i, jnp.clip(j - _SEG_GB0, 0, 1)))
    if prompt:
        assert tm % MOBA_BLOCK == 0 and tm // MOBA_BLOCK <= V7X_SUBLANES
        t_spec = pl.BlockSpec((SEG, tm), lambda i, j: (0, i))
        out_shape = [jax.ShapeDtypeStruct((SEG, n_rows), F32), b16_seg, jax.ShapeDtypeStruct((SEG, n_rows), BF16),
                     kv_out, kv_out, b16_seg, b16_seg, b16_gate, b16_gate,
                     jax.ShapeDtypeStruct((grid[0] * V7X_SUBLANES, SEG), F32)]
        out_specs = ([t_spec, seg_spec, t_spec] + [kv_spec] * 2 + [seg_spec] * 2
                     + [ga_spec, gb_spec, pl.BlockSpec((V7X_SUBLANES, SEG), row_blk)])
    else:
        out_shape = [f32_seg, f32_seg, f32_seg, kv_out, kv_out, b16_seg, b16_seg, f32_seg, b16_gate, b16_gate]
        out_specs = [seg_spec] * 3 + [kv_spec] * 2 + [seg_spec] * 3 + [ga_spec, gb_spec]
    vec = lambda n: pl.BlockSpec((1, n), lambda i, j: (0, 0))
    return pl.pallas_call(
        functools.partial(_inproj_kernel, tm=tm, prompt=prompt),
        out_shape=out_shape,
        grid=grid,
        in_specs=[pl.BlockSpec((g_rows, V7X_SUBLANES, D_MODEL), lambda i, j: (i, 0, 0)),
                  _mod_spec(g_rows, 0, prompt, 2),
                  _mod_spec(g_rows, 1, prompt, 2),
                  pl.BlockSpec((1, 1, D_MODEL), lambda i, j: (0, 0, 0)),
                  pl.BlockSpec((D_MODEL, SEG), lambda i, j: (0, _w_in_col(j))),
                  vec(HEAD_DIM), vec(HEAD_DIM), vec(GM_WIDTH), vec(GM_WIDTH)],
        out_specs=out_specs,
        scratch_shapes=[pltpu.VMEM((tm, D_MODEL), BF16)],
        compiler_params=_cparams(2),
        name="inproj_prompt" if prompt else "inproj_sample",
    )(x3, mod3, mod3, g1.reshape(1, 1, D_MODEL), w_in16, qg.reshape(1, HEAD_DIM), kg.reshape(1, HEAD_DIM),
      lng.reshape(1, GM_WIDTH), lnb.reshape(1, GM_WIDTH))


def _select_topk(scores, n_valid, axis):
    idx = lax.broadcasted_iota(jnp.int32, scores.shape, axis)
    idx_f = idx.astype(F32)
    cand = idx < n_valid
    sel = jnp.zeros(scores.shape, jnp.bool_)
    for _ in range(MOBA_TOPK):
        s_m = jnp.where(cand, scores, -jnp.inf)
        mx = jnp.max(s_m, axis=axis, keepdims=True)
        first = jnp.min(jnp.where(cand & (s_m == mx), idx_f, 1e9), axis=axis, keepdims=True)
        pick = idx_f == first
        sel = sel | pick
        cand = cand & jnp.logical_not(pick)
    return sel


_FAR_GROUP = 4
_SUM_ROWS = 16
_HEADS_PER_STEP = 8


def _attn_prompt_kernel(rb_ref, qT_ref, k_ref, vT_ref, kbar_ref, e_ref, bko_ref, bkp_ref, o_ref,
                        town, tprev, m_scr, r_scr):
    hp = pl.program_id(0)
    j = pl.program_id(1)
    tq = MOBA_BLOCK
    n_sel_rows = kbar_ref.shape[0]
    heads = range(_HEADS_PER_STEP)
    cols = [slice(i * HEAD_DIM, (i + 1) * HEAD_DIM) for i in heads]

    @pl.when(j == 0)
    def _():
        bo = bko_ref[...]
        bp = bkp_ref[...]
        key = lax.broadcasted_iota(jnp.int32, (tq, tq), 0)
        qry = lax.broadcasted_iota(jnp.int32, (tq, tq), 1)
        for i in heads:
            h = hp * _HEADS_PER_STEP + i
            b_far = rb_ref[h, NUM_BUCKETS - 1]
            to = jnp.zeros((tq, tq), F32)
            tp = jnp.zeros((tq, tq), F32)
            for b in range(NUM_BUCKETS - 1):
                val = (rb_ref[h, b] - b_far) * LOG2E
                to = jnp.where(bo == b, val, to)
                tp = jnp.where(bp == b, val, tp)
            town[i] = jnp.where(key <= qry, to, NEG)
            tprev[i] = tp

    q_sT, q_catT, prev_mask = [], [], []
    for i in heads:
        qT = qT_ref[cols[i], :]
        scoresT = jnp.dot(kbar_ref[:, cols[i]], qT, precision=lax.Precision.HIGHEST,
                          preferred_element_type=F32)
        selT = _select_topk(scoresT, j, axis=0)
        blk = lax.broadcasted_iota(jnp.int32, selT.shape, 0)
        qs = (qT * (SCALE * LOG2E)).astype(BF16)
        aug = jnp.where(selT & (blk < j - 1), 0.0, NEG)
        aug = jnp.concatenate([aug, jnp.full((HEAD_DIM - n_sel_rows, tq), NEG, F32)], axis=0)
        q_sT.append(qs)
        q_catT.append(jnp.concatenate([qs, aug.astype(BF16)], axis=0))
        prev_sel = jnp.max(jnp.where(selT & (blk == j - 1), 1.0, 0.0), axis=0, keepdims=True) > 0.0
        prev_mask.append(jnp.where(prev_sel, 0.0, NEG))

    ones_rows = jnp.ones((_SUM_ROWS, tq), BF16)

    def partial_softmax(logits, where):
        logits = [s.astype(BF16) for s in logits]
        ms = [jnp.max(s, axis=0, keepdims=True) for s in logits]
        ps = [jnp.exp2(s - m) for s, m in zip(logits, ms)]
        rs = []
        for p, (i, start) in zip(ps, where):
            vT_aug = jnp.concatenate([vT_ref[cols[i], pl.ds(start, tq)], ones_rows], axis=0)
            rs.append(jnp.dot(vT_aug, p, preferred_element_type=F32))
        return [(m.astype(F32), r) for m, r in zip(ms, rs)]

    def combine(m_run, r_run, parts):
        m_new = m_run
        for m, _ in parts:
            m_new = jnp.maximum(m_new, m)
        r_new = r_run * jnp.exp2(m_run - m_new)
        for m, r in parts:
            r_new = r_new + r * jnp.exp2(m - m_new)
        return m_new, r_new

    own_start = pl.multiple_of(j * tq, tq)
    prev_start = pl.multiple_of(jnp.maximum(j - 1, 0) * tq, tq)
    logits, where = [], []
    for i in heads:
        logits.append(jnp.dot(k_ref[pl.ds(own_start, tq), cols[i]], q_sT[i], preferred_element_type=F32)
                      + town[i])
        logits.append(jnp.dot(k_ref[pl.ds(prev_start, tq), cols[i]], q_sT[i], preferred_element_type=F32)
                      + (tprev[i] + prev_mask[i]))
        where += [(i, own_start), (i, prev_start)]
    parts = partial_softmax(logits, where)
    for i in heads:
        (m_own, r_own), part_prev = parts[2 * i], parts[2 * i + 1]
        m_scr[i], r_scr[i] = combine(m_own, r_own, [part_prev])

    def far_blocks(first_block, n):
        starts = [pl.multiple_of((first_block + b) * tq, tq) for b in range(n)]
        logits, where = [], []
        for i in heads:
            for start in starts:
                k_cat = jnp.concatenate([k_ref[pl.ds(start, tq), cols[i]], e_ref[pl.ds(start, tq), :]], axis=1)
                logits.append(jnp.dot(k_cat, q_catT[i], preferred_element_type=F32))
                where.append((i, start))
        parts = partial_softmax(logits, where)
        for i in heads:
            m_scr[i], r_scr[i] = combine(m_scr[i], r_scr[i], parts[i * n:(i + 1) * n])

    def far_body(g, carry):
        far_blocks(g * _FAR_GROUP, _FAR_GROUP)
        return carry

    n_far = jnp.maximum(j - 1, 0)
    n_groups = n_far // _FAR_GROUP
    lax.fori_loop(0, n_groups, far_body, 0)
    done = n_groups * _FAR_GROUP
    size = _FAR_GROUP // 2
    while size >= 1:
        @pl.when(((n_far - done) & size) != 0)
        def _(done=done, size=size):
            far_blocks(done, size)
        done = done + ((n_far - n_groups * _FAR_GROUP) & size)
        size //= 2

    for i in heads:
        r = r_scr[i]
        o_ref[:, cols[i]] = (r[:HEAD_DIM] / r[HEAD_DIM:HEAD_DIM + 1]).T.astype(o_ref.dtype)


def _attn_prompt(qT32, k16, vT16, kbar, rel_bias):
    seq = k16.shape[0]
    tq = MOBA_BLOCK
    n_blocks = seq // tq
    assert n_blocks % _FAR_GROUP == 0 and kbar.shape[0] == n_blocks
    pos = jnp.arange(seq, dtype=jnp.int32)
    onehot = (pos[:, None] // tq == jnp.arange(HEAD_DIM, dtype=jnp.int32)[None, :]).astype(BF16)
    r = jnp.arange(tq, dtype=jnp.int32)
    bucket_own = _t5_bucket(r[None, :] - r[:, None])
    bucket_prev = _t5_bucket(tq + r[None, :] - r[:, None])
    const = lambda h, j: (0, 0)
    hps = _HEADS_PER_STEP
    width = hps * HEAD_DIM
    return pl.pallas_call(
        _attn_prompt_kernel,
        out_shape=jax.ShapeDtypeStruct((seq, ATT_WIDTH), BF16),
        grid=(N_HEADS // hps, n_blocks),
        in_specs=[pl.BlockSpec(memory_space=pltpu.SMEM),
                  pl.BlockSpec((width, tq), lambda h, j: (h, j)),
                  pl.BlockSpec((seq, width), lambda h, j: (0, h)),
                  pl.BlockSpec((width, seq), lambda h, j: (h, 0)),
                  pl.BlockSpec((n_blocks, width), lambda h, j: (0, h)),
                  pl.BlockSpec((seq, HEAD_DIM), const),
                  pl.BlockSpec((tq, tq), const),
                  pl.BlockSpec((tq, tq), const)],
        out_specs=pl.BlockSpec((tq, width), lambda h, j: (j, h)),
        scratch_shapes=[pltpu.VMEM((hps, tq, tq), F32), pltpu.VMEM((hps, tq, tq), F32),
                        pltpu.VMEM((hps, 1, tq), F32), pltpu.VMEM((hps, HEAD_DIM + _SUM_ROWS, tq), F32)],
        compiler_params=_cparams(2),
        name="attn_prompt",
    )(rel_bias.T, qT32, k16, vT16, kbar, onehot, bucket_own, bucket_prev)


_SAMPLE_GROUP = 2


def _attn_sample_kernel(pt_ref, q_ref, kn_ref, vn_ref, rbrows_ref, bkp_ref, *rest, n_pages, dec_seq):
    k_pages = rest[:n_pages]
    v_pages = rest[n_pages:2 * n_pages]
    o_ref, k16, v16, kbar_scr, bprev_scr, bown_scr = rest[2 * n_pages:]
    b = pl.program_id(0)
    past = n_pages * PAGE_SIZE
    n_past_blocks = past // MOBA_BLOCK
    rows = N_HEADS * dec_seq
    pages_per_block = MOBA_BLOCK // PAGE_SIZE

    @pl.when(b == 0)
    def _():
        kbar_scr[...] = jnp.zeros_like(kbar_scr)
        rb = rbrows_ref[...]
        bp = bkp_ref[...]
        tp = jnp.zeros((rows, MOBA_BLOCK), F32)
        for bb in range(NUM_BUCKETS - 1):
            tp = jnp.where(bp == bb, rb[:, bb:bb + 1], tp)
        bprev_scr[...] = tp
        t_row = lax.broadcasted_iota(jnp.int32, (rows, V7X_LANES), 0) & (dec_seq - 1)
        c_col = lax.broadcasted_iota(jnp.int32, (rows, V7X_LANES), 1)
        rel = t_row - c_col
        to = jnp.zeros((rows, V7X_LANES), F32)
        for bb in range(dec_seq):
            to = jnp.where(rel == bb, rb[:, bb:bb + 1], to)
        bown_scr[...] = jnp.where(rel >= 0, to, NEG)

    q = q_ref[...]
    q_rep = jnp.concatenate([q] * N_HEADS, axis=0)
    row_head = lax.broadcasted_iota(jnp.int32, (rows, ATT_WIDTH), 0) >> int(math.log2(dec_seq))
    col_head = lax.broadcasted_iota(jnp.int32, (rows, ATT_WIDTH), 1) >> int(math.log2(HEAD_DIM))
    q_bd = jnp.where(row_head == col_head, q_rep, 0.0)
    q_s = (q_bd * SCALE).astype(BF16)

    def to_head_lanes(pages, dst, n, sums_ref):
        for h in range(N_HEADS):
            cols = slice(h * HEAD_DIM, (h + 1) * HEAD_DIM)
            total = None
            for p in range(n * pages_per_block, (n + 1) * pages_per_block):
                xh = pages[p][pl.ds(h, PAGE_SIZE, stride=N_HEADS), :]
                dst[p * PAGE_SIZE:(p + 1) * PAGE_SIZE, cols] = xh.astype(BF16)
                if sums_ref is not None:
                    cs = jnp.sum(xh, axis=0, keepdims=True)
                    total = cs if total is None else total + cs
            if sums_ref is not None:
                sums_ref[n:n + 1, cols] = total * (1.0 / MOBA_BLOCK)

    def new_rows(ref):
        pad = jnp.zeros((V7X_LANES - dec_seq, ATT_WIDTH), F32)
        return jnp.concatenate([ref[...], pad], axis=0).astype(BF16)

    groups = [slice(g * MOBA_BLOCK, (g + _SAMPLE_GROUP) * MOBA_BLOCK)
              for g in range(0, n_past_blocks, _SAMPLE_GROUP)]
    new_keys = slice(past, past + V7X_LANES)
    raw = []
    for keys in groups:
        for n in range(keys.start // MOBA_BLOCK, keys.stop // MOBA_BLOCK):
            to_head_lanes(k_pages, k16, n, kbar_scr)
        raw.append(lax.dot_general(q_s, k16[keys, :], NT_DIMS, preferred_element_type=F32))
    k16[new_keys, :] = new_rows(kn_ref)
    raw.append(lax.dot_general(q_s, k16[new_keys, :], NT_DIMS, preferred_element_type=F32))

    assert n_past_blocks == V7X_SUBLANES
    q_pad = jnp.concatenate([q_bd, jnp.zeros((V7X_LANES - rows, ATT_WIDTH), F32)], axis=0)
    scores_t = lax.dot_general(kbar_scr[0:n_past_blocks, :], q_pad, NT_DIMS, precision=lax.Precision.HIGHEST,
                               preferred_element_type=F32)
    blk = lax.broadcasted_iota(jnp.int32, scores_t.shape, 0)
    rank = jnp.zeros(scores_t.shape, F32)
    for k in range(1, n_past_blocks):
        other = pltpu.roll(scores_t, k, 0)
        ge = jnp.where(other >= scores_t, 1.0, 0.0)
        gt = jnp.where(other > scores_t, 1.0, 0.0)
        rank = rank + jnp.where(blk >= k, ge, gt)
    sel_t = jnp.where(rank < MOBA_TOPK, 1.0, 0.0)
    eye = jnp.where(lax.broadcasted_iota(jnp.int32, (rows, V7X_LANES), 0)
                    == lax.broadcasted_iota(jnp.int32, (rows, V7X_LANES), 1), 1.0, 0.0).astype(BF16)
    sel_pad = jnp.concatenate([sel_t, jnp.zeros((V7X_LANES - n_past_blocks, V7X_LANES), F32)], axis=0)
    sel = lax.dot_general(eye, sel_pad.astype(BF16), NT_DIMS, preferred_element_type=F32)
    mask_add = jnp.where(sel > 0.5, 0.0, NEG)

    pieces = []
    for keys, piece in zip(groups, raw):
        cols = []
        for i, n in enumerate(range(keys.start // MOBA_BLOCK, keys.stop // MOBA_BLOCK)):
            blk = piece[:, i * MOBA_BLOCK:(i + 1) * MOBA_BLOCK] + mask_add[:, n:n + 1]
            if n == n_past_blocks - 1:
                blk = blk + bprev_scr[...]
            cols.append(blk)
        pieces.append(jnp.concatenate(cols, axis=1))
    pieces.append(raw[-1] + bown_scr[...])
    m = pieces[0].max(axis=-1, keepdims=True)
    for piece in pieces[1:]:
        m = jnp.maximum(m, piece.max(axis=-1, keepdims=True))
    probs = [jnp.exp(piece - m) for piece in pieces]
    l = sum(p.sum(axis=-1, keepdims=True) for p in probs)

    out = None
    for keys, p in zip(groups + [new_keys], probs):
        if keys is new_keys:
            v16[keys, :] = new_rows(vn_ref)
        else:
            for n in range(keys.start // MOBA_BLOCK, keys.stop // MOBA_BLOCK):
                to_head_lanes(v_pages, v16, n, None)
        part = jnp.dot(p.astype(BF16), v16[keys, :], preferred_element_type=F32)
        out = part if out is None else out + part
    out = out / l
    for h in range(N_HEADS):
        sl = slice(h * HEAD_DIM, (h + 1) * HEAD_DIM)
        o_ref[:, sl] = out[h * dec_seq:(h + 1) * dec_seq, sl]


def _attn_sample(q3, k3, v3, cache_k, cache_v, page_table, rel_bias):
    n_seq, dec_seq, _ = q3.shape
    n_pages = page_table.shape[1]
    past = n_pages * PAGE_SIZE
    rows = N_HEADS * dec_seq
    page_rows = PAGE_SIZE * N_HEADS
    ck = cache_k.reshape(-1, HEAD_DIM)
    cv = cache_v.reshape(-1, HEAD_DIM)
    rb_rel = rel_bias.T - rel_bias[NUM_BUCKETS - 1][:, None]
    rbrows = jnp.repeat(rb_rel, dec_seq, axis=0)
    t = jnp.arange(rows, dtype=jnp.int32) % dec_seq
    c = jnp.arange(MOBA_BLOCK, dtype=jnp.int32)
    bucket_prev = _t5_bucket(MOBA_BLOCK + t[:, None] - c[None, :])
    tok = pl.BlockSpec((None, dec_seq, ATT_WIDTH), lambda b, pt: (b, 0, 0))
    const2 = lambda shape: pl.BlockSpec(shape, lambda b, pt: (0, 0))

    def page_spec(p):
        return pl.BlockSpec((page_rows, HEAD_DIM), lambda b, pt: (pt[b, p], 0))

    grid_spec = pltpu.PrefetchScalarGridSpec(
        num_scalar_prefetch=1,
        grid=(n_seq,),
        in_specs=[tok, tok, tok, const2((rows, NUM_BUCKETS)), const2((rows, MOBA_BLOCK))]
                 + [page_spec(p) for p in range(n_pages)] * 2,
        out_specs=tok,
        scratch_shapes=[pltpu.VMEM((past + V7X_LANES, ATT_WIDTH), BF16),
                        pltpu.VMEM((past + V7X_LANES, ATT_WIDTH), BF16),
                        pltpu.VMEM((V7X_LANES, ATT_WIDTH), F32),
                        pltpu.VMEM((rows, MOBA_BLOCK), F32),
                        pltpu.VMEM((rows, V7X_LANES), F32)],
    )
    return pl.pallas_call(
        functools.partial(_attn_sample_kernel, n_pages=n_pages, dec_seq=dec_seq),
        out_shape=jax.ShapeDtypeStruct((n_seq, dec_seq, ATT_WIDTH), F32),
        grid_spec=grid_spec,
        compiler_params=_cparams(1),
        name="attn_sample",
    )(page_table, q3, k3, v3, rbrows, bucket_prev, *([ck] * n_pages), *([cv] * n_pages))


def _merge_kernel(x_ref, gt_ref, u_ref, vn_ref, ga_ref, gb_ref, att_ref, wsp_ref, bsp_ref,
                  wpa_ref, wpb_ref, wo_ref, o_ref, gm_scr, *, tm):
    for c in range(tm // GM_CHUNK):
        rows = slice(c * GM_CHUNK, (c + 1) * GM_CHUNK)
        for g in range(GM_GROUPS):
            cols = slice(g * GM_CHUNK, (g + 1) * GM_CHUNK)
            s = jnp.dot(wsp_ref[g], vn_ref[rows, cols], preferred_element_type=F32) + bsp_ref[:, cols]
            gm_scr[rows, cols] = (u_ref[rows, cols].astype(F32) * s).astype(BF16)
    a = jnp.dot(gm_scr[...], wpa_ref[...], preferred_element_type=F32)
    b = jnp.dot(att_ref[...].astype(BF16), wpb_ref[...], preferred_element_type=F32)
    merged = ga_ref[...].astype(F32) * a + gb_ref[...].astype(F32) * b
    c_out = jnp.dot(merged.astype(BF16), wo_ref[...], preferred_element_type=F32)
    o_ref[...] = x_ref[...] + gt_ref[...] * c_out.reshape(o_ref.shape)


def _merge(x3, mod3, u16, vn16, ga16, gb16, att, w_sp16, b_sp, w_pa16, w_pb16, w_o16, *, prompt, tm):
    n_rows = x3.shape[0] * V7X_SUBLANES
    g_rows = tm // V7X_SUBLANES
    row3 = pl.BlockSpec((g_rows, V7X_SUBLANES, D_MODEL), lambda i: (i, 0, 0))
    seg = pl.BlockSpec((tm, SEG), lambda i: (i, 0))
    gate = pl.BlockSpec((tm, D_MODEL), lambda i: (i, 0))
    resident = lambda shape: pl.BlockSpec(shape, lambda i: (0,) * len(shape), pipeline_mode=pl.Buffered(1))
    return pl.pallas_call(
        functools.partial(_merge_kernel, tm=tm),
        out_shape=jax.ShapeDtypeStruct(x3.shape, F32),
        grid=(n_rows // tm,),
        in_specs=[row3, _mod_spec(g_rows, 2, prompt, 1), seg, seg, gate, gate, seg,
                  resident((GM_GROUPS, GM_CHUNK, GM_CHUNK)), resident((GM_CHUNK, GM_WIDTH)),
                  resident((GM_WIDTH, D_MODEL)), resident((ATT_WIDTH, D_MODEL)), resident((D_MODEL, D_MODEL))],
        out_specs=row3,
        scratch_shapes=[pltpu.VMEM((tm, GM_WIDTH), BF16)],
        compiler_params=_cparams(1),
        name="merge_prompt" if prompt else "merge_sample",
    )(x3, mod3, u16, vn16, ga16, gb16, att, w_sp16, b_sp, w_pa16, w_pb16, w_o16)


def _ffn_kernel(x_ref, sh_ref, sc_ref, gt_ref, g2_ref, wg_ref, wu_ref, wd_ref, o_ref, h_scr, *, tm):
    j = pl.program_id(1)
    last = pl.num_programs(1) - 1
    acc_ref = o_ref

    def chunk(h):
        g = jnp.dot(h, wg_ref[...], preferred_element_type=F32)
        u = jnp.dot(h, wu_ref[...], preferred_element_type=F32)
        a = (g * _sigmoid(g) * u).astype(BF16)
        return jnp.dot(a, wd_ref[...], preferred_element_type=F32)

    @pl.when(j == 0)
    def _():
        x = x_ref[...]
        ms = jnp.mean(x * x, axis=-1, keepdims=True)
        y = x * lax.rsqrt(ms + EPS) * g2_ref[...]
        h = (y * (1.0 + sc_ref[...]) + sh_ref[...]).reshape(tm, D_MODEL).astype(BF16)
        h_scr[...] = h
        acc_ref[...] = chunk(h).reshape(o_ref.shape)

    @pl.when((j > 0) & (j < last))
    def _():
        acc_ref[...] += chunk(h_scr[...]).reshape(o_ref.shape)

    @pl.when(j == last)
    def _():
        total = acc_ref[...] + chunk(h_scr[...]).reshape(o_ref.shape)
        o_ref[...] = x_ref[...] + gt_ref[...] * total


def _ffn(x3, mod3, g2, w_gu16, w_down16, *, prompt, tm, tf):
    n_rows = x3.shape[0] * V7X_SUBLANES
    g_rows = tm // V7X_SUBLANES
    n_chunks = D_FF // tf
    row3 = pl.BlockSpec((g_rows, V7X_SUBLANES, D_MODEL), lambda i, j: (i, 0, 0))
    return pl.pallas_call(
        functools.partial(_ffn_kernel, tm=tm),
        out_shape=jax.ShapeDtypeStruct(x3.shape, F32),
        grid=(n_rows // tm, n_chunks),
        in_specs=[row3, _mod_spec(g_rows, 3, prompt, 2), _mod_spec(g_rows, 4, prompt, 2),
                  _mod_spec(g_rows, 5, prompt, 2),
                  pl.BlockSpec((1, 1, D_MODEL), lambda i, j: (0, 0, 0)),
                  pl.BlockSpec((D_MODEL, tf), lambda i, j: (0, j)),
                  pl.BlockSpec((D_MODEL, tf), lambda i, j: (0, n_chunks + j)),
                  pl.BlockSpec((tf, D_MODEL), lambda i, j: (j, 0))],
        out_specs=row3,
        scratch_shapes=[pltpu.VMEM((tm, D_MODEL), BF16)],
        compiler_params=_cparams(2),
        name="ffn_prompt" if prompt else "ffn_sample",
    )(x3, mod3, mod3, mod3, g2.reshape(1, 1, D_MODEL), w_gu16, w_gu16, w_down16)


def _gm_spatial_weights(gm_ws, gm_bs, dec_seq):
    tril = jnp.tril(jnp.ones((GM_CHUNK, GM_CHUNK), dtype=bool))
    w = jnp.where(tril[None], gm_ws, jnp.zeros_like(gm_ws))
    w_prompt = w.astype(BF16)
    b_prompt = jnp.repeat(gm_bs.T, GM_WIDTH // GM_GROUPS, axis=1)
    reps = GM_CHUNK // dec_seq
    eye = jnp.eye(reps, dtype=F32)
    w_small = w[:, :dec_seq, :dec_seq]
    w_sample = jnp.einsum('ab,gts->gatbs', eye, w_small).reshape(GM_GROUPS, GM_CHUNK, GM_CHUNK).astype(BF16)
    b_sample = jnp.tile(jnp.repeat(gm_bs[:, :dec_seq].T, GM_WIDTH // GM_GROUPS, axis=1), (reps, 1))
    return w_prompt, b_prompt, w_sample, b_sample


def kernel(x_prompt, x_sample, c_prompt, c_sample, cache_k, cache_v, page_table, w_ada, b_ada, norm1_g, w_in,
           q_norm_g, k_norm_g, gm_ln_g, gm_ln_b, gm_ws, gm_bs, w_pa, w_pb, w_o, norm2_g, w_gu, w_down,
           rel_bias):
    depth = w_in.shape[0]
    assert depth == 1, "single-layer step"
    n_seq, dec_seq, _ = x_sample.shape
    batch, seq, _ = x_prompt.shape
    assert batch == 1 and n_seq == _PROMPT_MOD_ROW and dec_seq == V7X_SUBLANES

    pad_rows = V7X_SUBLANES - 1
    c_all = jnp.concatenate([c_sample, c_prompt, jnp.zeros((pad_rows, D_MODEL), F32)], axis=0)
    mod = _modulation(c_all, w_ada[0], b_ada[0])
    mod3 = mod.reshape(mod.shape[0], 1, 6 * D_MODEL)

    w_in16 = w_in[0].astype(BF16)
    w_pa16 = w_pa[0].astype(BF16)
    w_pb16 = w_pb[0].astype(BF16)
    w_o16 = w_o[0].astype(BF16)
    w_gu16 = w_gu[0].astype(BF16)
    w_down16 = w_down[0].astype(BF16)
    wsp_p, bsp_p, wsp_s, bsp_s = _gm_spatial_weights(gm_ws[0], gm_bs[0], dec_seq)

    xp3 = x_prompt.reshape(seq // V7X_SUBLANES, V7X_SUBLANES, D_MODEL)
    xs3 = x_sample

    proj = functools.partial(_inproj, g1=norm1_g[0], w_in16=w_in16, qg=q_norm_g[0], kg=k_norm_g[0],
                             lng=gm_ln_g[0], lnb=gm_ln_b[0])
    tm_prompt = 512
    qT32, k16, vT16, k_out, v_out, u16, vn16, ga16, gb16, kbar_tiles = proj(xp3, mod3, prompt=True, tm=tm_prompt)
    per_tile = tm_prompt // MOBA_BLOCK
    kbar = kbar_tiles.reshape(-1, V7X_SUBLANES, ATT_WIDTH)[:, :per_tile].reshape(-1, ATT_WIDTH)
    sq32, sk32, sv32, sk_out, sv_out, su16, svn16, svn32, sga16, sgb16 = proj(xs3, mod3, prompt=False, tm=256)

    att_p = _attn_prompt(qT32, k16, vT16, kbar, rel_bias)
    tok3 = lambda a: a.reshape(n_seq, dec_seq, ATT_WIDTH)
    att_s = _attn_sample(tok3(sq32), tok3(sk32), tok3(sv32), cache_k[0], cache_v[0], page_table, rel_bias)
    att_s = att_s.reshape(n_seq * dec_seq, ATT_WIDTH)

    mrg = functools.partial(_merge, w_pa16=w_pa16, w_pb16=w_pb16, w_o16=w_o16, tm=256)
    x1p = mrg(xp3, mod3, u16, vn16, ga16, gb16, att_p, wsp_p, bsp_p, prompt=True)
    x1s = mrg(xs3, mod3, su16, svn16, sga16, sgb16, att_s, wsp_s, bsp_s, prompt=False)

    ffn = functools.partial(_ffn, g2=norm2_g[0], w_gu16=w_gu16, w_down16=w_down16, tf=512)
    y_p = ffn(x1p, mod3, prompt=True, tm=1024).reshape(batch, seq, D_MODEL)
    y_s = ffn(x1s, mod3, prompt=False, tm=1024)

    kv_p = lambda a: a.reshape(1, batch, seq, N_HEADS, HEAD_DIM)
    kv_s = lambda a: a.reshape(1, n_seq, dec_seq, N_HEADS, HEAD_DIM)
    return (y_p, y_s, kv_p(k_out), kv_p(v_out), kv_s(sk_out), kv_s(sv_out),
            svn32.reshape(1, n_seq, dec_seq, GM_WIDTH))
```

```python
import functools
import math

import jax
import jax.numpy as jnp
from jax import lax
from jax.experimental import pallas as pl
from jax.experimental.pallas import tpu as pltpu

F32 = jnp.float32
BF16 = jnp.bfloat16

D_MODEL = 2048
N_HEADS = 8
HEAD_DIM = 128
ATT_WIDTH = N_HEADS * HEAD_DIM
GM_WIDTH = 1024
GM_GROUPS = 8
GM_CHUNK = 128
MOBA_BLOCK = 256
MOBA_TOPK = 3
NUM_BUCKETS = 32
MAX_DISTANCE = 128
PAGE_SIZE = 128
D_FF = 5632
EPS = 1e-6
SCALE = HEAD_DIM ** -0.5
LOG2E = math.log2(math.e)

V7X_LANES = 128
V7X_SUBLANES = 8
V7X_VMEM_BYTES = 64 * 1024 * 1024
VMEM_LIMIT = V7X_VMEM_BYTES - 8 * 1024 * 1024

NEG = -1e30
SEG = 1024

NT_DIMS = (((1,), (1,)), ((), ()))


def _cparams(n_axes, vmem=VMEM_LIMIT):
    return pltpu.CompilerParams(dimension_semantics=("arbitrary",) * n_axes, vmem_limit_bytes=vmem)


def _sigmoid(x):
    return 1.0 / (1.0 + jnp.exp(-x))


def _gelu_tanh(x):
    c = math.sqrt(2.0 / math.pi)
    return x * (0.5 * (1.0 + jnp.tanh(c * (x + 0.044715 * (x * x * x)))))


def _t5_bucket(rel):
    n = jnp.maximum(rel, 0)
    max_exact = NUM_BUCKETS // 2
    nf = jnp.maximum(n, 1).astype(F32)
    large = max_exact + (jnp.log(nf / max_exact) / math.log(MAX_DISTANCE / max_exact)
                         * (NUM_BUCKETS - max_exact)).astype(jnp.int32)
    large = jnp.minimum(large, NUM_BUCKETS - 1)
    return jnp.where(n < max_exact, n, large)


def _mod_kernel(c_ref, w_ref, b_ref, o_ref):
    c = c_ref[...]
    s = (c * _sigmoid(c)).astype(BF16)
    o_ref[...] = jnp.dot(s, w_ref[...].astype(BF16), preferred_element_type=F32) + b_ref[...]


def _modulation(c_all, w_ada, b_ada):
    rows = c_all.shape[0]
    n_out = w_ada.shape[1]
    tn = 1024
    return pl.pallas_call(
        _mod_kernel,
        out_shape=jax.ShapeDtypeStruct((rows, n_out), F32),
        grid=(n_out // tn,),
        in_specs=[pl.BlockSpec((rows, D_MODEL), lambda j: (0, 0)),
                  pl.BlockSpec((D_MODEL, tn), lambda j: (0, j)),
                  pl.BlockSpec((1, tn), lambda j: (0, j))],
        out_specs=pl.BlockSpec((rows, tn), lambda j: (0, j)),
        compiler_params=_cparams(1),
        name="mod",
    )(c_all, w_ada, b_ada.reshape(1, n_out))


_SEG_K, _SEG_Q, _SEG_V, _SEG_U, _SEG_VG, _SEG_GA0, _SEG_GA1, _SEG_GB0, _SEG_GB1 = range(9)


def _inproj_kernel(x_ref, sh_ref, sc_ref, g1_ref, w_ref, qg_ref, kg_ref, lng_ref, lnb_ref,
                   *rest, tm, prompt):
    if prompt:
        qT32, k16, vT16, k_out, v_out, u16, vn16, ga16, gb16, kbar, h_scr = rest
        q32 = k32 = v32 = None
    else:
        q32, k32, v32, k_out, v_out, u16, vn16, vn32, ga16, gb16, h_scr = rest
        qT32 = k16 = vT16 = None
    j = pl.program_id(1)

    def normed_input():
        x = x_ref[...]
        ms = jnp.mean(x * x, axis=-1, keepdims=True)
        y = x * lax.rsqrt(ms + EPS) * g1_ref[...]
        h = y * (1.0 + sc_ref[...]) + sh_ref[...]
        return h.reshape(tm, D_MODEL).astype(BF16)

    sub = MOBA_BLOCK
    sub_tiles = [slice(r * sub, (r + 1) * sub) for r in range(tm // sub)]

    def matmul(rows, h=None):
        lhs = h_scr[rows, :] if h is None else h[rows]
        return jnp.dot(lhs, w_ref[...], preferred_element_type=F32)

    def store_heads(h_val, y_of_head, dst32, dst16, dst_out, with_means, dst_t=None):
        for r, rows in enumerate(sub_tiles):
            z = matmul(rows, h_val)
            for h in range(N_HEADS):
                sl = slice(h * HEAD_DIM, (h + 1) * HEAD_DIM)
                yh = y_of_head(z[:, sl])
                if dst32 is not None:
                    dst32[rows, sl] = yh
                if dst16 is not None:
                    dst16[rows, sl] = yh.astype(BF16)
                if dst_t is not None:
                    dst_t[sl, rows] = yh.T.astype(dst_t.dtype)
                if dst_out is not None:
                    dst_out[pl.ds(rows.start * N_HEADS + h, sub, stride=N_HEADS), :] = yh
                if with_means:
                    kbar[r:r + 1, sl] = jnp.mean(yh, axis=0, keepdims=True)

    def head_norm(g_ref):
        def f(zh):
            ms = jnp.mean(zh * zh, axis=-1, keepdims=True)
            return zh * lax.rsqrt(ms + EPS) * g_ref[...]
        return f

    def elementwise(f, dsts):
        for rows in sub_tiles:
            y = f(matmul(rows))
            for dst in dsts:
                dst[rows, :] = y.astype(dst.dtype)

    def gelu_layernorm(z):
        a = _gelu_tanh(z)
        mu = jnp.mean(a, axis=-1, keepdims=True)
        ac = a - mu
        y = ac * lax.rsqrt(jnp.mean(ac * ac, axis=-1, keepdims=True) + EPS)
        return y * lng_ref[...] + lnb_ref[...]

    @pl.when(j == _SEG_K)
    def _():
        h = normed_input()
        h_scr[...] = h
        if prompt:
            kbar[...] = jnp.zeros_like(kbar)
        store_heads(h, head_norm(kg_ref), k32, k16, k_out, prompt)

    @pl.when(j == _SEG_Q)
    def _():
        store_heads(None, head_norm(qg_ref), q32, None, None, False, dst_t=qT32)

    @pl.when(j == _SEG_V)
    def _():
        store_heads(None, lambda zh: zh, v32, None, v_out, False, dst_t=vT16)

    @pl.when(j == _SEG_U)
    def _():
        elementwise(_gelu_tanh, [u16])

    @pl.when(j == _SEG_VG)
    def _():
        elementwise(gelu_layernorm, [vn16] if prompt else [vn16, vn32])

    @pl.when((j == _SEG_GA0) | (j == _SEG_GA1))
    def _():
        elementwise(_sigmoid, [ga16])

    @pl.when((j == _SEG_GB0) | (j == _SEG_GB1))
    def _():
        elementwise(_sigmoid, [gb16])


def _mod_spec(g_rows, chunk, prompt, n_grid):
    if n_grid == 2:
        if prompt:
            return pl.BlockSpec((1, 1, D_MODEL), lambda i, j: (_PROMPT_MOD_ROW, 0, chunk))
        return pl.BlockSpec((g_rows, 1, D_MODEL), lambda i, j: (i, 0, chunk))
    if prompt:
        return pl.BlockSpec((1, 1, D_MODEL), lambda i: (_PROMPT_MOD_ROW, 0, chunk))
    return pl.BlockSpec((g_rows, 1, D_MODEL), lambda i: (i, 0, chunk))


_PROMPT_MOD_ROW = 128


def _w_in_col(j):
    return jnp.where(j < 2, 1 - j, j)


def _inproj(x3, mod3, g1, w_in16, qg, kg, lng, lnb, *, prompt, tm):
    n_rows = x3.shape[0] * V7X_SUBLANES
    g_rows = tm // V7X_SUBLANES
    grid = (n_rows // tm, 9)
    row_blk = lambda i, j: (i, 0)
    f32_seg = jax.ShapeDtypeStruct((n_rows, SEG), F32)
    b16_seg = jax.ShapeDtypeStruct((n_rows, SEG), BF16)
    b16_gate = jax.ShapeDtypeStruct((n_rows, D_MODEL), BF16)
    kv_out = jax.ShapeDtypeStruct((n_rows * N_HEADS, HEAD_DIM), F32)

    def row_after(seg):
        return lambda i, j: jnp.minimum(i + (j > seg).astype(jnp.int32), grid[0] - 1)

    def seg_spec(seg):
        return pl.BlockSpec((tm, SEG), lambda i, j: (row_after(seg)(i, j), 0))

    def kv_spec(seg):
        return pl.BlockSpec((tm * N_HEADS, HEAD_DIM), lambda i, j: (row_after(seg)(i, j), 0))
    ga_spec = pl.BlockSpec((tm, SEG), lambda i, j: (i, jnp.clip(j - _SEG_GA0, 0, 1)))
    gb_spec = pl.BlockSpec((tm, SEG), lambda i, j: (i, jnp.clip(j - _SEG_GB0, 0, 1)))
    if prompt:
        assert tm % MOBA_BLOCK == 0 and tm // MOBA_BLOCK <= V7X_SUBLANES
        def t_spec(seg):
            return pl.BlockSpec((SEG, tm), lambda i, j: (0, row_after(seg)(i, j)))
        out_shape = [jax.ShapeDtypeStruct((SEG, n_rows), F32), b16_seg, jax.ShapeDtypeStruct((SEG, n_rows), BF16),
                     kv_out, kv_out, b16_seg, b16_seg, b16_gate, b16_gate,
                     jax.ShapeDtypeStruct((grid[0] * V7X_SUBLANES, SEG), F32)]
        out_specs = [t_spec(_SEG_Q), seg_spec(_SEG_K), t_spec(_SEG_V), kv_spec(_SEG_K), kv_spec(_SEG_V),
                     seg_spec(_SEG_U), seg_spec(_SEG_VG), ga_spec, gb_spec,
                     pl.BlockSpec((V7X_SUBLANES, SEG), row_blk)]
    else:
        out_shape = [f32_seg, f32_seg, f32_seg, kv_out, kv_out, b16_seg, b16_seg, f32_seg, b16_gate, b16_gate]
        out_specs = [seg_spec(_SEG_Q), seg_spec(_SEG_K), seg_spec(_SEG_V), kv_spec(_SEG_K), kv_spec(_SEG_V),
                     seg_spec(_SEG_U), seg_spec(_SEG_VG), seg_spec(_SEG_VG), ga_spec, gb_spec]
    vec = lambda n: pl.BlockSpec((1, n), lambda i, j: (0, 0))
    return pl.pallas_call(
        functools.partial(_inproj_kernel, tm=tm, prompt=prompt),
        out_shape=out_shape,
        grid=grid,
        in_specs=[pl.BlockSpec((g_rows, V7X_SUBLANES, D_MODEL), lambda i, j: (i, 0, 0)),
                  _mod_spec(g_rows, 0, prompt, 2),
                  _mod_spec(g_rows, 1, prompt, 2),
                  pl.BlockSpec((1, 1, D_MODEL), lambda i, j: (0, 0, 0)),
                  pl.BlockSpec((D_MODEL, SEG), lambda i, j: (0, _w_in_col(j))),
                  vec(HEAD_DIM), vec(HEAD_DIM), vec(GM_WIDTH), vec(GM_WIDTH)],
        out_specs=out_specs,
        scratch_shapes=[pltpu.VMEM((tm, D_MODEL), BF16)],
        compiler_params=_cparams(2),
        name="inproj_prompt" if prompt else "inproj_sample",
    )(x3, mod3, mod3, g1.reshape(1, 1, D_MODEL), w_in16, qg.reshape(1, HEAD_DIM), kg.reshape(1, HEAD_DIM),
      lng.reshape(1, GM_WIDTH), lnb.reshape(1, GM_WIDTH))


def _select_topk(scores, n_valid, axis):
    idx = lax.broadcasted_iota(jnp.int32, scores.shape, axis)
    idx_f = idx.astype(F32)
    cand = idx < n_valid
    sel = jnp.zeros(scores.shape, jnp.bool_)
    for _ in range(MOBA_TOPK):
        s_m = jnp.where(cand, scores, -jnp.inf)
        mx = jnp.max(s_m, axis=axis, keepdims=True)
        first = jnp.min(jnp.where(cand & (s_m == mx), idx_f, 1e9), axis=axis, keepdims=True)
        pick = idx_f == first
        sel = sel | pick
        cand = cand & jnp.logical_not(pick)
    return sel


_FAR_GROUP = 4
_SUM_ROWS = 16
_HEADS_PER_STEP = 8


def _attn_prompt_kernel(rb_ref, qT_ref, k_ref, vT_ref, kbar_ref, e_ref, bko_ref, bkp_ref, o_ref,
                        town, tprev, m_scr, r_scr):
    hp = pl.program_id(0)
    j = pl.program_id(1)
    tq = MOBA_BLOCK
    n_sel_rows = kbar_ref.shape[0]
    heads = range(_HEADS_PER_STEP)
    cols = [slice(i * HEAD_DIM, (i + 1) * HEAD_DIM) for i in heads]

    @pl.when(j == 0)
    def _():
        bo = bko_ref[...]
        bp = bkp_ref[...]
        key = lax.broadcasted_iota(jnp.int32, (tq, tq), 0)
        qry = lax.broadcasted_iota(jnp.int32, (tq, tq), 1)
        for i in heads:
            h = hp * _HEADS_PER_STEP + i
            b_far = rb_ref[h, NUM_BUCKETS - 1]
            to = jnp.zeros((tq, tq), F32)
            tp = jnp.zeros((tq, tq), F32)
            for b in range(NUM_BUCKETS - 1):
                val = (rb_ref[h, b] - b_far) * LOG2E
                to = jnp.where(bo == b, val, to)
                tp = jnp.where(bp == b, val, tp)
            town[i] = jnp.where(key <= qry, to, NEG)
            tprev[i] = tp

    q_sT, q_catT, prev_mask = [], [], []
    for i in heads:
        qT = qT_ref[cols[i], :]
        scoresT = jnp.dot(kbar_ref[:, cols[i]], qT, precision=lax.Precision.HIGHEST,
                          preferred_element_type=F32)
        selT = _select_topk(scoresT, j, axis=0)
        blk = lax.broadcasted_iota(jnp.int32, selT.shape, 0)
        qs = (qT * (SCALE * LOG2E)).astype(BF16)
        aug = jnp.where(selT & (blk < j - 1), 0.0, NEG)
        aug = jnp.concatenate([aug, jnp.full((HEAD_DIM - n_sel_rows, tq), NEG, F32)], axis=0)
        q_sT.append(qs)
        q_catT.append(jnp.concatenate([qs, aug.astype(BF16)], axis=0))
        prev_sel = jnp.max(jnp.where(selT & (blk == j - 1), 1.0, 0.0), axis=0, keepdims=True) > 0.0
        prev_mask.append(jnp.where(prev_sel, 0.0, NEG))

    ones_rows = jnp.ones((_SUM_ROWS, tq), BF16)

    def partial_softmax(logits, where):
        logits = [s.astype(BF16) for s in logits]
        ms = [jnp.max(s, axis=0, keepdims=True) for s in logits]
        ps = [jnp.exp2(s - m) for s, m in zip(logits, ms)]
        rs = []
        for p, (i, start) in zip(ps, where):
            vT_aug = jnp.concatenate([vT_ref[cols[i], pl.ds(start, tq)], ones_rows], axis=0)
            rs.append(jnp.dot(vT_aug, p, preferred_element_type=F32))
        return [(m.astype(F32), r) for m, r in zip(ms, rs)]

    def combine(m_run, r_run, parts):
        m_new = m_run
        for m, _ in parts:
            m_new = jnp.maximum(m_new, m)
        r_new = r_run * jnp.exp2(m_run - m_new)
        for m, r in parts:
            r_new = r_new + r * jnp.exp2(m - m_new)
        return m_new, r_new

    own_start = pl.multiple_of(j * tq, tq)
    prev_start = pl.multiple_of(jnp.maximum(j - 1, 0) * tq, tq)
    logits, where = [], []
    for i in heads:
        logits.append(jnp.dot(k_ref[pl.ds(own_start, tq), cols[i]], q_sT[i], preferred_element_type=F32)
                      + town[i])
        logits.append(jnp.dot(k_ref[pl.ds(prev_start, tq), cols[i]], q_sT[i], preferred_element_type=F32)
                      + (tprev[i] + prev_mask[i]))
        where += [(i, own_start), (i, prev_start)]
    parts = partial_softmax(logits, where)
    for i in heads:
        (m_own, r_own), part_prev = parts[2 * i], parts[2 * i + 1]
        m_scr[i], r_scr[i] = combine(m_own, r_own, [part_prev])

    def far_blocks(first_block, n):
        starts = [pl.multiple_of((first_block + b) * tq, tq) for b in range(n)]
        logits, where = [], []
        for i in heads:
            for start in starts:
                k_cat = jnp.concatenate([k_ref[pl.ds(start, tq), cols[i]], e_ref[pl.ds(start, tq), :]], axis=1)
                logits.append(jnp.dot(k_cat, q_catT[i], preferred_element_type=F32))
                where.append((i, start))
        parts = partial_softmax(logits, where)
        for i in heads:
            m_scr[i], r_scr[i] = combine(m_scr[i], r_scr[i], parts[i * n:(i + 1) * n])

    def far_body(g, carry):
        far_blocks(g * _FAR_GROUP, _FAR_GROUP)
        return carry

    n_far = jnp.maximum(j - 1, 0)
    n_groups = n_far // _FAR_GROUP
    lax.fori_loop(0, n_groups, far_body, 0)
    done = n_groups * _FAR_GROUP
    size = _FAR_GROUP // 2
    while size >= 1:
        @pl.when(((n_far - done) & size) != 0)
        def _(done=done, size=size):
            far_blocks(done, size)
        done = done + ((n_far - n_groups * _FAR_GROUP) & size)
        size //= 2

    for i in heads:
        r = r_scr[i]
        o_ref[:, cols[i]] = (r[:HEAD_DIM] / r[HEAD_DIM:HEAD_DIM + 1]).T.astype(o_ref.dtype)


def _attn_prompt(qT32, k16, vT16, kbar, rel_bias):
    seq = k16.shape[0]
    tq = MOBA_BLOCK
    n_blocks = seq // tq
    assert n_blocks % _FAR_GROUP == 0 and kbar.shape[0] == n_blocks
    pos = jnp.arange(seq, dtype=jnp.int32)
    onehot = (pos[:, None] // tq == jnp.arange(HEAD_DIM, dtype=jnp.int32)[None, :]).astype(BF16)
    r = jnp.arange(tq, dtype=jnp.int32)
    bucket_own = _t5_bucket(r[None, :] - r[:, None])
    bucket_prev = _t5_bucket(tq + r[None, :] - r[:, None])
    const = lambda h, j: (0, 0)
    hps = _HEADS_PER_STEP
    width = hps * HEAD_DIM
    return pl.pallas_call(
        _attn_prompt_kernel,
        out_shape=jax.ShapeDtypeStruct((seq, ATT_WIDTH), BF16),
        grid=(N_HEADS // hps, n_blocks),
        in_specs=[pl.BlockSpec(memory_space=pltpu.SMEM),
                  pl.BlockSpec((width, tq), lambda h, j: (h, j)),
                  pl.BlockSpec((seq, width), lambda h, j: (0, h)),
                  pl.BlockSpec((width, seq), lambda h, j: (h, 0)),
                  pl.BlockSpec((n_blocks, width), lambda h, j: (0, h)),
                  pl.BlockSpec((seq, HEAD_DIM), const),
                  pl.BlockSpec((tq, tq), const),
                  pl.BlockSpec((tq, tq), const)],
        out_specs=pl.BlockSpec((tq, width), lambda h, j: (j, h)),
        scratch_shapes=[pltpu.VMEM((hps, tq, tq), F32), pltpu.VMEM((hps, tq, tq), F32),
                        pltpu.VMEM((hps, 1, tq), F32), pltpu.VMEM((hps, HEAD_DIM + _SUM_ROWS, tq), F32)],
        compiler_params=_cparams(2),
        name="attn_prompt",
    )(rel_bias.T, qT32, k16, vT16, kbar, onehot, bucket_own, bucket_prev)


_SAMPLE_GROUP = 2


def _attn_sample_kernel(pt_ref, q_ref, kn_ref, vn_ref, rbrows_ref, bkp_ref, *rest, n_pages, dec_seq):
    k_pages = rest[:n_pages]
    v_pages = rest[n_pages:2 * n_pages]
    o_ref, k16, v16, kbar_scr, bprev_scr, bown_scr = rest[2 * n_pages:]
    b = pl.program_id(0)
    past = n_pages * PAGE_SIZE
    n_past_blocks = past // MOBA_BLOCK
    rows = N_HEADS * dec_seq
    pages_per_block = MOBA_BLOCK // PAGE_SIZE

    @pl.when(b == 0)
    def _():
        kbar_scr[...] = jnp.zeros_like(kbar_scr)
        rb = rbrows_ref[...]
        bp = bkp_ref[...]
        tp = jnp.zeros((rows, MOBA_BLOCK), F32)
        for bb in range(NUM_BUCKETS - 1):
            tp = jnp.where(bp == bb, rb[:, bb:bb + 1], tp)
        bprev_scr[...] = tp
        t_row = lax.broadcasted_iota(jnp.int32, (rows, V7X_LANES), 0) & (dec_seq - 1)
        c_col = lax.broadcasted_iota(jnp.int32, (rows, V7X_LANES), 1)
        rel = t_row - c_col
        to = jnp.zeros((rows, V7X_LANES), F32)
        for bb in range(dec_seq):
            to = jnp.where(rel == bb, rb[:, bb:bb + 1], to)
        bown_scr[...] = jnp.where(rel >= 0, to, NEG)

    q = q_ref[...]
    q_rep = jnp.concatenate([q] * N_HEADS, axis=0)
    row_head = lax.broadcasted_iota(jnp.int32, (rows, ATT_WIDTH), 0) >> int(math.log2(dec_seq))
    col_head = lax.broadcasted_iota(jnp.int32, (rows, ATT_WIDTH), 1) >> int(math.log2(HEAD_DIM))
    q_bd = jnp.where(row_head == col_head, q_rep, 0.0)
    q_s = (q_bd * SCALE).astype(BF16)

    def to_head_lanes(pages, dst, n, sums_ref):
        for h in range(N_HEADS):
            cols = slice(h * HEAD_DIM, (h + 1) * HEAD_DIM)
            total = None
            for p in range(n * pages_per_block, (n + 1) * pages_per_block):
                xh = pages[p][pl.ds(h, PAGE_SIZE, stride=N_HEADS), :]
                dst[p * PAGE_SIZE:(p + 1) * PAGE_SIZE, cols] = xh.astype(BF16)
                if sums_ref is not None:
                    cs = jnp.sum(xh, axis=0, keepdims=True)
                    total = cs if total is None else total + cs
            if sums_ref is not None:
                sums_ref[n:n + 1, cols] = total * (1.0 / MOBA_BLOCK)

    def new_rows(ref):
        pad = jnp.zeros((V7X_LANES - dec_seq, ATT_WIDTH), F32)
        return jnp.concatenate([ref[...], pad], axis=0).astype(BF16)

    groups = [slice(g * MOBA_BLOCK, (g + _SAMPLE_GROUP) * MOBA_BLOCK)
              for g in range(0, n_past_blocks, _SAMPLE_GROUP)]
    new_keys = slice(past, past + V7X_LANES)
    raw = []
    for keys in groups:
        for n in range(keys.start // MOBA_BLOCK, keys.stop // MOBA_BLOCK):
            to_head_lanes(k_pages, k16, n, kbar_scr)
        raw.append(lax.dot_general(q_s, k16[keys, :], NT_DIMS, preferred_element_type=F32))
    k16[new_keys, :] = new_rows(kn_ref)
    raw.append(lax.dot_general(q_s, k16[new_keys, :], NT_DIMS, preferred_element_type=F32))

    assert n_past_blocks == V7X_SUBLANES
    q_pad = jnp.concatenate([q_bd, jnp.zeros((V7X_LANES - rows, ATT_WIDTH), F32)], axis=0)
    scores_t = lax.dot_general(kbar_scr[0:n_past_blocks, :], q_pad, NT_DIMS, precision=lax.Precision.HIGHEST,
                               preferred_element_type=F32)
    blk = lax.broadcasted_iota(jnp.int32, scores_t.shape, 0)
    rank = jnp.zeros(scores_t.shape, F32)
    for k in range(1, n_past_blocks):
        other = pltpu.roll(scores_t, k, 0)
        ge = jnp.where(other >= scores_t, 1.0, 0.0)
        gt = jnp.where(other > scores_t, 1.0, 0.0)
        rank = rank + jnp.where(blk >= k, ge, gt)
    sel_t = jnp.where(rank < MOBA_TOPK, 1.0, 0.0)
    eye = jnp.where(lax.broadcasted_iota(jnp.int32, (rows, V7X_LANES), 0)
                    == lax.broadcasted_iota(jnp.int32, (rows, V7X_LANES), 1), 1.0, 0.0).astype(BF16)
    sel_pad = jnp.concatenate([sel_t, jnp.zeros((V7X_LANES - n_past_blocks, V7X_LANES), F32)], axis=0)
    sel = lax.dot_general(eye, sel_pad.astype(BF16), NT_DIMS, preferred_element_type=F32)
    mask_add = jnp.where(sel > 0.5, 0.0, NEG)

    pieces = []
    for keys, piece in zip(groups, raw):
        cols = []
        for i, n in enumerate(range(keys.start // MOBA_BLOCK, keys.stop // MOBA_BLOCK)):
            blk = piece[:, i * MOBA_BLOCK:(i + 1) * MOBA_BLOCK] + mask_add[:, n:n + 1]
            if n == n_past_blocks - 1:
                blk = blk + bprev_scr[...]
            cols.append(blk)
        pieces.append(jnp.concatenate(cols, axis=1))
    pieces.append(raw[-1] + bown_scr[...])
    m = pieces[0].max(axis=-1, keepdims=True)
    for piece in pieces[1:]:
        m = jnp.maximum(m, piece.max(axis=-1, keepdims=True))
    probs = [jnp.exp(piece - m) for piece in pieces]
    l = sum(p.sum(axis=-1, keepdims=True) for p in probs)

    out = None
    for keys, p in zip(groups + [new_keys], probs):
        if keys is new_keys:
            v16[keys, :] = new_rows(vn_ref)
        else:
            for n in range(keys.start // MOBA_BLOCK, keys.stop // MOBA_BLOCK):
                to_head_lanes(v_pages, v16, n, None)
        part = jnp.dot(p.astype(BF16), v16[keys, :], preferred_element_type=F32)
        out = part if out is None else out + part
    out = out / l
    for h in range(N_HEADS):
        sl = slice(h * HEAD_DIM, (h + 1) * HEAD_DIM)
        o_ref[:, sl] = out[h * dec_seq:(h + 1) * dec_seq, sl]


def _attn_sample(q3, k3, v3, cache_k, cache_v, page_table, rel_bias):
    n_seq, dec_seq, _ = q3.shape
    n_pages = page_table.shape[1]
    past = n_pages * PAGE_SIZE
    rows = N_HEADS * dec_seq
    page_rows = PAGE_SIZE * N_HEADS
    ck = cache_k.reshape(-1, HEAD_DIM)
    cv = cache_v.reshape(-1, HEAD_DIM)
    rb_rel = rel_bias.T - rel_bias[NUM_BUCKETS - 1][:, None]
    rbrows = jnp.repeat(rb_rel, dec_seq, axis=0)
    t = jnp.arange(rows, dtype=jnp.int32) % dec_seq
    c = jnp.arange(MOBA_BLOCK, dtype=jnp.int32)
    bucket_prev = _t5_bucket(MOBA_BLOCK + t[:, None] - c[None, :])
    tok = pl.BlockSpec((None, dec_seq, ATT_WIDTH), lambda b, pt: (b, 0, 0))
    const2 = lambda shape: pl.BlockSpec(shape, lambda b, pt: (0, 0))

    def page_spec(p):
        return pl.BlockSpec((page_rows, HEAD_DIM), lambda b, pt: (pt[b, p], 0))

    grid_spec = pltpu.PrefetchScalarGridSpec(
        num_scalar_prefetch=1,
        grid=(n_seq,),
        in_specs=[tok, tok, tok, const2((rows, NUM_BUCKETS)), const2((rows, MOBA_BLOCK))]
                 + [page_spec(p) for p in range(n_pages)] * 2,
        out_specs=tok,
        scratch_shapes=[pltpu.VMEM((past + V7X_LANES, ATT_WIDTH), BF16),
                        pltpu.VMEM((past + V7X_LANES, ATT_WIDTH), BF16),
                        pltpu.VMEM((V7X_LANES, ATT_WIDTH), F32),
                        pltpu.VMEM((rows, MOBA_BLOCK), F32),
                        pltpu.VMEM((rows, V7X_LANES), F32)],
    )
    return pl.pallas_call(
        functools.partial(_attn_sample_kernel, n_pages=n_pages, dec_seq=dec_seq),
        out_shape=jax.ShapeDtypeStruct((n_seq, dec_seq, ATT_WIDTH), F32),
        grid_spec=grid_spec,
        compiler_params=_cparams(1),
        name="attn_sample",
    )(page_table, q3, k3, v3, rbrows, bucket_prev, *([ck] * n_pages), *([cv] * n_pages))


def _merge_kernel(x_ref, gt_ref, u_ref, vn_ref, ga_ref, gb_ref, att_ref, wsp_ref, bsp_ref,
                  wpa_ref, wpb_ref, wo_ref, o_ref, gm_scr, *, tm):
    for c in range(tm // GM_CHUNK):
        rows = slice(c * GM_CHUNK, (c + 1) * GM_CHUNK)
        for g in range(GM_GROUPS):
            cols = slice(g * GM_CHUNK, (g + 1) * GM_CHUNK)
            s = jnp.dot(wsp_ref[g], vn_ref[rows, cols], preferred_element_type=F32) + bsp_ref[:, cols]
            gm_scr[rows, cols] = (u_ref[rows, cols].astype(F32) * s).astype(BF16)
    a = jnp.dot(gm_scr[...], wpa_ref[...], preferred_element_type=F32)
    b = jnp.dot(att_ref[...].astype(BF16), wpb_ref[...], preferred_element_type=F32)
    merged = ga_ref[...].astype(F32) * a + gb_ref[...].astype(F32) * b
    c_out = jnp.dot(merged.astype(BF16), wo_ref[...], preferred_element_type=F32)
    o_ref[...] = x_ref[...] + gt_ref[...] * c_out.reshape(o_ref.shape)


def _merge(x3, mod3, u16, vn16, ga16, gb16, att, w_sp16, b_sp, w_pa16, w_pb16, w_o16, *, prompt, tm):
    n_rows = x3.shape[0] * V7X_SUBLANES
    g_rows = tm // V7X_SUBLANES
    row3 = pl.BlockSpec((g_rows, V7X_SUBLANES, D_MODEL), lambda i: (i, 0, 0))
    seg = pl.BlockSpec((tm, SEG), lambda i: (i, 0))
    gate = pl.BlockSpec((tm, D_MODEL), lambda i: (i, 0))
    resident = lambda shape: pl.BlockSpec(shape, lambda i: (0,) * len(shape), pipeline_mode=pl.Buffered(1))
    return pl.pallas_call(
        functools.partial(_merge_kernel, tm=tm),
        out_shape=jax.ShapeDtypeStruct(x3.shape, F32),
        grid=(n_rows // tm,),
        in_specs=[row3, _mod_spec(g_rows, 2, prompt, 1), seg, seg, gate, gate, seg,
                  resident((GM_GROUPS, GM_CHUNK, GM_CHUNK)), resident((GM_CHUNK, GM_WIDTH)),
                  resident((GM_WIDTH, D_MODEL)), resident((ATT_WIDTH, D_MODEL)), resident((D_MODEL, D_MODEL))],
        out_specs=row3,
        scratch_shapes=[pltpu.VMEM((tm, GM_WIDTH), BF16)],
        compiler_params=_cparams(1),
        name="merge_prompt" if prompt else "merge_sample",
    )(x3, mod3, u16, vn16, ga16, gb16, att, w_sp16, b_sp, w_pa16, w_pb16, w_o16)


def _ffn_kernel(x_ref, sh_ref, sc_ref, gt_ref, g2_ref, wg_ref, wu_ref, wd_ref, o_ref, h_scr, *, tm):
    j = pl.program_id(1)
    last = pl.num_programs(1) - 1
    acc_ref = o_ref

    def chunk(h):
        g = jnp.dot(h, wg_ref[...], preferred_element_type=F32)
        u = jnp.dot(h, wu_ref[...], preferred_element_type=F32)
        a = (g * _sigmoid(g) * u).astype(BF16)
        return jnp.dot(a, wd_ref[...], preferred_element_type=F32)

    @pl.when(j == 0)
    def _():
        x = x_ref[...]
        ms = jnp.mean(x * x, axis=-1, keepdims=True)
        y = x * lax.rsqrt(ms + EPS) * g2_ref[...]
        h = (y * (1.0 + sc_ref[...]) + sh_ref[...]).reshape(tm, D_MODEL).astype(BF16)
        h_scr[...] = h
        acc_ref[...] = chunk(h).reshape(o_ref.shape)

    @pl.when((j > 0) & (j < last))
    def _():
        acc_ref[...] += chunk(h_scr[...]).reshape(o_ref.shape)

    @pl.when(j == last)
    def _():
        total = acc_ref[...] + chunk(h_scr[...]).reshape(o_ref.shape)
        o_ref[...] = x_ref[...] + gt_ref[...] * total


def _ffn(x3, mod3, g2, w_gu16, w_down16, *, prompt, tm, tf):
    n_rows = x3.shape[0] * V7X_SUBLANES
    g_rows = tm // V7X_SUBLANES
    n_chunks = D_FF // tf
    row3 = pl.BlockSpec((g_rows, V7X_SUBLANES, D_MODEL), lambda i, j: (i, 0, 0))
    return pl.pallas_call(
        functools.partial(_ffn_kernel, tm=tm),
        out_shape=jax.ShapeDtypeStruct(x3.shape, F32),
        grid=(n_rows // tm, n_chunks),
        in_specs=[row3, _mod_spec(g_rows, 3, prompt, 2), _mod_spec(g_rows, 4, prompt, 2),
                  _mod_spec(g_rows, 5, prompt, 2),
                  pl.BlockSpec((1, 1, D_MODEL), lambda i, j: (0, 0, 0)),
                  pl.BlockSpec((D_MODEL, tf), lambda i, j: (0, j)),
                  pl.BlockSpec((D_MODEL, tf), lambda i, j: (0, n_chunks + j)),
                  pl.BlockSpec((tf, D_MODEL), lambda i, j: (j, 0))],
        out_specs=row3,
        scratch_shapes=[pltpu.VMEM((tm, D_MODEL), BF16)],
        compiler_params=_cparams(2),
        name="ffn_prompt" if prompt else "ffn_sample",
    )(x3, mod3, mod3, mod3, g2.reshape(1, 1, D_MODEL), w_gu16, w_gu16, w_down16)


def _gm_spatial_weights(gm_ws, gm_bs, dec_seq):
    tril = jnp.tril(jnp.ones((GM_CHUNK, GM_CHUNK), dtype=bool))
    w = jnp.where(tril[None], gm_ws, jnp.zeros_like(gm_ws))
    w_prompt = w.astype(BF16)
    b_prompt = jnp.repeat(gm_bs.T, GM_WIDTH // GM_GROUPS, axis=1)
    reps = GM_CHUNK // dec_seq
    eye = jnp.eye(reps, dtype=F32)
    w_small = w[:, :dec_seq, :dec_seq]
    w_sample = jnp.einsum('ab,gts->gatbs', eye, w_small).reshape(GM_GROUPS, GM_CHUNK, GM_CHUNK).astype(BF16)
    b_sample = jnp.tile(jnp.repeat(gm_bs[:, :dec_seq].T, GM_WIDTH // GM_GROUPS, axis=1), (reps, 1))
    return w_prompt, b_prompt, w_sample, b_sample


def kernel(x_prompt, x_sample, c_prompt, c_sample, cache_k, cache_v, page_table, w_ada, b_ada, norm1_g, w_in,
           q_norm_g, k_norm_g, gm_ln_g, gm_ln_b, gm_ws, gm_bs, w_pa, w_pb, w_o, norm2_g, w_gu, w_down,
           rel_bias):
    depth = w_in.shape[0]
    assert depth == 1, "single-layer step"
    n_seq, dec_seq, _ = x_sample.shape
    batch, seq, _ = x_prompt.shape
    assert batch == 1 and n_seq == _PROMPT_MOD_ROW and dec_seq == V7X_SUBLANES

    pad_rows = V7X_SUBLANES - 1
    c_all = jnp.concatenate([c_sample, c_prompt, jnp.zeros((pad_rows, D_MODEL), F32)], axis=0)
    mod = _modulation(c_all, w_ada[0], b_ada[0])
    mod3 = mod.reshape(mod.shape[0], 1, 6 * D_MODEL)

    w_in16 = w_in[0].astype(BF16)
    w_pa16 = w_pa[0].astype(BF16)
    w_pb16 = w_pb[0].astype(BF16)
    w_o16 = w_o[0].astype(BF16)
    w_gu16 = w_gu[0].astype(BF16)
    w_down16 = w_down[0].astype(BF16)
    wsp_p, bsp_p, wsp_s, bsp_s = _gm_spatial_weights(gm_ws[0], gm_bs[0], dec_seq)

    xp3 = x_prompt.reshape(seq // V7X_SUBLANES, V7X_SUBLANES, D_MODEL)
    xs3 = x_sample

    proj = functools.partial(_inproj, g1=norm1_g[0], w_in16=w_in16, qg=q_norm_g[0], kg=k_norm_g[0],
                             lng=gm_ln_g[0], lnb=gm_ln_b[0])
    tm_prompt = 512
    qT32, k16, vT16, k_out, v_out, u16, vn16, ga16, gb16, kbar_tiles = proj(xp3, mod3, prompt=True, tm=tm_prompt)
    per_tile = tm_prompt // MOBA_BLOCK
    kbar = kbar_tiles.reshape(-1, V7X_SUBLANES, ATT_WIDTH)[:, :per_tile].reshape(-1, ATT_WIDTH)
    sq32, sk32, sv32, sk_out, sv_out, su16, svn16, svn32, sga16, sgb16 = proj(xs3, mod3, prompt=False, tm=256)

    att_p = _attn_prompt(qT32, k16, vT16, kbar, rel_bias)
    tok3 = lambda a: a.reshape(n_seq, dec_seq, ATT_WIDTH)
    att_s = _attn_sample(tok3(sq32), tok3(sk32), tok3(sv32), cache_k[0], cache_v[0], page_table, rel_bias)
    att_s = att_s.reshape(n_seq * dec_seq, ATT_WIDTH)

    mrg = functools.partial(_merge, w_pa16=w_pa16, w_pb16=w_pb16, w_o16=w_o16, tm=256)
    x1p = mrg(xp3, mod3, u16, vn16, ga16, gb16, att_p, wsp_p, bsp_p, prompt=True)
    x1s = mrg(xs3, mod3, su16, svn16, sga16, sgb16, att_s, wsp_s, bsp_s, prompt=False)

    ffn = functools.partial(_ffn, g2=norm2_g[0], w_gu16=w_gu16, w_down16=w_down16, tf=512)
    y_p = ffn(x1p, mod3, prompt=True, tm=1024).reshape(batch, seq, D_MODEL)
    y_s = ffn(x1s, mod3, prompt=False, tm=1024)

    kv_p = lambda a: a.reshape(1, batch, seq, N_HEADS, HEAD_DIM)
    kv_s = lambda a: a.reshape(1, n_seq, dec_seq, N_HEADS, HEAD_DIM)
    return (y_p, y_s, kv_p(k_out), kv_p(v_out), kv_s(sk_out), kv_s(sv_out),
            svn32.reshape(1, n_seq, dec_seq, GM_WIDTH))
```

```python
import functools
import math

import jax
import jax.numpy as jnp
from jax import lax
from jax.experimental import pallas as pl
from jax.experimental.pallas import tpu as pltpu

F32 = jnp.float32
BF16 = jnp.bfloat16

D_MODEL = 2048
N_HEADS = 8
HEAD_DIM = 128
ATT_WIDTH = N_HEADS * HEAD_DIM
GM_WIDTH = 1024
GM_GROUPS = 8
GM_CHUNK = 128
MOBA_BLOCK = 256
MOBA_TOPK = 3
NUM_BUCKETS = 32
MAX_DISTANCE = 128
PAGE_SIZE = 128
D_FF = 5632
EPS = 1e-6
SCALE = HEAD_DIM ** -0.5
LOG2E = math.log2(math.e)

V7X_LANES = 128
V7X_SUBLANES = 8
V7X_VMEM_BYTES = 64 * 1024 * 1024
VMEM_LIMIT = V7X_VMEM_BYTES - 8 * 1024 * 1024

NEG = -1e30
SEG = 1024

NT_DIMS = (((1,), (1,)), ((), ()))


def _cparams(n_axes, vmem=VMEM_LIMIT):
    return pltpu.CompilerParams(dimension_semantics=("arbitrary",) * n_axes, vmem_limit_bytes=vmem)


def _sigmoid(x):
    return 1.0 / (1.0 + jnp.exp(-x))


def _gelu_tanh(x):
    c = math.sqrt(2.0 / math.pi)
    return x * (0.5 * (1.0 + jnp.tanh(c * (x + 0.044715 * (x * x * x)))))


def _t5_bucket(rel):
    n = jnp.maximum(rel, 0)
    max_exact = NUM_BUCKETS // 2
    nf = jnp.maximum(n, 1).astype(F32)
    large = max_exact + (jnp.log(nf / max_exact) / math.log(MAX_DISTANCE / max_exact)
                         * (NUM_BUCKETS - max_exact)).astype(jnp.int32)
    large = jnp.minimum(large, NUM_BUCKETS - 1)
    return jnp.where(n < max_exact, n, large)


def _mod_kernel(c_ref, w_ref, b_ref, o_ref):
    c = c_ref[...]
    s = (c * _sigmoid(c)).astype(BF16)
    o_ref[...] = jnp.dot(s, w_ref[...].astype(BF16), preferred_element_type=F32) + b_ref[...]


def _modulation(c_all, w_ada, b_ada):
    rows = c_all.shape[0]
    n_out = w_ada.shape[1]
    tn = 1024
    return pl.pallas_call(
        _mod_kernel,
        out_shape=jax.ShapeDtypeStruct((rows, n_out), F32),
        grid=(n_out // tn,),
        in_specs=[pl.BlockSpec((rows, D_MODEL), lambda j: (0, 0)),
                  pl.BlockSpec((D_MODEL, tn), lambda j: (0, j)),
                  pl.BlockSpec((1, tn), lambda j: (0, j))],
        out_specs=pl.BlockSpec((rows, tn), lambda j: (0, j)),
        compiler_params=_cparams(1),
        name="mod",
    )(c_all, w_ada, b_ada.reshape(1, n_out))


_SEG_K, _SEG_Q, _SEG_V, _SEG_U, _SEG_VG, _SEG_GA0, _SEG_GA1, _SEG_GB0, _SEG_GB1 = range(9)


def _inproj_kernel(x_ref, sh_ref, sc_ref, g1_ref, w_ref, qg_ref, kg_ref, lng_ref, lnb_ref,
                   *rest, tm, prompt):
    if prompt:
        qT32, k16, vT16, k_out, v_out, u16, vn16, ga16, gb16, kbar, h_scr = rest
        q32 = k32 = v32 = None
    else:
        q32, k32, v32, k_out, v_out, u16, vn16, vn32, ga16, gb16, h_scr = rest
        qT32 = k16 = vT16 = None
    j = pl.program_id(1)

    def normed_input():
        x = x_ref[...]
        ms = jnp.mean(x * x, axis=-1, keepdims=True)
        y = x * lax.rsqrt(ms + EPS) * g1_ref[...]
        h = y * (1.0 + sc_ref[...]) + sh_ref[...]
        return h.reshape(tm, D_MODEL).astype(BF16)

    sub = MOBA_BLOCK
    sub_tiles = [slice(r * sub, (r + 1) * sub) for r in range(tm // sub)]

    def matmul(rows, h=None):
        lhs = h_scr[rows, :] if h is None else h[rows]
        return jnp.dot(lhs, w_ref[...], preferred_element_type=F32)

    def store_heads(h_val, y_of_head, dst32, dst16, dst_out, with_means, dst_t=None):
        for r, rows in enumerate(sub_tiles):
            z = matmul(rows, h_val)
            for h in range(N_HEADS):
                sl = slice(h * HEAD_DIM, (h + 1) * HEAD_DIM)
                yh = y_of_head(z[:, sl])
                if dst32 is not None:
                    dst32[rows, sl] = yh
                if dst16 is not None:
                    dst16[rows, sl] = yh.astype(BF16)
                if dst_t is not None:
                    dst_t[sl, rows] = yh.T.astype(dst_t.dtype)
                if dst_out is not None:
                    dst_out[pl.ds(rows.start * N_HEADS + h, sub, stride=N_HEADS), :] = yh
                if with_means:
                    kbar[r:r + 1, sl] = jnp.mean(yh, axis=0, keepdims=True)

    def head_norm(g_ref):
        def f(zh):
            ms = jnp.mean(zh * zh, axis=-1, keepdims=True)
            return zh * lax.rsqrt(ms + EPS) * g_ref[...]
        return f

    def elementwise(f, dsts):
        for rows in sub_tiles:
            y = f(matmul(rows))
            for dst in dsts:
                dst[rows, :] = y.astype(dst.dtype)

    def gelu_layernorm(z):
        a = _gelu_tanh(z)
        mu = jnp.mean(a, axis=-1, keepdims=True)
        ac = a - mu
        y = ac * lax.rsqrt(jnp.mean(ac * ac, axis=-1, keepdims=True) + EPS)
        return y * lng_ref[...] + lnb_ref[...]

    @pl.when(j == _SEG_K)
    def _():
        h = normed_input()
        h_scr[...] = h
        if prompt:
            kbar[...] = jnp.zeros_like(kbar)
        store_heads(h, head_norm(kg_ref), k32, k16, k_out, prompt)

    @pl.when(j == _SEG_Q)
    def _():
        store_heads(None, head_norm(qg_ref), q32, None, None, False, dst_t=qT32)

    @pl.when(j == _SEG_V)
    def _():
        store_heads(None, lambda zh: zh, v32, None, v_out, False, dst_t=vT16)

    @pl.when(j == _SEG_U)
    def _():
        elementwise(_gelu_tanh, [u16])

    @pl.when(j == _SEG_VG)
    def _():
        elementwise(gelu_layernorm, [vn16] if prompt else [vn16, vn32])

    @pl.when((j == _SEG_GA0) | (j == _SEG_GA1))
    def _():
        elementwise(_sigmoid, [ga16])

    @pl.when((j == _SEG_GB0) | (j == _SEG_GB1))
    def _():
        elementwise(_sigmoid, [gb16])


def _mod_spec(g_rows, chunk, prompt, n_grid):
    if n_grid == 2:
        if prompt:
            return pl.BlockSpec((1, 1, D_MODEL), lambda i, j: (_PROMPT_MOD_ROW, 0, chunk))
        return pl.BlockSpec((g_rows, 1, D_MODEL), lambda i, j: (i, 0, chunk))
    if prompt:
        return pl.BlockSpec((1, 1, D_MODEL), lambda i: (_PROMPT_MOD_ROW, 0, chunk))
    return pl.BlockSpec((g_rows, 1, D_MODEL), lambda i: (i, 0, chunk))


_PROMPT_MOD_ROW = 128


def _w_in_col(j):
    return jnp.where(j < 2, 1 - j, j)


def _inproj(x3, mod3, g1, w_in16, qg, kg, lng, lnb, *, prompt, tm):
    n_rows = x3.shape[0] * V7X_SUBLANES
    g_rows = tm // V7X_SUBLANES
    grid = (n_rows // tm, 9)
    row_blk = lambda i, j: (i, 0)
    f32_seg = jax.ShapeDtypeStruct((n_rows, SEG), F32)
    b16_seg = jax.ShapeDtypeStruct((n_rows, SEG), BF16)
    b16_gate = jax.ShapeDtypeStruct((n_rows, D_MODEL), BF16)
    seg_spec = pl.BlockSpec((tm, SEG), row_blk)
    kv_out = jax.ShapeDtypeStruct((n_rows * N_HEADS, HEAD_DIM), F32)
    kv_spec = pl.BlockSpec((tm * N_HEADS, HEAD_DIM), row_blk)
    ga_spec = pl.BlockSpec((tm, SEG), lambda i, j: (i, jnp.clip(j - _SEG_GA0, 0, 1)))
    gb_spec = pl.BlockSpec((tm, SEG), lambda i, j: (i, jnp.clip(j - _SEG_GB0, 0, 1)))
    if prompt:
        assert tm % MOBA_BLOCK == 0 and tm // MOBA_BLOCK <= V7X_SUBLANES
        t_spec = pl.BlockSpec((SEG, tm), lambda i, j: (0, i))
        out_shape = [jax.ShapeDtypeStruct((SEG, n_rows), F32), b16_seg, jax.ShapeDtypeStruct((SEG, n_rows), BF16),
                     kv_out, kv_out, b16_seg, b16_seg, b16_gate, b16_gate,
                     jax.ShapeDtypeStruct((grid[0] * V7X_SUBLANES, SEG), F32)]
        out_specs = ([t_spec, seg_spec, t_spec] + [kv_spec] * 2 + [seg_spec] * 2
                     + [ga_spec, gb_spec, pl.BlockSpec((V7X_SUBLANES, SEG), row_blk)])
    else:
        out_shape = [f32_seg, f32_seg, f32_seg, kv_out, kv_out, b16_seg, b16_seg, f32_seg, b16_gate, b16_gate]
        out_specs = [seg_spec] * 3 + [kv_spec] * 2 + [seg_spec] * 3 + [ga_spec, gb_spec]
    vec = lambda n: pl.BlockSpec((1, n), lambda i, j: (0, 0))
    return pl.pallas_call(
        functools.partial(_inproj_kernel, tm=tm, prompt=prompt),
        out_shape=out_shape,
        grid=grid,
        in_specs=[pl.BlockSpec((g_rows, V7X_SUBLANES, D_MODEL), lambda i, j: (i, 0, 0)),
                  _mod_spec(g_rows, 0, prompt, 2),
                  _mod_spec(g_rows, 1, prompt, 2),
                  pl.BlockSpec((1, 1, D_MODEL), lambda i, j: (0, 0, 0)),
                  pl.BlockSpec((D_MODEL, SEG), lambda i, j: (0, _w_in_col(j))),
                  vec(HEAD_DIM), vec(HEAD_DIM), vec(GM_WIDTH), vec(GM_WIDTH)],
        out_specs=out_specs,
        scratch_shapes=[pltpu.VMEM((tm, D_MODEL), BF16)],
        compiler_params=_cparams(2),
        name="inproj_prompt" if prompt else "inproj_sample",
    )(x3, mod3, mod3, g1.reshape(1, 1, D_MODEL), w_in16, qg.reshape(1, HEAD_DIM), kg.reshape(1, HEAD_DIM),
      lng.reshape(1, GM_WIDTH), lnb.reshape(1, GM_WIDTH))


def _select_topk(scores, n_valid, axis):
    idx = lax.broadcasted_iota(jnp.int32, scores.shape, axis)
    idx_f = idx.astype(F32)
    cand = idx < n_valid
    sel = jnp.zeros(scores.shape, jnp.bool_)
    for _ in range(MOBA_TOPK):
        s_m = jnp.where(cand, scores, -jnp.inf)
        mx = jnp.max(s_m, axis=axis, keepdims=True)
        first = jnp.min(jnp.where(cand & (s_m == mx), idx_f, 1e9), axis=axis, keepdims=True)
        pick = idx_f == first
        sel = sel | pick
        cand = cand & jnp.logical_not(pick)
    return sel


_FAR_GROUP = 4
_SUM_ROWS = 16
_HEADS_PER_STEP = 8


def _attn_prompt_kernel(rb_ref, qT_ref, k_ref, vT_ref, kbar_ref, e_ref, bko_ref, bkp_ref, o_ref,
                        town, tprev, m_scr, r_scr):
    hp = pl.program_id(0)
    j = pl.program_id(1)
    tq = MOBA_BLOCK
    n_sel_rows = kbar_ref.shape[0]
    heads = range(_HEADS_PER_STEP)
    cols = [slice(i * HEAD_DIM, (i + 1) * HEAD_DIM) for i in heads]

    @pl.when(j == 0)
    def _():
        bo = bko_ref[...]
        bp = bkp_ref[...]
        key = lax.broadcasted_iota(jnp.int32, (tq, tq), 0)
        qry = lax.broadcasted_iota(jnp.int32, (tq, tq), 1)
        for i in heads:
            h = hp * _HEADS_PER_STEP + i
            b_far = rb_ref[h, NUM_BUCKETS - 1]
            to = jnp.zeros((tq, tq), F32)
            tp = jnp.zeros((tq, tq), F32)
            for b in range(NUM_BUCKETS - 1):
                val = (rb_ref[h, b] - b_far) * LOG2E
                to = jnp.where(bo == b, val, to)
                tp = jnp.where(bp == b, val, tp)
            town[i] = jnp.where(key <= qry, to, NEG)
            tprev[i] = tp

    q_sT, q_catT, prev_mask = [], [], []
    for i in heads:
        qT = qT_ref[cols[i], :]
        scoresT = jnp.dot(kbar_ref[:, cols[i]], qT, precision=lax.Precision.HIGHEST,
                          preferred_element_type=F32)
        selT = _select_topk(scoresT, j, axis=0)
        blk = lax.broadcasted_iota(jnp.int32, selT.shape, 0)
        qs = (qT * (SCALE * LOG2E)).astype(BF16)
        aug = jnp.where(selT & (blk < j - 1), 0.0, NEG)
        aug = jnp.concatenate([aug, jnp.full((HEAD_DIM - n_sel_rows, tq), NEG, F32)], axis=0)
        q_sT.append(qs)
        q_catT.append(jnp.concatenate([qs, aug.astype(BF16)], axis=0))
        prev_sel = jnp.max(jnp.where(selT & (blk == j - 1), 1.0, 0.0), axis=0, keepdims=True) > 0.0
        prev_mask.append(jnp.where(prev_sel, 0.0, NEG))

    ones_rows = jnp.ones((_SUM_ROWS, tq), BF16)

    def partial_softmax(logits, where):
        logits = [s.astype(BF16) for s in logits]
        ms = [jnp.max(s, axis=0, keepdims=True) for s in logits]
        ps = [jnp.exp2(s - m) for s, m in zip(logits, ms)]
        rs = []
        for p, (i, start) in zip(ps, where):
            vT_aug = jnp.concatenate([vT_ref[cols[i], pl.ds(start, tq)], ones_rows], axis=0)
            rs.append(jnp.dot(vT_aug, p, preferred_element_type=F32))
        return [(m.astype(F32), r) for m, r in zip(ms, rs)]

    def combine(m_run, r_run, parts):
        m_new = m_run
        for m, _ in parts:
            m_new = jnp.maximum(m_new, m)
        r_new = r_run * jnp.exp2(m_run - m_new)
        for m, r in parts:
            r_new = r_new + r * jnp.exp2(m - m_new)
        return m_new, r_new

    own_start = pl.multiple_of(j * tq, tq)
    prev_start = pl.multiple_of(jnp.maximum(j - 1, 0) * tq, tq)
    logits, where = [], []
    for i in heads:
        logits.append(jnp.dot(k_ref[pl.ds(own_start, tq), cols[i]], q_sT[i], preferred_element_type=F32)
                      + town[i])
        logits.append(jnp.dot(k_ref[pl.ds(prev_start, tq), cols[i]], q_sT[i], preferred_element_type=F32)
                      + (tprev[i] + prev_mask[i]))
        where += [(i, own_start), (i, prev_start)]
    parts = partial_softmax(logits, where)
    for i in heads:
        (m_own, r_own), part_prev = parts[2 * i], parts[2 * i + 1]
        m_scr[i], r_scr[i] = combine(m_own, r_own, [part_prev])

    def far_blocks(first_block, n):
        starts = [pl.multiple_of((first_block + b) * tq, tq) for b in range(n)]
        logits, where = [], []
        for i in heads:
            for start in starts:
                k_cat = jnp.concatenate([k_ref[pl.ds(start, tq), cols[i]], e_ref[pl.ds(start, tq), :]], axis=1)
                logits.append(jnp.dot(k_cat, q_catT[i], preferred_element_type=F32))
                where.append((i, start))
        parts = partial_softmax(logits, where)
        for i in heads:
            m_scr[i], r_scr[i] = combine(m_scr[i], r_scr[i], parts[i * n:(i + 1) * n])

    def far_body(g, carry):
        far_blocks(g * _FAR_GROUP, _FAR_GROUP)
        return carry

    n_far = jnp.maximum(j - 1, 0)
    n_groups = n_far // _FAR_GROUP
    lax.fori_loop(0, n_groups, far_body, 0)
    done = n_groups * _FAR_GROUP
    size = _FAR_GROUP // 2
    while size >= 1:
        @pl.when(((n_far - done) & size) != 0)
        def _(done=done, size=size):
            far_blocks(done, size)
        done = done + ((n_far - n_groups * _FAR_GROUP) & size)
        size //= 2

    for i in heads:
        r = r_scr[i]
        o_ref[:, cols[i]] = (r[:HEAD_DIM] / r[HEAD_DIM:HEAD_DIM + 1]).T.astype(o_ref.dtype)


def _attn_prompt(qT32, k16, vT16, kbar, rel_bias):
    seq = k16.shape[0]
    tq = MOBA_BLOCK
    n_blocks = seq // tq
    assert n_blocks % _FAR_GROUP == 0 and kbar.shape[0] == n_blocks
    pos = jnp.arange(seq, dtype=jnp.int32)
    onehot = (pos[:, None] // tq == jnp.arange(HEAD_DIM, dtype=jnp.int32)[None, :]).astype(BF16)
    r = jnp.arange(tq, dtype=jnp.int32)
    bucket_own = _t5_bucket(r[None, :] - r[:, None])
    bucket_prev = _t5_bucket(tq + r[None, :] - r[:, None])
    const = lambda h, j: (0, 0)
    hps = _HEADS_PER_STEP
    width = hps * HEAD_DIM
    return pl.pallas_call(
        _attn_prompt_kernel,
        out_shape=jax.ShapeDtypeStruct((seq, ATT_WIDTH), BF16),
        grid=(N_HEADS // hps, n_blocks),
        in_specs=[pl.BlockSpec(memory_space=pltpu.SMEM),
                  pl.BlockSpec((width, tq), lambda h, j: (h, j)),
                  pl.BlockSpec((seq, width), lambda h, j: (0, h)),
                  pl.BlockSpec((width, seq), lambda h, j: (h, 0)),
                  pl.BlockSpec((n_blocks, width), lambda h, j: (0, h)),
                  pl.BlockSpec((seq, HEAD_DIM), const),
                  pl.BlockSpec((tq, tq), const),
                  pl.BlockSpec((tq, tq), const)],
        out_specs=pl.BlockSpec((tq, width), lambda h, j: (j, h)),
        scratch_shapes=[pltpu.VMEM((hps, tq, tq), F32), pltpu.VMEM((hps, tq, tq), F32),
                        pltpu.VMEM((hps, 1, tq), F32), pltpu.VMEM((hps, HEAD_DIM + _SUM_ROWS, tq), F32)],
        compiler_params=_cparams(2),
        name="attn_prompt",
    )(rel_bias.T, qT32, k16, vT16, kbar, onehot, bucket_own, bucket_prev)


_SAMPLE_GROUP = 2


def _attn_sample_kernel(pt_ref, q_ref, kn_ref, vn_ref, rbrows_ref, bkp_ref, *rest, n_pages, dec_seq):
    k_pages = rest[:n_pages]
    v_pages = rest[n_pages:2 * n_pages]
    o_ref, k16, v16, kbar_scr, bprev_scr, bown_scr = rest[2 * n_pages:]
    b = pl.program_id(0)
    past = n_pages * PAGE_SIZE
    n_past_blocks = past // MOBA_BLOCK
    rows = N_HEADS * dec_seq
    pages_per_block = MOBA_BLOCK // PAGE_SIZE

    @pl.when(b == 0)
    def _():
        kbar_scr[...] = jnp.zeros_like(kbar_scr)
        rb = rbrows_ref[...]
        bp = bkp_ref[...]
        tp = jnp.zeros((rows, MOBA_BLOCK), F32)
        for bb in range(NUM_BUCKETS - 1):
            tp = jnp.where(bp == bb, rb[:, bb:bb + 1], tp)
        bprev_scr[...] = tp
        t_row = lax.broadcasted_iota(jnp.int32, (rows, V7X_LANES), 0) & (dec_seq - 1)
        c_col = lax.broadcasted_iota(jnp.int32, (rows, V7X_LANES), 1)
        rel = t_row - c_col
        to = jnp.zeros((rows, V7X_LANES), F32)
        for bb in range(dec_seq):
            to = jnp.where(rel == bb, rb[:, bb:bb + 1], to)
        bown_scr[...] = jnp.where(rel >= 0, to, NEG)

    q = q_ref[...]
    q_rep = jnp.concatenate([q] * N_HEADS, axis=0)
    row_head = lax.broadcasted_iota(jnp.int32, (rows, ATT_WIDTH), 0) >> int(math.log2(dec_seq))
    col_head = lax.broadcasted_iota(jnp.int32, (rows, ATT_WIDTH), 1) >> int(math.log2(HEAD_DIM))
    q_bd = jnp.where(row_head == col_head, q_rep, 0.0)
    q_s = (q_bd * SCALE).astype(BF16)

    def to_head_lanes(pages, dst, n, sums_ref):
        for h in range(N_HEADS):
            cols = slice(h * HEAD_DIM, (h + 1) * HEAD_DIM)
            total = None
            for p in range(n * pages_per_block, (n + 1) * pages_per_block):
                xh = pages[p][pl.ds(h, PAGE_SIZE, stride=N_HEADS), :]
                dst[p * PAGE_SIZE:(p + 1) * PAGE_SIZE, cols] = xh.astype(BF16)
                if sums_ref is not None:
                    cs = jnp.sum(xh, axis=0, keepdims=True)
                    total = cs if total is None else total + cs
            if sums_ref is not None:
                sums_ref[n:n + 1, cols] = total * (1.0 / MOBA_BLOCK)

    def new_rows(ref):
        pad = jnp.zeros((V7X_LANES - dec_seq, ATT_WIDTH), F32)
        return jnp.concatenate([ref[...], pad], axis=0).astype(BF16)

    groups = [slice(g * MOBA_BLOCK, (g + _SAMPLE_GROUP) * MOBA_BLOCK)
              for g in range(0, n_past_blocks, _SAMPLE_GROUP)]
    new_keys = slice(past, past + V7X_LANES)
    raw = []
    for keys in groups:
        for n in range(keys.start // MOBA_BLOCK, keys.stop // MOBA_BLOCK):
            to_head_lanes(k_pages, k16, n, kbar_scr)
        raw.append(lax.dot_general(q_s, k16[keys, :], NT_DIMS, preferred_element_type=F32))
    k16[new_keys, :] = new_rows(kn_ref)
    raw.append(lax.dot_general(q_s, k16[new_keys, :], NT_DIMS, preferred_element_type=F32))

    assert n_past_blocks == V7X_SUBLANES
    q_pad = jnp.concatenate([q_bd, jnp.zeros((V7X_LANES - rows, ATT_WIDTH), F32)], axis=0)
    scores_t = lax.dot_general(kbar_scr[0:n_past_blocks, :], q_pad, NT_DIMS, precision=lax.Precision.HIGHEST,
                               preferred_element_type=F32)
    blk = lax.broadcasted_iota(jnp.int32, scores_t.shape, 0)
    rank = jnp.zeros(scores_t.shape, F32)
    for k in range(1, n_past_blocks):
        other = pltpu.roll(scores_t, k, 0)
        ge = jnp.where(other >= scores_t, 1.0, 0.0)
        gt = jnp.where(other > scores_t, 1.0, 0.0)
        rank = rank + jnp.where(blk >= k, ge, gt)
    sel_t = jnp.where(rank < MOBA_TOPK, 1.0, 0.0)
    eye = jnp.where(lax.broadcasted_iota(jnp.int32, (rows, V7X_LANES), 0)
                    == lax.broadcasted_iota(jnp.int32, (rows, V7X_LANES), 1), 1.0, 0.0).astype(BF16)
    sel_pad = jnp.concatenate([sel_t, jnp.zeros((V7X_LANES - n_past_blocks, V7X_LANES), F32)], axis=0)
    sel = lax.dot_general(eye, sel_pad.astype(BF16), NT_DIMS, preferred_element_type=F32)
    mask_add = jnp.where(sel > 0.5, 0.0, NEG)

    pieces = []
    for keys, piece in zip(groups, raw):
        cols = []
        for i, n in enumerate(range(keys.start // MOBA_BLOCK, keys.stop // MOBA_BLOCK)):
            blk = piece[:, i * MOBA_BLOCK:(i + 1) * MOBA_BLOCK] + mask_add[:, n:n + 1]
            if n == n_past_blocks - 1:
                blk = blk + bprev_scr[...]
            cols.append(blk)
        pieces.append(jnp.concatenate(cols, axis=1))
    pieces.append(raw[-1] + bown_scr[...])
    m = pieces[0].max(axis=-1, keepdims=True)
    for piece in pieces[1:]:
        m = jnp.maximum(m, piece.max(axis=-1, keepdims=True))
    probs = [jnp.exp(piece - m) for piece in pieces]
    l = sum(p.sum(axis=-1, keepdims=True) for p in probs)

    out = None
    for keys, p in zip(groups + [new_keys], probs):
        if keys is new_keys:
            v16[keys, :] = new_rows(vn_ref)
        else:
            for n in range(keys.start // MOBA_BLOCK, keys.stop // MOBA_BLOCK):
                to_head_lanes(v_pages, v16, n, None)
        part = jnp.dot(p.astype(BF16), v16[keys, :], preferred_element_type=F32)
        out = part if out is None else out + part
    out = out / l
    for h in range(N_HEADS):
        sl = slice(h * HEAD_DIM, (h + 1) * HEAD_DIM)
        o_ref[:, sl] = out[h * dec_seq:(h + 1) * dec_seq, sl]


def _attn_sample(q3, k3, v3, cache_k, cache_v, page_table, rel_bias):
    n_seq, dec_seq, _ = q3.shape
    n_pages = page_table.shape[1]
    past = n_pages * PAGE_SIZE
    rows = N_HEADS * dec_seq
    page_rows = PAGE_SIZE * N_HEADS
    ck = cache_k.reshape(-1, HEAD_DIM)
    cv = cache_v.reshape(-1, HEAD_DIM)
    rb_rel = rel_bias.T - rel_bias[NUM_BUCKETS - 1][:, None]
    rbrows = jnp.repeat(rb_rel, dec_seq, axis=0)
    t = jnp.arange(rows, dtype=jnp.int32) % dec_seq
    c = jnp.arange(MOBA_BLOCK, dtype=jnp.int32)
    bucket_prev = _t5_bucket(MOBA_BLOCK + t[:, None] - c[None, :])
    tok = pl.BlockSpec((None, dec_seq, ATT_WIDTH), lambda b, pt: (b, 0, 0))
    const2 = lambda shape: pl.BlockSpec(shape, lambda b, pt: (0, 0))

    def page_spec(p):
        return pl.BlockSpec((page_rows, HEAD_DIM), lambda b, pt: (pt[b, p], 0))

    grid_spec = pltpu.PrefetchScalarGridSpec(
        num_scalar_prefetch=1,
        grid=(n_seq,),
        in_specs=[tok, tok, tok, const2((rows, NUM_BUCKETS)), const2((rows, MOBA_BLOCK))]
                 + [page_spec(p) for p in range(n_pages)] * 2,
        out_specs=tok,
        scratch_shapes=[pltpu.VMEM((past + V7X_LANES, ATT_WIDTH), BF16),
                        pltpu.VMEM((past + V7X_LANES, ATT_WIDTH), BF16),
                        pltpu.VMEM((V7X_LANES, ATT_WIDTH), F32),
                        pltpu.VMEM((rows, MOBA_BLOCK), F32),
                        pltpu.VMEM((rows, V7X_LANES), F32)],
    )
    return pl.pallas_call(
        functools.partial(_attn_sample_kernel, n_pages=n_pages, dec_seq=dec_seq),
        out_shape=jax.ShapeDtypeStruct((n_seq, dec_seq, ATT_WIDTH), F32),
        grid_spec=grid_spec,
        compiler_params=_cparams(1),
        name="attn_sample",
    )(page_table, q3, k3, v3, rbrows, bucket_prev, *([ck] * n_pages), *([cv] * n_pages))


def _merge_kernel(x_ref, gt_ref, u_ref, vn_ref, ga_ref, gb_ref, att_ref, wsp_ref, bsp_ref,
                  wpa_ref, wpb_ref, wo_ref, o_ref, gm_scr, *, tm):
    for c in range(tm // GM_CHUNK):
        rows = slice(c * GM_CHUNK, (c + 1) * GM_CHUNK)
        for g in range(GM_GROUPS):
            cols = slice(g * GM_CHUNK, (g + 1) * GM_CHUNK)
            s = jnp.dot(wsp_ref[g], vn_ref[rows, cols], preferred_element_type=F32) + bsp_ref[:, cols]
            gm_scr[rows, cols] = (u_ref[rows, cols].astype(F32) * s).astype(BF16)
    a = jnp.dot(gm_scr[...], wpa_ref[...], preferred_element_type=F32)
    b = jnp.dot(att_ref[...].astype(BF16), wpb_ref[...], preferred_element_type=F32)
    merged = ga_ref[...].astype(F32) * a + gb_ref[...].astype(F32) * b
    c_out = jnp.dot(merged.astype(BF16), wo_ref[...], preferred_element_type=F32)
    o_ref[...] = x_ref[...] + gt_ref[...] * c_out.reshape(o_ref.shape)


def _merge(x3, mod3, u16, vn16, ga16, gb16, att, w_sp16, b_sp, w_pa16, w_pb16, w_o16, *, prompt, tm):
    n_rows = x3.shape[0] * V7X_SUBLANES
    g_rows = tm // V7X_SUBLANES
    row3 = pl.BlockSpec((g_rows, V7X_SUBLANES, D_MODEL), lambda i: (i, 0, 0))
    seg = pl.BlockSpec((tm, SEG), lambda i: (i, 0))
    gate = pl.BlockSpec((tm, D_MODEL), lambda i: (i, 0))
    resident = lambda shape: pl.BlockSpec(shape, lambda i: (0,) * len(shape), pipeline_mode=pl.Buffered(1))
    return pl.pallas_call(
        functools.partial(_merge_kernel, tm=tm),
        out_shape=jax.ShapeDtypeStruct(x3.shape, F32),
        grid=(n_rows // tm,),
        in_specs=[row3, _mod_spec(g_rows, 2, prompt, 1), seg, seg, gate, gate, seg,
                  resident((GM_GROUPS, GM_CHUNK, GM_CHUNK)), resident((GM_CHUNK, GM_WIDTH)),
                  resident((GM_WIDTH, D_MODEL)), resident((ATT_WIDTH, D_MODEL)), resident((D_MODEL, D_MODEL))],
        out_specs=row3,
        scratch_shapes=[pltpu.VMEM((tm, GM_WIDTH), BF16)],
        compiler_params=_cparams(1),
        name="merge_prompt" if prompt else "merge_sample",
    )(x3, mod3, u16, vn16, ga16, gb16, att, w_sp16, b_sp, w_pa16, w_pb16, w_o16)


def _ffn_kernel(x_ref, sh_ref, sc_ref, gt_ref, g2_ref, wg_ref, wu_ref, wd_ref, o_ref, h_scr, *, tm):
    j = pl.program_id(1)
    last = pl.num_programs(1) - 1
    acc_ref = o_ref

    def chunk(h):
        g = jnp.dot(h, wg_ref[...], preferred_element_type=F32)
        u = jnp.dot(h, wu_ref[...], preferred_element_type=F32)
        a = (g * _sigmoid(g) * u).astype(BF16)
        return jnp.dot(a, wd_ref[...], preferred_element_type=F32)

    @pl.when(j == 0)
    def _():
        x = x_ref[...]
        ms = jnp.mean(x * x, axis=-1, keepdims=True)
        y = x * lax.rsqrt(ms + EPS) * g2_ref[...]
        h = (y * (1.0 + sc_ref[...]) + sh_ref[...]).reshape(tm, D_MODEL).astype(BF16)
        h_scr[...] = h
        acc_ref[...] = chunk(h).reshape(o_ref.shape)

    @pl.when((j > 0) & (j < last))
    def _():
        acc_ref[...] += chunk(h_scr[...]).reshape(o_ref.shape)

    @pl.when(j == last)
    def _():
        total = acc_ref[...] + chunk(h_scr[...]).reshape(o_ref.shape)
        o_ref[...] = x_ref[...] + gt_ref[...] * total


def _ffn(x3, mod3, g2, w_gu16, w_down16, *, prompt, tm, tf):
    n_rows = x3.shape[0] * V7X_SUBLANES
    g_rows = tm // V7X_SUBLANES
    n_chunks = D_FF // tf
    row3 = pl.BlockSpec((g_rows, V7X_SUBLANES, D_MODEL), lambda i, j: (i, 0, 0))
    return pl.pallas_call(
        functools.partial(_ffn_kernel, tm=tm),
        out_shape=jax.ShapeDtypeStruct(x3.shape, F32),
        grid=(n_rows // tm, n_chunks),
        in_specs=[row3, _mod_spec(g_rows, 3, prompt, 2), _mod_spec(g_rows, 4, prompt, 2),
                  _mod_spec(g_rows, 5, prompt, 2),
                  pl.BlockSpec((1, 1, D_MODEL), lambda i, j: (0, 0, 0)),
                  pl.BlockSpec((D_MODEL, tf), lambda i, j: (0, j)),
                  pl.BlockSpec((D_MODEL, tf), lambda i, j: (0, n_chunks + j)),
                  pl.BlockSpec((tf, D_MODEL), lambda i, j: (j, 0))],
        out_specs=row3,
        scratch_shapes=[pltpu.VMEM((tm, D_MODEL), BF16)],
        compiler_params=_cparams(2),
        name="ffn_prompt" if prompt else "ffn_sample",
    )(x3, mod3, mod3, mod3, g2.reshape(1, 1, D_MODEL), w_gu16, w_gu16, w_down16)


def _gm_spatial_weights(gm_ws, gm_bs, dec_seq):
    tril = jnp.tril(jnp.ones((GM_CHUNK, GM_CHUNK), dtype=bool))
    w = jnp.where(tril[None], gm_ws, jnp.zeros_like(gm_ws))
    w_prompt = w.astype(BF16)
    b_prompt = jnp.repeat(gm_bs.T, GM_WIDTH // GM_GROUPS, axis=1)
    reps = GM_CHUNK // dec_seq
    eye = jnp.eye(reps, dtype=F32)
    w_small = w[:, :dec_seq, :dec_seq]
    w_sample = jnp.einsum('ab,gts->gatbs', eye, w_small).reshape(GM_GROUPS, GM_CHUNK, GM_CHUNK).astype(BF16)
    b_sample = jnp.tile(jnp.repeat(gm_bs[:, :dec_seq].T, GM_WIDTH // GM_GROUPS, axis=1), (reps, 1))
    return w_prompt, b_prompt, w_sample, b_sample


def kernel(x_prompt, x_sample, c_prompt, c_sample, cache_k, cache_v, page_table, w_ada, b_ada, norm1_g, w_in,
           q_norm_g, k_norm_g, gm_ln_g, gm_ln_b, gm_ws, gm_bs, w_pa, w_pb, w_o, norm2_g, w_gu, w_down,
           rel_bias):
    depth = w_in.shape[0]
    assert depth == 1, "single-layer step"
    n_seq, dec_seq, _ = x_sample.shape
    batch, seq, _ = x_prompt.shape
    assert batch == 1 and n_seq == _PROMPT_MOD_ROW and dec_seq == V7X_SUBLANES

    pad_rows = V7X_SUBLANES - 1
    c_all = jnp.concatenate([c_sample, c_prompt, jnp.zeros((pad_rows, D_MODEL), F32)], axis=0)
    mod = _modulation(c_all, w_ada[0], b_ada[0])
    mod3 = mod.reshape(mod.shape[0], 1, 6 * D_MODEL)

    w_in16 = w_in[0].astype(BF16)
    w_pa16 = w_pa[0].astype(BF16)
    w_pb16 = w_pb[0].astype(BF16)
    w_o16 = w_o[0].astype(BF16)
    w_gu16 = w_gu[0].astype(BF16)
    w_down16 = w_down[0].astype(BF16)
    wsp_p, bsp_p, wsp_s, bsp_s = _gm_spatial_weights(gm_ws[0], gm_bs[0], dec_seq)

    xp3 = x_prompt.reshape(seq // V7X_SUBLANES, V7X_SUBLANES, D_MODEL)
    xs3 = x_sample

    proj = functools.partial(_inproj, g1=norm1_g[0], w_in16=w_in16, qg=q_norm_g[0], kg=k_norm_g[0],
                             lng=gm_ln_g[0], lnb=gm_ln_b[0])
    tm_prompt = 512
    qT32, k16, vT16, k_out, v_out, u16, vn16, ga16, gb16, kbar_tiles = proj(xp3, mod3, prompt=True, tm=tm_prompt)
    per_tile = tm_prompt // MOBA_BLOCK
    kbar = kbar_tiles.reshape(-1, V7X_SUBLANES, ATT_WIDTH)[:, :per_tile].reshape(-1, ATT_WIDTH)
    sq32, sk32, sv32, sk_out, sv_out, su16, svn16, svn32, sga16, sgb16 = proj(xs3, mod3, prompt=False, tm=512)

    att_p = _attn_prompt(qT32, k16, vT16, kbar, rel_bias)
    tok3 = lambda a: a.reshape(n_seq, dec_seq, ATT_WIDTH)
    att_s = _attn_sample(tok3(sq32), tok3(sk32), tok3(sv32), cache_k[0], cache_v[0], page_table, rel_bias)
    att_s = att_s.reshape(n_seq * dec_seq, ATT_WIDTH)

    mrg = functools.partial(_merge, w_pa16=w_pa16, w_pb16=w_pb16, w_o16=w_o16, tm=256)
    x1p = mrg(xp3, mod3, u16, vn16, ga16, gb16, att_p, wsp_p, bsp_p, prompt=True)
    x1s = mrg(xs3, mod3, su16, svn16, sga16, sgb16, att_s, wsp_s, bsp_s, prompt=False)

    ffn = functools.partial(_ffn, g2=norm2_g[0], w_gu16=w_gu16, w_down16=w_down16, tf=512)
    y_p = ffn(x1p, mod3, prompt=True, tm=1024).reshape(batch, seq, D_MODEL)
    y_s = ffn(x1s, mod3, prompt=False, tm=1024)

    kv_p = lambda a: a.reshape(1, batch, seq, N_HEADS, HEAD_DIM)
    kv_s = lambda a: a.reshape(1, n_seq, dec_seq, N_HEADS, HEAD_DIM)
    return (y_p, y_s, kv_p(k_out), kv_p(v_out), kv_s(sk_out), kv_s(sv_out),
            svn32.reshape(1, n_seq, dec_seq, GM_WIDTH))
```

```python
import functools
import math

import jax
import jax.numpy as jnp
from jax import lax
from jax.experimental import pallas as pl
from jax.experimental.pallas import tpu as pltpu

F32 = jnp.float32
BF16 = jnp.bfloat16

D_MODEL = 2048
N_HEADS = 8
HEAD_DIM = 128
ATT_WIDTH = N_HEADS * HEAD_DIM
GM_WIDTH = 1024
GM_GROUPS = 8
GM_CHUNK = 128
MOBA_BLOCK = 256
MOBA_TOPK = 3
NUM_BUCKETS = 32
MAX_DISTANCE = 128
PAGE_SIZE = 128
D_FF = 5632
EPS = 1e-6
SCALE = HEAD_DIM ** -0.5
LOG2E = math.log2(math.e)

V7X_LANES = 128
V7X_SUBLANES = 8
V7X_VMEM_BYTES = 64 * 1024 * 1024
VMEM_LIMIT = V7X_VMEM_BYTES - 8 * 1024 * 1024

NEG = -1e30
SEG = 1024

NT_DIMS = (((1,), (1,)), ((), ()))


def _cparams(n_axes, vmem=VMEM_LIMIT):
    return pltpu.CompilerParams(dimension_semantics=("arbitrary",) * n_axes, vmem_limit_bytes=vmem)


def _sigmoid(x):
    return 1.0 / (1.0 + jnp.exp(-x))


def _gelu_tanh(x):
    c = math.sqrt(2.0 / math.pi)
    return x * (0.5 * (1.0 + jnp.tanh(c * (x + 0.044715 * (x * x * x)))))


def _t5_bucket(rel):
    n = jnp.maximum(rel, 0)
    max_exact = NUM_BUCKETS // 2
    nf = jnp.maximum(n, 1).astype(F32)
    large = max_exact + (jnp.log(nf / max_exact) / math.log(MAX_DISTANCE / max_exact)
                         * (NUM_BUCKETS - max_exact)).astype(jnp.int32)
    large = jnp.minimum(large, NUM_BUCKETS - 1)
    return jnp.where(n < max_exact, n, large)


def _mod_kernel(c_ref, w_ref, b_ref, o_ref):
    c = c_ref[...]
    s = (c * _sigmoid(c)).astype(BF16)
    o_ref[...] = jnp.dot(s, w_ref[...].astype(BF16), preferred_element_type=F32) + b_ref[...]


def _modulation(c_all, w_ada, b_ada):
    rows = c_all.shape[0]
    n_out = w_ada.shape[1]
    tn = 1024
    return pl.pallas_call(
        _mod_kernel,
        out_shape=jax.ShapeDtypeStruct((rows, n_out), F32),
        grid=(n_out // tn,),
        in_specs=[pl.BlockSpec((rows, D_MODEL), lambda j: (0, 0)),
                  pl.BlockSpec((D_MODEL, tn), lambda j: (0, j)),
                  pl.BlockSpec((1, tn), lambda j: (0, j))],
        out_specs=pl.BlockSpec((rows, tn), lambda j: (0, j)),
        compiler_params=_cparams(1),
        name="mod",
    )(c_all, w_ada, b_ada.reshape(1, n_out))


_SEG_K, _SEG_Q, _SEG_V, _SEG_U, _SEG_VG, _SEG_GA0, _SEG_GA1, _SEG_GB0, _SEG_GB1 = range(9)


def _inproj_kernel(x_ref, sh_ref, sc_ref, g1_ref, w_ref, qg_ref, kg_ref, lng_ref, lnb_ref,
                   *rest, tm, prompt):
    if prompt:
        qT32, k16, vT16, k_out, v_out, u16, vn16, ga16, gb16, kbar, h_scr = rest
        q32 = k32 = v32 = None
    else:
        q32, k32, v32, k_out, v_out, u16, vn16, vn32, ga16, gb16, h_scr = rest
        qT32 = k16 = vT16 = None
    j = pl.program_id(1)

    def normed_input():
        x = x_ref[...]
        ms = jnp.mean(x * x, axis=-1, keepdims=True)
        y = x * lax.rsqrt(ms + EPS) * g1_ref[...]
        h = y * (1.0 + sc_ref[...]) + sh_ref[...]
        return h.reshape(tm, D_MODEL).astype(BF16)

    sub = MOBA_BLOCK
    sub_tiles = [slice(r * sub, (r + 1) * sub) for r in range(tm // sub)]

    def matmul(rows, h=None):
        lhs = h_scr[rows, :] if h is None else h[rows]
        return jnp.dot(lhs, w_ref[...], preferred_element_type=F32)

    def store_heads(h_val, y_of_head, dst32, dst16, dst_out, with_means, dst_t=None):
        for r, rows in enumerate(sub_tiles):
            z = matmul(rows, h_val)
            for h in range(N_HEADS):
                sl = slice(h * HEAD_DIM, (h + 1) * HEAD_DIM)
                yh = y_of_head(z[:, sl])
                if dst32 is not None:
                    dst32[rows, sl] = yh
                if dst16 is not None:
                    dst16[rows, sl] = yh.astype(BF16)
                if dst_t is not None:
                    dst_t[sl, rows] = yh.T.astype(dst_t.dtype)
                if dst_out is not None:
                    dst_out[pl.ds(rows.start * N_HEADS + h, sub, stride=N_HEADS), :] = yh
                if with_means:
                    kbar[r:r + 1, sl] = jnp.mean(yh, axis=0, keepdims=True)

    def head_norm(g_ref):
        def f(zh):
            ms = jnp.mean(zh * zh, axis=-1, keepdims=True)
            return zh * lax.rsqrt(ms + EPS) * g_ref[...]
        return f

    def elementwise(f, dsts):
        for rows in sub_tiles:
            y = f(matmul(rows))
            for dst in dsts:
                dst[rows, :] = y.astype(dst.dtype)

    def gelu_layernorm(z):
        a = _gelu_tanh(z)
        mu = jnp.mean(a, axis=-1, keepdims=True)
        ac = a - mu
        y = ac * lax.rsqrt(jnp.mean(ac * ac, axis=-1, keepdims=True) + EPS)
        return y * lng_ref[...] + lnb_ref[...]

    @pl.when(j == _SEG_K)
    def _():
        h = normed_input()
        h_scr[...] = h
        if prompt:
            kbar[...] = jnp.zeros_like(kbar)
        store_heads(h, head_norm(kg_ref), k32, k16, k_out, prompt)

    @pl.when(j == _SEG_Q)
    def _():
        store_heads(None, head_norm(qg_ref), q32, None, None, False, dst_t=qT32)

    @pl.when(j == _SEG_V)
    def _():
        store_heads(None, lambda zh: zh, v32, None, v_out, False, dst_t=vT16)

    @pl.when(j == _SEG_U)
    def _():
        elementwise(_gelu_tanh, [u16])

    @pl.when(j == _SEG_VG)
    def _():
        elementwise(gelu_layernorm, [vn16] if prompt else [vn16, vn32])

    @pl.when((j == _SEG_GA0) | (j == _SEG_GA1))
    def _():
        elementwise(_sigmoid, [ga16])

    @pl.when((j == _SEG_GB0) | (j == _SEG_GB1))
    def _():
        elementwise(_sigmoid, [gb16])


def _mod_spec(g_rows, chunk, prompt, n_grid):
    if n_grid == 2:
        if prompt:
            return pl.BlockSpec((1, 1, D_MODEL), lambda i, j: (_PROMPT_MOD_ROW, 0, chunk))
        return pl.BlockSpec((g_rows, 1, D_MODEL), lambda i, j: (i, 0, chunk))
    if prompt:
        return pl.BlockSpec((1, 1, D_MODEL), lambda i: (_PROMPT_MOD_ROW, 0, chunk))
    return pl.BlockSpec((g_rows, 1, D_MODEL), lambda i: (i, 0, chunk))


_PROMPT_MOD_ROW = 128


def _w_in_col(j):
    return jnp.where(j < 2, 1 - j, j)


def _inproj(x3, mod3, g1, w_in16, qg, kg, lng, lnb, *, prompt, tm):
    n_rows = x3.shape[0] * V7X_SUBLANES
    g_rows = tm // V7X_SUBLANES
    grid = (n_rows // tm, 9)
    row_blk = lambda i, j: (i, 0)
    f32_seg = jax.ShapeDtypeStruct((n_rows, SEG), F32)
    b16_seg = jax.ShapeDtypeStruct((n_rows, SEG), BF16)
    b16_gate = jax.ShapeDtypeStruct((n_rows, D_MODEL), BF16)
    seg_spec = pl.BlockSpec((tm, SEG), row_blk)
    kv_out = jax.ShapeDtypeStruct((n_rows * N_HEADS, HEAD_DIM), F32)
    kv_spec = pl.BlockSpec((tm * N_HEADS, HEAD_DIM), row_blk)
    ga_spec = pl.BlockSpec((tm, SEG), lambda i, j: (i, jnp.clip(j - _SEG_GA0, 0, 1)))
    gb_spec = pl.BlockSpec((tm, SEG), lambda i, j: (i, jnp.clip(j - _SEG_GB0, 0, 1)))
    if prompt:
        assert tm % MOBA_BLOCK == 0 and tm // MOBA_BLOCK <= V7X_SUBLANES
        t_spec = pl.BlockSpec((SEG, tm), lambda i, j: (0, i))
        out_shape = [jax.ShapeDtypeStruct((SEG, n_rows), F32), b16_seg, jax.ShapeDtypeStruct((SEG, n_rows), BF16),
                     kv_out, kv_out, b16_seg, b16_seg, b16_gate, b16_gate,
                     jax.ShapeDtypeStruct((grid[0] * V7X_SUBLANES, SEG), F32)]
        out_specs = ([t_spec, seg_spec, t_spec] + [kv_spec] * 2 + [seg_spec] * 2
                     + [ga_spec, gb_spec, pl.BlockSpec((V7X_SUBLANES, SEG), row_blk)])
    else:
        out_shape = [f32_seg, f32_seg, f32_seg, kv_out, kv_out, b16_seg, b16_seg, f32_seg, b16_gate, b16_gate]
        out_specs = [seg_spec] * 3 + [kv_spec] * 2 + [seg_spec] * 3 + [ga_spec, gb_spec]
    vec = lambda n: pl.BlockSpec((1, n), lambda i, j: (0, 0))
    return pl.pallas_call(
        functools.partial(_inproj_kernel, tm=tm, prompt=prompt),
        out_shape=out_shape,
        grid=grid,
        in_specs=[pl.BlockSpec((g_rows, V7X_SUBLANES, D_MODEL), lambda i, j: (i, 0, 0)),
                  _mod_spec(g_rows, 0, prompt, 2),
                  _mod_spec(g_rows, 1, prompt, 2),
                  pl.BlockSpec((1, 1, D_MODEL), lambda i, j: (0, 0, 0)),
                  pl.BlockSpec((D_MODEL, SEG), lambda i, j: (0, _w_in_col(j))),
                  vec(HEAD_DIM), vec(HEAD_DIM), vec(GM_WIDTH), vec(GM_WIDTH)],
        out_specs=out_specs,
        scratch_shapes=[pltpu.VMEM((tm, D_MODEL), BF16)],
        compiler_params=_cparams(2),
        name="inproj_prompt" if prompt else "inproj_sample",
    )(x3, mod3, mod3, g1.reshape(1, 1, D_MODEL), w_in16, qg.reshape(1, HEAD_DIM), kg.reshape(1, HEAD_DIM),
      lng.reshape(1, GM_WIDTH), lnb.reshape(1, GM_WIDTH))


def _select_topk(scores, n_valid, axis):
    idx = lax.broadcasted_iota(jnp.int32, scores.shape, axis)
    idx_f = idx.astype(F32)
    cand = idx < n_valid
    sel = jnp.zeros(scores.shape, jnp.bool_)
    for _ in range(MOBA_TOPK):
        s_m = jnp.where(cand, scores, -jnp.inf)
        mx = jnp.max(s_m, axis=axis, keepdims=True)
        first = jnp.min(jnp.where(cand & (s_m == mx), idx_f, 1e9), axis=axis, keepdims=True)
        pick = idx_f == first
        sel = sel | pick
        cand = cand & jnp.logical_not(pick)
    return sel


_FAR_GROUP = 4
_SUM_ROWS = 16
_HEADS_PER_STEP = 8


def _attn_prompt_kernel(rb_ref, qT_ref, k_ref, vT_ref, kbar_ref, e_ref, bko_ref, bkp_ref, o_ref,
                        town, tprev, m_scr, r_scr):
    hp = pl.program_id(0)
    j = pl.program_id(1)
    tq = MOBA_BLOCK
    n_sel_rows = kbar_ref.shape[0]
    heads = range(_HEADS_PER_STEP)
    cols = [slice(i * HEAD_DIM, (i + 1) * HEAD_DIM) for i in heads]

    @pl.when(j == 0)
    def _():
        bo = bko_ref[...]
        bp = bkp_ref[...]
        key = lax.broadcasted_iota(jnp.int32, (tq, tq), 0)
        qry = lax.broadcasted_iota(jnp.int32, (tq, tq), 1)
        for i in heads:
            h = hp * _HEADS_PER_STEP + i
            b_far = rb_ref[h, NUM_BUCKETS - 1]
            to = jnp.zeros((tq, tq), F32)
            tp = jnp.zeros((tq, tq), F32)
            for b in range(NUM_BUCKETS - 1):
                val = (rb_ref[h, b] - b_far) * LOG2E
                to = jnp.where(bo == b, val, to)
                tp = jnp.where(bp == b, val, tp)
            town[i] = jnp.where(key <= qry, to, NEG)
            tprev[i] = tp

    q_sT, q_catT, prev_mask = [], [], []
    for i in heads:
        qT = qT_ref[cols[i], :]
        scoresT = jnp.dot(kbar_ref[:, cols[i]], qT, precision=lax.Precision.HIGHEST,
                          preferred_element_type=F32)
        selT = _select_topk(scoresT, j, axis=0)
        blk = lax.broadcasted_iota(jnp.int32, selT.shape, 0)
        qs = (qT * (SCALE * LOG2E)).astype(BF16)
        aug = jnp.where(selT & (blk < j - 1), 0.0, NEG)
        aug = jnp.concatenate([aug, jnp.full((HEAD_DIM - n_sel_rows, tq), NEG, F32)], axis=0)
        q_sT.append(qs)
        q_catT.append(jnp.concatenate([qs, aug.astype(BF16)], axis=0))
        prev_sel = jnp.max(jnp.where(selT & (blk == j - 1), 1.0, 0.0), axis=0, keepdims=True) > 0.0
        prev_mask.append(jnp.where(prev_sel, 0.0, NEG))

    ones_rows = jnp.ones((_SUM_ROWS, tq), BF16)

    def partial_softmax(logits, where):
        logits = [s.astype(BF16) for s in logits]
        ms = [jnp.max(s, axis=0, keepdims=True) for s in logits]
        ps = [jnp.exp2(s - m) for s, m in zip(logits, ms)]
        rs = []
        for p, (i, start) in zip(ps, where):
            vT_aug = jnp.concatenate([vT_ref[cols[i], pl.ds(start, tq)], ones_rows], axis=0)
            rs.append(jnp.dot(vT_aug, p, preferred_element_type=F32))
        return [(m.astype(F32), r) for m, r in zip(ms, rs)]

    def combine(m_run, r_run, parts):
        m_new = m_run
        for m, _ in parts:
            m_new = jnp.maximum(m_new, m)
        r_new = r_run * jnp.exp2(m_run - m_new)
        for m, r in parts:
            r_new = r_new + r * jnp.exp2(m - m_new)
        return m_new, r_new

    own_start = pl.multiple_of(j * tq, tq)
    prev_start = pl.multiple_of(jnp.maximum(j - 1, 0) * tq, tq)
    logits, where = [], []
    for i in heads:
        logits.append(jnp.dot(k_ref[pl.ds(own_start, tq), cols[i]], q_sT[i], preferred_element_type=F32)
                      + town[i])
        logits.append(jnp.dot(k_ref[pl.ds(prev_start, tq), cols[i]], q_sT[i], preferred_element_type=F32)
                      + (tprev[i] + prev_mask[i]))
        where += [(i, own_start), (i, prev_start)]
    parts = partial_softmax(logits, where)
    for i in heads:
        (m_own, r_own), part_prev = parts[2 * i], parts[2 * i + 1]
        m_scr[i], r_scr[i] = combine(m_own, r_own, [part_prev])

    def far_blocks(first_block, n):
        starts = [pl.multiple_of((first_block + b) * tq, tq) for b in range(n)]
        logits, where = [], []
        for i in heads:
            for start in starts:
                k_cat = jnp.concatenate([k_ref[pl.ds(start, tq), cols[i]], e_ref[pl.ds(start, tq), :]], axis=1)
                logits.append(jnp.dot(k_cat, q_catT[i], preferred_element_type=F32))
                where.append((i, start))
        parts = partial_softmax(logits, where)
        for i in heads:
            m_scr[i], r_scr[i] = combine(m_scr[i], r_scr[i], parts[i * n:(i + 1) * n])

    def far_body(g, carry):
        far_blocks(g * _FAR_GROUP, _FAR_GROUP)
        return carry

    n_far = jnp.maximum(j - 1, 0)
    n_groups = n_far // _FAR_GROUP
    lax.fori_loop(0, n_groups, far_body, 0)
    done = n_groups * _FAR_GROUP
    size = _FAR_GROUP // 2
    while size >= 1:
        @pl.when(((n_far - done) & size) != 0)
        def _(done=done, size=size):
            far_blocks(done, size)
        done = done + ((n_far - n_groups * _FAR_GROUP) & size)
        size //= 2

    for i in heads:
        r = r_scr[i]
        o_ref[:, cols[i]] = (r[:HEAD_DIM] / r[HEAD_DIM:HEAD_DIM + 1]).T.astype(o_ref.dtype)


def _attn_prompt(qT32, k16, vT16, kbar, rel_bias):
    seq = k16.shape[0]
    tq = MOBA_BLOCK
    n_blocks = seq // tq
    assert n_blocks % _FAR_GROUP == 0 and kbar.shape[0] == n_blocks
    pos = jnp.arange(seq, dtype=jnp.int32)
    onehot = (pos[:, None] // tq == jnp.arange(HEAD_DIM, dtype=jnp.int32)[None, :]).astype(BF16)
    r = jnp.arange(tq, dtype=jnp.int32)
    bucket_own = _t5_bucket(r[None, :] - r[:, None])
    bucket_prev = _t5_bucket(tq + r[None, :] - r[:, None])
    const = lambda h, j: (0, 0)
    hps = _HEADS_PER_STEP
    width = hps * HEAD_DIM
    return pl.pallas_call(
        _attn_prompt_kernel,
        out_shape=jax.ShapeDtypeStruct((seq, ATT_WIDTH), BF16),
        grid=(N_HEADS // hps, n_blocks),
        in_specs=[pl.BlockSpec(memory_space=pltpu.SMEM),
                  pl.BlockSpec((width, tq), lambda h, j: (h, j)),
                  pl.BlockSpec((seq, width), lambda h, j: (0, h)),
                  pl.BlockSpec((width, seq), lambda h, j: (h, 0)),
                  pl.BlockSpec((n_blocks, width), lambda h, j: (0, h)),
                  pl.BlockSpec((seq, HEAD_DIM), const),
                  pl.BlockSpec((tq, tq), const),
                  pl.BlockSpec((tq, tq), const)],
        out_specs=pl.BlockSpec((tq, width), lambda h, j: (j, h)),
        scratch_shapes=[pltpu.VMEM((hps, tq, tq), F32), pltpu.VMEM((hps, tq, tq), F32),
                        pltpu.VMEM((hps, 1, tq), F32), pltpu.VMEM((hps, HEAD_DIM + _SUM_ROWS, tq), F32)],
        compiler_params=_cparams(2),
        name="attn_prompt",
    )(rel_bias.T, qT32, k16, vT16, kbar, onehot, bucket_own, bucket_prev)


_SAMPLE_GROUP = 4


def _attn_sample_kernel(pt_ref, q_ref, kn_ref, vn_ref, rbrows_ref, bkp_ref, *rest, n_pages, dec_seq):
    k_pages = rest[:n_pages]
    v_pages = rest[n_pages:2 * n_pages]
    o_ref, k16, v16, kbar_scr, bprev_scr, bown_scr = rest[2 * n_pages:]
    b = pl.program_id(0)
    past = n_pages * PAGE_SIZE
    n_past_blocks = past // MOBA_BLOCK
    rows = N_HEADS * dec_seq
    pages_per_block = MOBA_BLOCK // PAGE_SIZE

    @pl.when(b == 0)
    def _():
        kbar_scr[...] = jnp.zeros_like(kbar_scr)
        rb = rbrows_ref[...]
        bp = bkp_ref[...]
        tp = jnp.zeros((rows, MOBA_BLOCK), F32)
        for bb in range(NUM_BUCKETS - 1):
            tp = jnp.where(bp == bb, rb[:, bb:bb + 1], tp)
        bprev_scr[...] = tp
        t_row = lax.broadcasted_iota(jnp.int32, (rows, V7X_LANES), 0) & (dec_seq - 1)
        c_col = lax.broadcasted_iota(jnp.int32, (rows, V7X_LANES), 1)
        rel = t_row - c_col
        to = jnp.zeros((rows, V7X_LANES), F32)
        for bb in range(dec_seq):
            to = jnp.where(rel == bb, rb[:, bb:bb + 1], to)
        bown_scr[...] = jnp.where(rel >= 0, to, NEG)

    q = q_ref[...]
    q_rep = jnp.concatenate([q] * N_HEADS, axis=0)
    row_head = lax.broadcasted_iota(jnp.int32, (rows, ATT_WIDTH), 0) >> int(math.log2(dec_seq))
    col_head = lax.broadcasted_iota(jnp.int32, (rows, ATT_WIDTH), 1) >> int(math.log2(HEAD_DIM))
    q_bd = jnp.where(row_head == col_head, q_rep, 0.0)
    q_s = (q_bd * SCALE).astype(BF16)

    def to_head_lanes(pages, dst, n, sums_ref):
        for h in range(N_HEADS):
            cols = slice(h * HEAD_DIM, (h + 1) * HEAD_DIM)
            total = None
            for p in range(n * pages_per_block, (n + 1) * pages_per_block):
                xh = pages[p][pl.ds(h, PAGE_SIZE, stride=N_HEADS), :]
                dst[p * PAGE_SIZE:(p + 1) * PAGE_SIZE, cols] = xh.astype(BF16)
                if sums_ref is not None:
                    cs = jnp.sum(xh, axis=0, keepdims=True)
                    total = cs if total is None else total + cs
            if sums_ref is not None:
                sums_ref[n:n + 1, cols] = total * (1.0 / MOBA_BLOCK)

    def new_rows(ref):
        pad = jnp.zeros((V7X_LANES - dec_seq, ATT_WIDTH), F32)
        return jnp.concatenate([ref[...], pad], axis=0).astype(BF16)

    groups = [slice(g * MOBA_BLOCK, (g + _SAMPLE_GROUP) * MOBA_BLOCK)
              for g in range(0, n_past_blocks, _SAMPLE_GROUP)]
    new_keys = slice(past, past + V7X_LANES)
    raw = []
    for keys in groups:
        for n in range(keys.start // MOBA_BLOCK, keys.stop // MOBA_BLOCK):
            to_head_lanes(k_pages, k16, n, kbar_scr)
        raw.append(lax.dot_general(q_s, k16[keys, :], NT_DIMS, preferred_element_type=F32))
    k16[new_keys, :] = new_rows(kn_ref)
    raw.append(lax.dot_general(q_s, k16[new_keys, :], NT_DIMS, preferred_element_type=F32))

    assert n_past_blocks == V7X_SUBLANES
    q_pad = jnp.concatenate([q_bd, jnp.zeros((V7X_LANES - rows, ATT_WIDTH), F32)], axis=0)
    scores_t = lax.dot_general(kbar_scr[0:n_past_blocks, :], q_pad, NT_DIMS, precision=lax.Precision.HIGHEST,
                               preferred_element_type=F32)
    blk = lax.broadcasted_iota(jnp.int32, scores_t.shape, 0)
    rank = jnp.zeros(scores_t.shape, F32)
    for k in range(1, n_past_blocks):
        other = pltpu.roll(scores_t, k, 0)
        ge = jnp.where(other >= scores_t, 1.0, 0.0)
        gt = jnp.where(other > scores_t, 1.0, 0.0)
        rank = rank + jnp.where(blk >= k, ge, gt)
    sel_t = jnp.where(rank < MOBA_TOPK, 1.0, 0.0)
    eye = jnp.where(lax.broadcasted_iota(jnp.int32, (rows, V7X_LANES), 0)
                    == lax.broadcasted_iota(jnp.int32, (rows, V7X_LANES), 1), 1.0, 0.0).astype(BF16)
    sel_pad = jnp.concatenate([sel_t, jnp.zeros((V7X_LANES - n_past_blocks, V7X_LANES), F32)], axis=0)
    sel = lax.dot_general(eye, sel_pad.astype(BF16), NT_DIMS, preferred_element_type=F32)
    mask_add = jnp.where(sel > 0.5, 0.0, NEG)

    pieces = []
    for keys, piece in zip(groups, raw):
        cols = []
        for i, n in enumerate(range(keys.start // MOBA_BLOCK, keys.stop // MOBA_BLOCK)):
            blk = piece[:, i * MOBA_BLOCK:(i + 1) * MOBA_BLOCK] + mask_add[:, n:n + 1]
            if n == n_past_blocks - 1:
                blk = blk + bprev_scr[...]
            cols.append(blk)
        pieces.append(jnp.concatenate(cols, axis=1))
    pieces.append(raw[-1] + bown_scr[...])
    m = pieces[0].max(axis=-1, keepdims=True)
    for piece in pieces[1:]:
        m = jnp.maximum(m, piece.max(axis=-1, keepdims=True))
    probs = [jnp.exp(piece - m) for piece in pieces]
    l = sum(p.sum(axis=-1, keepdims=True) for p in probs)

    out = None
    for keys, p in zip(groups + [new_keys], probs):
        if keys is new_keys:
            v16[keys, :] = new_rows(vn_ref)
        else:
            for n in range(keys.start // MOBA_BLOCK, keys.stop // MOBA_BLOCK):
                to_head_lanes(v_pages, v16, n, None)
        part = jnp.dot(p.astype(BF16), v16[keys, :], preferred_element_type=F32)
        out = part if out is None else out + part
    out = out / l
    for h in range(N_HEADS):
        sl = slice(h * HEAD_DIM, (h + 1) * HEAD_DIM)
        o_ref[:, sl] = out[h * dec_seq:(h + 1) * dec_seq, sl]


def _attn_sample(q3, k3, v3, cache_k, cache_v, page_table, rel_bias):
    n_seq, dec_seq, _ = q3.shape
    n_pages = page_table.shape[1]
    past = n_pages * PAGE_SIZE
    rows = N_HEADS * dec_seq
    page_rows = PAGE_SIZE * N_HEADS
    ck = cache_k.reshape(-1, HEAD_DIM)
    cv = cache_v.reshape(-1, HEAD_DIM)
    rb_rel = rel_bias.T - rel_bias[NUM_BUCKETS - 1][:, None]
    rbrows = jnp.repeat(rb_rel, dec_seq, axis=0)
    t = jnp.arange(rows, dtype=jnp.int32) % dec_seq
    c = jnp.arange(MOBA_BLOCK, dtype=jnp.int32)
    bucket_prev = _t5_bucket(MOBA_BLOCK + t[:, None] - c[None, :])
    tok = pl.BlockSpec((None, dec_seq, ATT_WIDTH), lambda b, pt: (b, 0, 0))
    const2 = lambda shape: pl.BlockSpec(shape, lambda b, pt: (0, 0))

    def page_spec(p):
        return pl.BlockSpec((page_rows, HEAD_DIM), lambda b, pt: (pt[b, p], 0))

    grid_spec = pltpu.PrefetchScalarGridSpec(
        num_scalar_prefetch=1,
        grid=(n_seq,),
        in_specs=[tok, tok, tok, const2((rows, NUM_BUCKETS)), const2((rows, MOBA_BLOCK))]
                 + [page_spec(p) for p in range(n_pages)] * 2,
        out_specs=tok,
        scratch_shapes=[pltpu.VMEM((past + V7X_LANES, ATT_WIDTH), BF16),
                        pltpu.VMEM((past + V7X_LANES, ATT_WIDTH), BF16),
                        pltpu.VMEM((V7X_LANES, ATT_WIDTH), F32),
                        pltpu.VMEM((rows, MOBA_BLOCK), F32),
                        pltpu.VMEM((rows, V7X_LANES), F32)],
    )
    return pl.pallas_call(
        functools.partial(_attn_sample_kernel, n_pages=n_pages, dec_seq=dec_seq),
        out_shape=jax.ShapeDtypeStruct((n_seq, dec_seq, ATT_WIDTH), F32),
        grid_spec=grid_spec,
        compiler_params=_cparams(1),
        name="attn_sample",
    )(page_table, q3, k3, v3, rbrows, bucket_prev, *([ck] * n_pages), *([cv] * n_pages))


def _merge_kernel(x_ref, gt_ref, u_ref, vn_ref, ga_ref, gb_ref, att_ref, wsp_ref, bsp_ref,
                  wpa_ref, wpb_ref, wo_ref, o_ref, gm_scr, *, tm):
    for c in range(tm // GM_CHUNK):
        rows = slice(c * GM_CHUNK, (c + 1) * GM_CHUNK)
        for g in range(GM_GROUPS):
            cols = slice(g * GM_CHUNK, (g + 1) * GM_CHUNK)
            s = jnp.dot(wsp_ref[g], vn_ref[rows, cols], preferred_element_type=F32) + bsp_ref[:, cols]
            gm_scr[rows, cols] = (u_ref[rows, cols].astype(F32) * s).astype(BF16)
    a = jnp.dot(gm_scr[...], wpa_ref[...], preferred_element_type=F32)
    b = jnp.dot(att_ref[...].astype(BF16), wpb_ref[...], preferred_element_type=F32)
    merged = ga_ref[...].astype(F32) * a + gb_ref[...].astype(F32) * b
    c_out = jnp.dot(merged.astype(BF16), wo_ref[...], preferred_element_type=F32)
    o_ref[...] = x_ref[...] + gt_ref[...] * c_out.reshape(o_ref.shape)


def _merge(x3, mod3, u16, vn16, ga16, gb16, att, w_sp16, b_sp, w_pa16, w_pb16, w_o16, *, prompt, tm):
    n_rows = x3.shape[0] * V7X_SUBLANES
    g_rows = tm // V7X_SUBLANES
    row3 = pl.BlockSpec((g_rows, V7X_SUBLANES, D_MODEL), lambda i: (i, 0, 0))
    seg = pl.BlockSpec((tm, SEG), lambda i: (i, 0))
    gate = pl.BlockSpec((tm, D_MODEL), lambda i: (i, 0))
    resident = lambda shape: pl.BlockSpec(shape, lambda i: (0,) * len(shape), pipeline_mode=pl.Buffered(1))
    return pl.pallas_call(
        functools.partial(_merge_kernel, tm=tm),
        out_shape=jax.ShapeDtypeStruct(x3.shape, F32),
        grid=(n_rows // tm,),
        in_specs=[row3, _mod_spec(g_rows, 2, prompt, 1), seg, seg, gate, gate, seg,
                  resident((GM_GROUPS, GM_CHUNK, GM_CHUNK)), resident((GM_CHUNK, GM_WIDTH)),
                  resident((GM_WIDTH, D_MODEL)), resident((ATT_WIDTH, D_MODEL)), resident((D_MODEL, D_MODEL))],
        out_specs=row3,
        scratch_shapes=[pltpu.VMEM((tm, GM_WIDTH), BF16)],
        compiler_params=_cparams(1),
        name="merge_prompt" if prompt else "merge_sample",
    )(x3, mod3, u16, vn16, ga16, gb16, att, w_sp16, b_sp, w_pa16, w_pb16, w_o16)


def _ffn_kernel(x_ref, sh_ref, sc_ref, gt_ref, g2_ref, wg_ref, wu_ref, wd_ref, o_ref, h_scr, *, tm):
    j = pl.program_id(1)
    last = pl.num_programs(1) - 1
    acc_ref = o_ref

    def chunk(h):
        g = jnp.dot(h, wg_ref[...], preferred_element_type=F32)
        u = jnp.dot(h, wu_ref[...], preferred_element_type=F32)
        a = (g * _sigmoid(g) * u).astype(BF16)
        return jnp.dot(a, wd_ref[...], preferred_element_type=F32)

    @pl.when(j == 0)
    def _():
        x = x_ref[...]
        ms = jnp.mean(x * x, axis=-1, keepdims=True)
        y = x * lax.rsqrt(ms + EPS) * g2_ref[...]
        h = (y * (1.0 + sc_ref[...]) + sh_ref[...]).reshape(tm, D_MODEL).astype(BF16)
        h_scr[...] = h
        acc_ref[...] = chunk(h).reshape(o_ref.shape)

    @pl.when((j > 0) & (j < last))
    def _():
        acc_ref[...] += chunk(h_scr[...]).reshape(o_ref.shape)

    @pl.when(j == last)
    def _():
        total = acc_ref[...] + chunk(h_scr[...]).reshape(o_ref.shape)
        o_ref[...] = x_ref[...] + gt_ref[...] * total


def _ffn(x3, mod3, g2, w_gu16, w_down16, *, prompt, tm, tf):
    n_rows = x3.shape[0] * V7X_SUBLANES
    g_rows = tm // V7X_SUBLANES
    n_chunks = D_FF // tf
    row3 = pl.BlockSpec((g_rows, V7X_SUBLANES, D_MODEL), lambda i, j: (i, 0, 0))
    return pl.pallas_call(
        functools.partial(_ffn_kernel, tm=tm),
        out_shape=jax.ShapeDtypeStruct(x3.shape, F32),
        grid=(n_rows // tm, n_chunks),
        in_specs=[row3, _mod_spec(g_rows, 3, prompt, 2), _mod_spec(g_rows, 4, prompt, 2),
                  _mod_spec(g_rows, 5, prompt, 2),
                  pl.BlockSpec((1, 1, D_MODEL), lambda i, j: (0, 0, 0)),
                  pl.BlockSpec((D_MODEL, tf), lambda i, j: (0, j)),
                  pl.BlockSpec((D_MODEL, tf), lambda i, j: (0, n_chunks + j)),
                  pl.BlockSpec((tf, D_MODEL), lambda i, j: (j, 0))],
        out_specs=row3,
        scratch_shapes=[pltpu.VMEM((tm, D_MODEL), BF16)],
        compiler_params=_cparams(2),
        name="ffn_prompt" if prompt else "ffn_sample",
    )(x3, mod3, mod3, mod3, g2.reshape(1, 1, D_MODEL), w_gu16, w_gu16, w_down16)


def _gm_spatial_weights(gm_ws, gm_bs, dec_seq):
    tril = jnp.tril(jnp.ones((GM_CHUNK, GM_CHUNK), dtype=bool))
    w = jnp.where(tril[None], gm_ws, jnp.zeros_like(gm_ws))
    w_prompt = w.astype(BF16)
    b_prompt = jnp.repeat(gm_bs.T, GM_WIDTH // GM_GROUPS, axis=1)
    reps = GM_CHUNK // dec_seq
    eye = jnp.eye(reps, dtype=F32)
    w_small = w[:, :dec_seq, :dec_seq]
    w_sample = jnp.einsum('ab,gts->gatbs', eye, w_small).reshape(GM_GROUPS, GM_CHUNK, GM_CHUNK).astype(BF16)
    b_sample = jnp.tile(jnp.repeat(gm_bs[:, :dec_seq].T, GM_WIDTH // GM_GROUPS, axis=1), (reps, 1))
    return w_prompt, b_prompt, w_sample, b_sample


def kernel(x_prompt, x_sample, c_prompt, c_sample, cache_k, cache_v, page_table, w_ada, b_ada, norm1_g, w_in,
           q_norm_g, k_norm_g, gm_ln_g, gm_ln_b, gm_ws, gm_bs, w_pa, w_pb, w_o, norm2_g, w_gu, w_down,
           rel_bias):
    depth = w_in.shape[0]
    assert depth == 1, "single-layer step"
    n_seq, dec_seq, _ = x_sample.shape
    batch, seq, _ = x_prompt.shape
    assert batch == 1 and n_seq == _PROMPT_MOD_ROW and dec_seq == V7X_SUBLANES

    pad_rows = V7X_SUBLANES - 1
    c_all = jnp.concatenate([c_sample, c_prompt, jnp.zeros((pad_rows, D_MODEL), F32)], axis=0)
    mod = _modulation(c_all, w_ada[0], b_ada[0])
    mod3 = mod.reshape(mod.shape[0], 1, 6 * D_MODEL)

    w_in16 = w_in[0].astype(BF16)
    w_pa16 = w_pa[0].astype(BF16)
    w_pb16 = w_pb[0].astype(BF16)
    w_o16 = w_o[0].astype(BF16)
    w_gu16 = w_gu[0].astype(BF16)
    w_down16 = w_down[0].astype(BF16)
    wsp_p, bsp_p, wsp_s, bsp_s = _gm_spatial_weights(gm_ws[0], gm_bs[0], dec_seq)

    xp3 = x_prompt.reshape(seq // V7X_SUBLANES, V7X_SUBLANES, D_MODEL)
    xs3 = x_sample

    proj = functools.partial(_inproj, g1=norm1_g[0], w_in16=w_in16, qg=q_norm_g[0], kg=k_norm_g[0],
                             lng=gm_ln_g[0], lnb=gm_ln_b[0])
    tm_prompt = 512
    qT32, k16, vT16, k_out, v_out, u16, vn16, ga16, gb16, kbar_tiles = proj(xp3, mod3, prompt=True, tm=tm_prompt)
    per_tile = tm_prompt // MOBA_BLOCK
    kbar = kbar_tiles.reshape(-1, V7X_SUBLANES, ATT_WIDTH)[:, :per_tile].reshape(-1, ATT_WIDTH)
    sq32, sk32, sv32, sk_out, sv_out, su16, svn16, svn32, sga16, sgb16 = proj(xs3, mod3, prompt=False, tm=512)

    att_p = _attn_prompt(qT32, k16, vT16, kbar, rel_bias)
    tok3 = lambda a: a.reshape(n_seq, dec_seq, ATT_WIDTH)
    att_s = _attn_sample(tok3(sq32), tok3(sk32), tok3(sv32), cache_k[0], cache_v[0], page_table, rel_bias)
    att_s = att_s.reshape(n_seq * dec_seq, ATT_WIDTH)

    mrg = functools.partial(_merge, w_pa16=w_pa16, w_pb16=w_pb16, w_o16=w_o16, tm=256)
    x1p = mrg(xp3, mod3, u16, vn16, ga16, gb16, att_p, wsp_p, bsp_p, prompt=True)
    x1s = mrg(xs3, mod3, su16, svn16, sga16, sgb16, att_s, wsp_s, bsp_s, prompt=False)

    ffn = functools.partial(_ffn, g2=norm2_g[0], w_gu16=w_gu16, w_down16=w_down16, tf=512)
    y_p = ffn(x1p, mod3, prompt=True, tm=1024).reshape(batch, seq, D_MODEL)
    y_s = ffn(x1s, mod3, prompt=False, tm=1024)

    kv_p = lambda a: a.reshape(1, batch, seq, N_HEADS, HEAD_DIM)
    kv_s = lambda a: a.reshape(1, n_seq, dec_seq, N_HEADS, HEAD_DIM)
    return (y_p, y_s, kv_p(k_out), kv_p(v_out), kv_s(sk_out), kv_s(sv_out),
            svn32.reshape(1, n_seq, dec_seq, GM_WIDTH))
```
